```python
import math
import jax, jax.numpy as jnp
from jax import lax
import numpy as np

D_MODEL = 2048
BATCH = 2
SEQ = 4096
DEPTH = 2

CHUNK = 64
Q_BLOCK = 128
EPS = 1e-6

SB_HEADS = 4
SB_HEAD_DIM = 128
SB_WIDTH = SB_HEADS * SB_HEAD_DIM

MLA_HEADS = 8
MLA_NOPE = 128
MLA_ROPE = 64
MLA_QK = MLA_NOPE + MLA_ROPE
MLA_V = 128
MLA_WIDTH = MLA_HEADS * MLA_V
Q_LORA = 512
KV_LORA = 512
ROPE_THETA = 10000.0

RW_HEADS = 8
RW_HEAD_DIM = 64
RW_WIDTH = RW_HEADS * RW_HEAD_DIM
DECAY_LORA = 64
ICL_LORA = 64
RW_GN_EPS = 64e-5
RW_SHIFT_WIDTH = 3 * RW_WIDTH + DECAY_LORA + ICL_LORA

N_BRANCH = 3
SB_COLS = 4 * SB_WIDTH
MLA_COLS = Q_LORA + KV_LORA + MLA_ROPE + MLA_WIDTH
RW_COLS = RW_SHIFT_WIDTH + RW_WIDTH
GATE_COLS = N_BRANCH * D_MODEL
N_IN = SB_COLS + MLA_COLS + RW_COLS + GATE_COLS

kernel_name = "hybrid_sb_mla_rwkv7_gated_merge"


def rms_norm(x, g, eps=EPS):
    xf = x.astype(jnp.float32)
    y = xf * lax.rsqrt(jnp.mean(xf * xf, axis=-1, keepdims=True) + eps)
    return (y * g.astype(jnp.float32)).astype(x.dtype)


def split_cols(t, sizes):
    out, start = [], 0
    for n in sizes:
        out.append(t[..., start:start + n])
        start += n
    return out


def rope(x, pos):
    half = x.shape[-1] // 2
    freqs = ROPE_THETA ** (-jnp.arange(half, dtype=jnp.float32) / half)
    ang = pos.astype(jnp.float32)[:, None] * freqs[None, :]
    cos = jnp.cos(ang)[None, :, None, :]
    sin = jnp.sin(ang)[None, :, None, :]
    x1, x2 = x[..., :half], x[..., half:]
    return jnp.concatenate([x1 * cos - x2 * sin, x1 * sin + x2 * cos], axis=-1).astype(x.dtype)


def stick_breaking_attention(q, k, v):
    S, Dh = q.shape[1], q.shape[-1]
    scale = Dh ** -0.5
    outs = []
    for i in range(S // Q_BLOCK):
        q0, q1 = i * Q_BLOCK, (i + 1) * Q_BLOCK
        z = jnp.einsum('bqhd,bkhd->bhqk', q[:, q0:q1], k[:, :q1]).astype(jnp.float32) * scale
        t_pos = q0 + jnp.arange(Q_BLOCK)
        s_pos = jnp.arange(q1)
        causal = s_pos[None, :] < t_pos[:, None]
        log_beta = jax.nn.log_sigmoid(z)
        log_fail = jnp.where(causal, log_beta - z, 0.0)
        between = lax.cumsum(log_fail, axis=3, reverse=True) - log_fail
        w = jnp.where(causal, jnp.exp(log_beta + between), 0.0)
        outs.append(jnp.einsum('bhqk,bkhd->bqhd', w.astype(v.dtype), v[:, :q1]))
    return jnp.concatenate(outs, axis=1)


def chunk_causal_softmax_attention(q, k, v):
    S, Dh = q.shape[1], q.shape[-1]
    scale = Dh ** -0.5
    outs = []
    for i in range(S // Q_BLOCK):
        q0, q1 = i * Q_BLOCK, (i + 1) * Q_BLOCK
        s = jnp.einsum('bqhd,bkhd->bhqk', q[:, q0:q1], k[:, :q1]).astype(jnp.float32) * scale
        t_chunk = (q0 + jnp.arange(Q_BLOCK)) // CHUNK
        s_chunk = jnp.arange(q1) // CHUNK
        mask = s_chunk[None, :] <= t_chunk[:, None]
        p = jax.nn.softmax(jnp.where(mask, s, -jnp.inf), axis=-1)
        outs.append(jnp.einsum('bhqk,bkhd->bqhd', p.astype(v.dtype), v[:, :q1]))
    return jnp.concatenate(outs, axis=1)


def rwkv7_scan(r, decay, k, v, a_vec, b_vec):
    B, S, H, N = r.shape

    def step(state, inp):
        r_t, w_t, k_t, v_t, a_t, b_t = inp
        sa = jnp.einsum('bhij,bhj->bhi', state, a_t)
        state = (state * w_t[:, :, None, :] + sa[..., None] * b_t[:, :, None, :]
                 + v_t[..., None] * k_t[:, :, None, :])
        return state, jnp.einsum('bhij,bhj->bhi', state, r_t)

    xs = tuple(jnp.moveaxis(t, 1, 0) for t in (r, decay, k, v, a_vec, b_vec))
    state0 = jnp.zeros((B, H, N, N), jnp.float32)
    _, out = lax.scan(step, state0, xs)
    return jnp.moveaxis(out, 0, 1)


def rwkv7_mixer(r, k, v, w_down, a_down, w0, w_up, a0, a_up, k_k, k_a, r_k, gn_g, gn_b):
    B, S, _ = r.shape
    dt = r.dtype
    f32 = jnp.float32
    heads = lambda t: t.astype(f32).reshape(B, S, RW_HEADS, RW_HEAD_DIM)
    w_log = -jax.nn.softplus(-(w0 + jnp.tanh(w_down) @ w_up).astype(f32)) - 0.5
    decay = jnp.exp(-jnp.exp(w_log))
    a = jax.nn.sigmoid((a0 + a_down @ a_up).astype(f32))
    kk = heads(k * k_k)
    kk = kk / jnp.maximum(jnp.sqrt(jnp.sum(kk * kk, axis=-1, keepdims=True)), 1e-12)
    k_mod = k.astype(f32) * (1.0 + (a - 1.0) * k_a.astype(f32))
    rh, kh, vh, ah = heads(r), heads(k_mod), heads(v), heads(a)
    o = rwkv7_scan(rh, heads(decay), kh, vh, -kk, kk * ah)
    mu = jnp.mean(o, axis=-1, keepdims=True)
    var = jnp.mean(jnp.square(o - mu), axis=-1, keepdims=True)
    o = ((o - mu) * lax.rsqrt(var + RW_GN_EPS)).reshape(B, S, RW_WIDTH)
    o = o * gn_g.astype(f32) + gn_b.astype(f32)
    bonus = jnp.sum(rh * kh * r_k.astype(f32), axis=-1, keepdims=True) * vh
    return (o + bonus.reshape(B, S, RW_WIDTH)).astype(dt)


def setup_inputs(seed: int = 0) -> dict:
    key = jax.random.key(seed)
    ks = iter(jax.random.split(key, 32))
    L = DEPTH
    nrm = lambda shape, scale: jax.random.normal(next(ks), shape, jnp.float32) * scale
    gain = lambda shape: 1.0 + 0.1 * jax.random.normal(next(ks), shape, jnp.float32)
    return {
        "x": nrm((BATCH, SEQ, D_MODEL), 1.0),
        "norm_g": gain((L, D_MODEL)),
        "w_in": nrm((L, D_MODEL, N_IN), D_MODEL ** -0.5),
        "mla_q_norm_g": gain((L, Q_LORA)),
        "mla_kv_norm_g": gain((L, KV_LORA)),
        "mla_w_uq": nrm((L, Q_LORA, MLA_HEADS * MLA_QK), Q_LORA ** -0.5),
        "mla_w_ukv": nrm((L, KV_LORA, MLA_HEADS * (MLA_NOPE + MLA_V)), KV_LORA ** -0.5),
        "mla_qn_g": gain((L, MLA_QK)),
        "mla_kn_g": gain((L, MLA_QK)),
        "rw_mu": jax.random.uniform(next(ks), (L, RW_SHIFT_WIDTH), jnp.float32),
        "rw_w0": jax.random.uniform(next(ks), (L, RW_WIDTH), jnp.float32, minval=-6.0, maxval=-1.0),
        "rw_w_up": nrm((L, DECAY_LORA, RW_WIDTH), 0.5 * DECAY_LORA ** -0.5),
        "rw_a0": nrm((L, RW_WIDTH), 0.5),
        "rw_a_up": nrm((L, ICL_LORA, RW_WIDTH), ICL_LORA ** -0.5),
        "rw_k_k": 0.85 + 0.05 * jax.random.normal(next(ks), (L, RW_WIDTH), jnp.float32),
        "rw_k_a": 1.0 + 0.05 * jax.random.normal(next(ks), (L, RW_WIDTH), jnp.float32),
        "rw_r_k": nrm((L, RW_HEADS, RW_HEAD_DIM), 0.1),
        "rw_gn_g": gain((L, RW_WIDTH)),
        "rw_gn_b": nrm((L, RW_WIDTH), 0.02),
        "w_br_sb": nrm((L, SB_WIDTH, D_MODEL), SB_WIDTH ** -0.5),
        "w_br_mla": nrm((L, MLA_WIDTH, D_MODEL), MLA_WIDTH ** -0.5),
        "w_br_rw": nrm((L, RW_WIDTH, D_MODEL), RW_WIDTH ** -0.5),
        "w_out": nrm((L, D_MODEL, D_MODEL), D_MODEL ** -0.5),
    }


def reference(x, norm_g, w_in, mla_q_norm_g, mla_kv_norm_g, mla_w_uq, mla_w_ukv, mla_qn_g, mla_kn_g,
              rw_mu, rw_w0, rw_w_up, rw_a0, rw_a_up, rw_k_k, rw_k_a, rw_r_k, rw_gn_g, rw_gn_b,
              w_br_sb, w_br_mla, w_br_rw, w_out):
    B, S, _ = x.shape
    pos = jnp.arange(S)
    for l in range(DEPTH):
        h = rms_norm(x, norm_g[l])
        p = h @ w_in[l]
        p_sb, p_mla, p_rw, p_gate = split_cols(p, (SB_COLS, MLA_COLS, RW_COLS, GATE_COLS))

        sq, sk, sv, sg = split_cols(p_sb, (SB_WIDTH,) * 4)
        sbh = lambda t: t.reshape(B, S, SB_HEADS, SB_HEAD_DIM)
        y_sb = stick_breaking_attention(sbh(sq), sbh(sk), sbh(sv)).reshape(B, S, SB_WIDTH)
        y_sb = y_sb * jax.nn.silu(sg)

        c_q, c_kv, k_rope, mg = split_cols(p_mla, (Q_LORA, KV_LORA, MLA_ROPE, MLA_WIDTH))
        q = (rms_norm(c_q, mla_q_norm_g[l]) @ mla_w_uq[l]).reshape(B, S, MLA_HEADS, MLA_QK)
        kv = (rms_norm(c_kv, mla_kv_norm_g[l]) @ mla_w_ukv[l]).reshape(B, S, MLA_HEADS, MLA_NOPE + MLA_V)
        k_nope, v_m = kv[..., :MLA_NOPE], kv[..., MLA_NOPE:]
        k = jnp.concatenate(
            [k_nope, jnp.broadcast_to(k_rope[:, :, None, :], (B, S, MLA_HEADS, MLA_ROPE))], axis=-1)
        q = rms_norm(q, mla_qn_g[l])
        k = rms_norm(k, mla_kn_g[l])
        q = jnp.concatenate([q[..., :MLA_NOPE], rope(q[..., MLA_NOPE:], pos)], axis=-1)
        k = jnp.concatenate([k[..., :MLA_NOPE], rope(k[..., MLA_NOPE:], pos)], axis=-1)
        y_mla = chunk_causal_softmax_attention(q, k, v_m).reshape(B, S, MLA_WIDTH)
        y_mla = y_mla * jax.nn.silu(mg)

        rw_main, rg = split_cols(p_rw, (RW_SHIFT_WIDTH, RW_WIDTH))
        prev = jnp.pad(rw_main, ((0, 0), (1, 0), (0, 0)))[:, :-1]
        rw_main = rw_main + rw_mu[l] * (prev - rw_main)
        rr, rk, rv, wd, ad = split_cols(rw_main, (RW_WIDTH,) * 3 + (DECAY_LORA, ICL_LORA))
        y_rw = rwkv7_mixer(rr, rk, rv, wd, ad, rw_w0[l], rw_w_up[l], rw_a0[l], rw_a_up[l],
                           rw_k_k[l], rw_k_a[l], rw_r_k[l], rw_gn_g[l], rw_gn_b[l])
        y_rw = y_rw * jax.nn.silu(rg)

        g_sb, g_mla, g_rw = split_cols(jax.nn.sigmoid(p_gate), (D_MODEL,) * N_BRANCH)
        merged = (g_sb * (y_sb @ w_br_sb[l]) + g_mla * (y_mla @ w_br_mla[l])
                  + g_rw * (y_rw @ w_br_rw[l]))
        x = x + merged @ w_out[l]
    return x
```

```python
import functools
import math

import jax
import jax.numpy as jnp
import numpy as np
from jax import lax
from jax.experimental import pallas as pl
from jax.experimental.pallas import tpu as pltpu

F32 = jnp.float32
BF16 = jnp.bfloat16

D_MODEL = 2048
EPS = 1e-6
CHUNK = 64

SB_HEADS = 4
SB_HEAD_DIM = 128
SB_WIDTH = SB_HEADS * SB_HEAD_DIM

MLA_HEADS = 8
MLA_NOPE = 128
MLA_ROPE = 64
MLA_QK = MLA_NOPE + MLA_ROPE
MLA_V = 128
MLA_WIDTH = MLA_HEADS * MLA_V
Q_LORA = 512
KV_LORA = 512
ROPE_THETA = 10000.0
HALF_ROPE = MLA_ROPE // 2

RW_HEADS = 8
RW_HEAD_DIM = 64
RW_WIDTH = RW_HEADS * RW_HEAD_DIM
DECAY_LORA = 64
ICL_LORA = 64
RW_GN_EPS = 64e-5
RW_CHUNK = 64

LANES = 128
VMEM_LIMIT = 56 * 1024 * 1024

SEC_SB = 0
SEC_MLA = 2048
SEC_RG = 4096
SEC_RW = 4608
SEC_GATE = 6144
SEC_SMALL = 12288
N_PROJ = SEC_SMALL + 256

NEG_BIG = -1e30


def _dot(a, b, precision=None):
    return jnp.dot(a, b, preferred_element_type=F32, precision=precision)


def _dot_nt(a, b, precision=None):
    return lax.dot_general(a, b, (((1,), (1,)), ((), ())), preferred_element_type=F32,
                           precision=precision)


def _dot_tn(a, b, precision=None):
    return lax.dot_general(a, b, (((0,), (0,)), ((), ())), preferred_element_type=F32,
                           precision=precision)


def _split_bf16(x):
    hi = x.astype(BF16)
    lo = (x - hi.astype(F32)).astype(BF16)
    return hi, lo


def _softplus(z):
    return jnp.maximum(z, 0.0) + jnp.log1p(jnp.exp(-jnp.abs(z)))


def _sigmoid(z):
    return 1.0 / (1.0 + jnp.exp(-z))


def _silu(z):
    return z * _sigmoid(z)


def _params(semantics):
    return pltpu.CompilerParams(dimension_semantics=semantics, vmem_limit_bytes=VMEM_LIMIT)


def _in_proj_kernel(x_ref, g_ref, w_ref, o_ref, h_ref):
    @pl.when(pl.program_id(1) == 0)
    def _():
        x = x_ref[...]
        rs = lax.rsqrt(jnp.mean(x * x, axis=-1, keepdims=True) + EPS)
        h_ref[...] = (x * rs * g_ref[...]).astype(BF16)

    o_ref[...] = _dot(h_ref[...], w_ref[...])


def _in_proj(x2, g, w, tm=512, tn=1792):
    m = x2.shape[0]
    return pl.pallas_call(
        _in_proj_kernel,
        out_shape=jax.ShapeDtypeStruct((m, N_PROJ), F32),
        grid=(m // tm, N_PROJ // tn),
        in_specs=[
            pl.BlockSpec((tm, D_MODEL), lambda i, j: (i, 0)),
            pl.BlockSpec((1, D_MODEL), lambda i, j: (0, 0)),
            pl.BlockSpec((D_MODEL, tn), lambda i, j: (0, j)),
        ],
        out_specs=pl.BlockSpec((tm, tn), lambda i, j: (i, j)),
        scratch_shapes=[pltpu.VMEM((tm, D_MODEL), BF16)],
        compiler_params=_params(("parallel", "arbitrary")),
        name="in_proj",
    )(x2, g, w)


def _sb_attn_kernel(q_ref, k_ref, v_ref, g_ref, o_ref, *, t):
    i = pl.program_id(2)
    scale = SB_HEAD_DIM ** -0.5
    q = q_ref[...].astype(BF16)
    row = lax.broadcasted_iota(jnp.int32, (t, t), 0)
    col = lax.broadcasted_iota(jnp.int32, (t, t), 1)
    after = (row > col).astype(BF16)

    def body(n, carry):
        run, acc = carry
        j = i - n
        start = pl.multiple_of(j * t, t)
        kb = k_ref[pl.ds(start, t), :].astype(BF16)
        vb = v_ref[pl.ds(start, t), :].astype(BF16)
        z = _dot_nt(q, kb) * scale
        causal = (col + (j - i) * t) < row
        sp = _softplus(z)
        log_fail = jnp.where(causal, -sp, 0.0)
        hi, lo = _split_bf16(log_fail)
        between = _dot(hi, after) + _dot(lo, after) + run
        w = jnp.where(causal, jnp.exp(z - sp + between), 0.0)
        acc = acc + _dot(w.astype(BF16), vb)
        run = run + jnp.sum(log_fail, axis=-1, keepdims=True)
        return run, acc

    run0 = jnp.zeros((t, 1), F32)
    acc0 = jnp.zeros((t, SB_HEAD_DIM), F32)
    _, acc = lax.fori_loop(0, i + 1, body, (run0, acc0))
    o_ref[...] = (acc * _silu(g_ref[...])).astype(o_ref.dtype)


def _sb_attn(p, batch, seq, t=256):
    nq = seq // t
    sblk = seq // t
    q_spec = pl.BlockSpec((t, LANES), lambda b, h, i: (b * sblk + i, h))
    k_spec = pl.BlockSpec((seq, LANES), lambda b, h, i: (b, SB_HEADS + h))
    v_spec = pl.BlockSpec((seq, LANES), lambda b, h, i: (b, 2 * SB_HEADS + h))
    g_spec = pl.BlockSpec((t, LANES), lambda b, h, i: (b * sblk + i, 3 * SB_HEADS + h))
    return pl.pallas_call(
        functools.partial(_sb_attn_kernel, t=t),
        out_shape=jax.ShapeDtypeStruct((batch * seq, SB_WIDTH), BF16),
        grid=(batch, SB_HEADS, nq),
        in_specs=[q_spec, k_spec, v_spec, g_spec],
        out_specs=pl.BlockSpec((t, LANES), lambda b, h, i: (b * sblk + i, h)),
        compiler_params=_params(("parallel", "parallel", "arbitrary")),
        name="sb_attn",
    )(p, p, p, p)


def _mla_prep_kernel(cq_ref, ckv_ref, sm_ref, cos_ref, sin_ref, gq_ref, gkv_ref, wuq_ref, wukv_ref,
                     gqn_ref, gqr_ref, gkn_ref, gkr_ref, q_ref, k_ref, v_ref):
    lane = lax.broadcasted_iota(jnp.int32, (1, LANES), 1)
    first = (lane % MLA_ROPE) < HALF_ROPE
    cos = cos_ref[...]
    sin = sin_ref[...]

    def latent_norm(c_ref, g_ref):
        c = c_ref[...]
        rs = lax.rsqrt(jnp.mean(c * c, axis=-1, keepdims=True) + EPS)
        return (c * rs * g_ref[...]).astype(BF16)

    def rotary(y):
        return y * cos + pltpu.roll(y, MLA_ROPE, axis=1) * sin

    def head_sums(sq):
        s_first = jnp.sum(jnp.where(first, sq, 0.0), axis=-1, keepdims=True)
        s_second = jnp.sum(jnp.where(first, 0.0, sq), axis=-1, keepdims=True)
        return s_first, s_second

    qfull = _dot(latent_norm(cq_ref, gq_ref), wuq_ref[...])
    kvfull = _dot(latent_norm(ckv_ref, gkv_ref), wukv_ref[...])
    v_ref[...] = kvfull[:, MLA_HEADS * MLA_NOPE:].astype(BF16)

    kr = sm_ref[...][:, :LANES]
    kr_ss, _ = head_sums(kr * kr)
    kr_rot = rotary(kr * gkr_ref[...])

    for pair in range(MLA_HEADS // 2):
        qr = qfull[:, MLA_HEADS * MLA_NOPE + pair * LANES:MLA_HEADS * MLA_NOPE + (pair + 1) * LANES]
        qr_ss = head_sums(qr * qr)
        q_rs, k_rs = [], []
        for e in range(2):
            h = 2 * pair + e
            qn = qfull[:, h * MLA_NOPE:(h + 1) * MLA_NOPE]
            kn = kvfull[:, h * MLA_NOPE:(h + 1) * MLA_NOPE]
            qs = lax.rsqrt((jnp.sum(qn * qn, axis=-1, keepdims=True) + qr_ss[e]) / MLA_QK + EPS)
            ks = lax.rsqrt((jnp.sum(kn * kn, axis=-1, keepdims=True) + kr_ss) / MLA_QK + EPS)
            q_rs.append(qs)
            k_rs.append(ks)
            q_ref[:, 2 * h * LANES:(2 * h + 1) * LANES] = (qn * qs * gqn_ref[...]).astype(BF16)
            k_ref[:, 2 * h * LANES:(2 * h + 1) * LANES] = (kn * ks * gkn_ref[...]).astype(BF16)
        q_rot = rotary(qr * jnp.where(first, q_rs[0], q_rs[1]) * gqr_ref[...])
        k_rot = (kr_rot * jnp.where(first, k_rs[0], k_rs[1])).astype(BF16)
        for e in range(2):
            h = 2 * pair + e
            own = first if e == 0 else jnp.logical_not(first)
            q_ref[:, (2 * h + 1) * LANES:(2 * h + 2) * LANES] = jnp.where(own, q_rot, 0.0).astype(BF16)
            k_ref[:, (2 * h + 1) * LANES:(2 * h + 2) * LANES] = k_rot


def _mla_prep(p, cos, sin, gq, gkv, wuq, wukv, gqn, gqr, gkn, gkr, seq, tm=512):
    m = p.shape[0]
    sblk = seq // tm
    row = lambda width, cb: pl.BlockSpec((tm, width), lambda i: (i, cb))
    const = lambda shape: pl.BlockSpec(shape, lambda i: (0, 0))
    pos = pl.BlockSpec((tm, LANES), lambda i: (i % sblk, 0))
    qk_width = MLA_HEADS * 2 * LANES
    return pl.pallas_call(
        _mla_prep_kernel,
        out_shape=(jax.ShapeDtypeStruct((m, qk_width), BF16),
                   jax.ShapeDtypeStruct((m, qk_width), BF16),
                   jax.ShapeDtypeStruct((m, MLA_WIDTH), BF16)),
        grid=(m // tm,),
        in_specs=[
            row(Q_LORA, SEC_MLA // Q_LORA),
            row(KV_LORA, SEC_MLA // KV_LORA + 1),
            row(256, SEC_SMALL // 256),
            pos, pos,
            const((1, Q_LORA)), const((1, KV_LORA)),
            const(wuq.shape), const(wukv.shape),
            const((1, LANES)), const((1, LANES)), const((1, LANES)), const((1, LANES)),
        ],
        out_specs=(row(qk_width, 0), row(qk_width, 0), row(MLA_WIDTH, 0)),
        compiler_params=_params(("parallel",)),
        name="mla_prep",
    )(p, p, p, cos, sin, gq, gkv, wuq, wukv, gqn, gqr, gkn, gkr)


def _mla_attn_kernel(q_ref, k_ref, v_ref, g_ref, o_ref, *, t):
    i = pl.program_id(2)
    scale = MLA_QK ** -0.5
    q = q_ref[...]
    row_chunk = lax.broadcasted_iota(jnp.int32, (t, t), 0) // CHUNK
    col_chunk = lax.broadcasted_iota(jnp.int32, (t, t), 1) // CHUNK

    def body(j, carry):
        m, l, acc = carry
        start = pl.multiple_of(j * t, t)
        kb = k_ref[pl.ds(start, t), :]
        vb = v_ref[pl.ds(start, t), :]
        s = _dot_nt(q, kb) * scale
        s = jnp.where(col_chunk + (j - i) * (t // CHUNK) <= row_chunk, s, NEG_BIG)
        m_new = jnp.maximum(m, jnp.max(s, axis=-1, keepdims=True))
        alpha = jnp.exp(m - m_new)
        pr = jnp.exp(s - m_new)
        l = alpha * l + jnp.sum(pr, axis=-1, keepdims=True)
        acc = alpha * acc + _dot(pr.astype(BF16), vb)
        return m_new, l, acc

    m0 = jnp.full((t, 1), NEG_BIG, F32)
    l0 = jnp.zeros((t, 1), F32)
    acc0 = jnp.zeros((t, MLA_V), F32)
    _, l, acc = lax.fori_loop(0, i + 1, body, (m0, l0, acc0))
    o_ref[...] = (acc / l * _silu(g_ref[...])).astype(o_ref.dtype)


def _mla_attn(q, k, v, p, batch, seq, t=256):
    nq = seq // t
    return pl.pallas_call(
        functools.partial(_mla_attn_kernel, t=t),
        out_shape=jax.ShapeDtypeStruct((batch * seq, MLA_WIDTH), BF16),
        grid=(batch, MLA_HEADS, nq),
        in_specs=[
            pl.BlockSpec((t, 2 * LANES), lambda b, h, i: (b * nq + i, h)),
            pl.BlockSpec((seq, 2 * LANES), lambda b, h, i: (b, h)),
            pl.BlockSpec((seq, LANES), lambda b, h, i: (b, h)),
            pl.BlockSpec((t, LANES), lambda b, h, i: (b * nq + i, (SEC_MLA + 1024) // LANES + h)),
        ],
        out_specs=pl.BlockSpec((t, LANES), lambda b, h, i: (b * nq + i, h)),
        compiler_params=_params(("parallel", "parallel", "arbitrary")),
        name="mla_attn",
    )(q, k, v, p)


def _rw_prep_kernel(cur_ref, prev_ref, smc_ref, smp_ref, mu_ref, mus_ref, wlora_ref, w0_ref, a0_ref,
                    kk_ref, ka_ref, ones_ref, r_ref, lw_ref, k_ref, v_ref, na_ref, b_ref,
                    *, tm, seq):
    i = pl.program_id(0)
    at_start = (i * tm) % seq == 0
    row = lax.broadcasted_iota(jnp.int32, (tm, 1), 0)

    def shifted(c_ref, p_ref, mu):
        cur = c_ref[...]
        last = jnp.where(at_start, 0.0, p_ref[...][7:8, :])
        prev = jnp.where(row == 0, last, pltpu.roll(cur, 1, axis=0))
        return cur + mu * (prev - cur)

    main = shifted(cur_ref, prev_ref, mu_ref[...])
    small = shifted(smc_ref, smp_ref, mus_ref[...])[:, LANES:]
    r = main[:, :RW_WIDTH]
    k = main[:, RW_WIDTH:2 * RW_WIDTH]
    v = main[:, 2 * RW_WIDTH:]

    lane = lax.broadcasted_iota(jnp.int32, (1, LANES), 1)
    lora_in = jnp.where(lane < DECAY_LORA, jnp.tanh(small), small).astype(BF16)
    lora = _dot(lora_in, wlora_ref[...])
    w_log = -_softplus(-(w0_ref[...] + lora[:, :RW_WIDTH])) - 0.5
    a = _sigmoid(a0_ref[...] + lora[:, RW_WIDTH:])

    kk = k * kk_ref[...]
    hi, lo = _split_bf16(kk * kk)
    ss = _dot(hi, ones_ref[...]) + _dot(lo, ones_ref[...])
    kk = kk / jnp.maximum(jnp.sqrt(ss), 1e-12)

    r_ref[...] = r
    lw_ref[...] = -jnp.exp(w_log)
    k_ref[...] = k * (1.0 + (a - 1.0) * ka_ref[...])
    v_ref[...] = v
    na_ref[...] = -kk
    b_ref[...] = kk * a


def _rw_prep(p, mu, mus, wlora, w0, a0, k_k, k_a, ones_bd, seq, tm=512):
    m = p.shape[0]
    vec = lambda width: pl.BlockSpec((1, width), lambda i: (0, 0))
    out = jax.ShapeDtypeStruct((m, RW_WIDTH), F32)
    main_w = 3 * RW_WIDTH
    prev_blk = lambda i: (jnp.maximum(i * (tm // 8) - 1, 0))
    return pl.pallas_call(
        functools.partial(_rw_prep_kernel, tm=tm, seq=seq),
        out_shape=(out,) * 6,
        grid=(m // tm,),
        in_specs=[
            pl.BlockSpec((tm, main_w), lambda i: (i, SEC_RW // main_w)),
            pl.BlockSpec((8, main_w), lambda i: (prev_blk(i), SEC_RW // main_w)),
            pl.BlockSpec((tm, 256), lambda i: (i, SEC_SMALL // 256)),
            pl.BlockSpec((8, 256), lambda i: (prev_blk(i), SEC_SMALL // 256)),
            vec(main_w), vec(256),
            pl.BlockSpec(wlora.shape, lambda i: (0, 0)),
            vec(RW_WIDTH), vec(RW_WIDTH), vec(RW_WIDTH), vec(RW_WIDTH),
            pl.BlockSpec((RW_WIDTH, RW_WIDTH), lambda i: (0, 0)),
        ],
        out_specs=(pl.BlockSpec((tm, RW_WIDTH), lambda i: (i, 0)),) * 6,
        compiler_params=_params(("parallel",)),
        name="rw_prep",
    )(p, p, p, p, mu, mus, wlora, w0, a0, k_k, k_a, ones_bd)


_HI = lax.Precision.HIGHEST


def _rw_scan_kernel(r_ref, lw_ref, k_ref, v_ref, na_ref, b_ref, g_ref, rk_ref, gng_ref, gnb_ref,
                    o_ref, s_ref):
    c = RW_CHUNK
    n = 2 * c

    @pl.when(pl.program_id(2) == 0)
    def _():
        s_ref[...] = jnp.zeros_like(s_ref)

    r = r_ref[...]
    lw = lw_ref[...]
    k = k_ref[...]
    v = v_ref[...]
    lane = lax.broadcasted_iota(jnp.int32, (1, LANES), 1)
    first = lane < RW_HEAD_DIM

    ti = lax.broadcasted_iota(jnp.int32, (c, c), 0)
    si = lax.broadcasted_iota(jnp.int32, (c, c), 1)
    cum = _dot((si <= ti).astype(F32), lw, _HI)
    total = cum[c - 1:c, :]
    p_inv = jnp.exp(-cum)
    p_end = jnp.exp(total - cum)

    def stack(x):
        return jnp.concatenate([jnp.where(first, x, 0.0), jnp.where(first, 0.0, x)], axis=0)

    at = stack(na_ref[...] * jnp.exp(cum - lw))
    rt = stack(r * jnp.exp(cum))
    bt = stack(b_ref[...] * p_inv)
    kt = stack(k * p_inv)
    bh = stack(b_ref[...] * p_end)
    kh = stack(k * p_end)
    vs = stack(v)

    g = _dot_nt(jnp.concatenate([at, rt], axis=0), jnp.concatenate([bt, kt], axis=0), _HI)
    ri = lax.broadcasted_iota(jnp.int32, (n, n), 0)
    ci = lax.broadcasted_iota(jnp.int32, (n, n), 1)
    strict = (ci % c) < (ri % c)
    incl = (ci % c) <= (ri % c)
    l_ab = jnp.where(strict, g[:n, :n], 0.0)
    l_ak = jnp.where(strict, g[:n, n:], 0.0)
    m_rb = jnp.where(incl, g[n:, :n], 0.0)
    m_rk = jnp.where(incl, g[n:, n:], 0.0)

    eye = ri == ci
    x = l_ab
    tinv = jnp.where(eye, 1.0, 0.0) + x
    for _ in range(int(math.log2(c)) - 1):
        x = _dot(x, x, _HI)
        tinv = tinv + _dot(tinv, x, _HI)

    wt = _dot(tinv, at, _HI)
    u0 = _dot(tinv, _dot(l_ak, vs, _HI), _HI)
    rh = rt + _dot(m_rb, wt, _HI)
    o0 = _dot(m_rb, u0, _HI) + _dot(m_rk, vs, _HI)
    a_mat = jnp.where(eye, jnp.exp(total), 0.0) + _dot_tn(wt, bh, _HI)
    b_mat = _dot_tn(u0, bh, _HI) + _dot_tn(vs, kh, _HI)

    s0 = s_ref[...]
    o = _dot_nt(rh[:c] + rh[c:], s0, _HI) + o0[:c] + o0[c:]
    s_ref[...] = _dot(s0, a_mat, _HI) + b_mat

    def head_sum(y):
        s_first = jnp.sum(jnp.where(first, y, 0.0), axis=-1, keepdims=True)
        s_second = jnp.sum(jnp.where(first, 0.0, y), axis=-1, keepdims=True)
        return jnp.where(first, s_first, s_second)

    mu = head_sum(o) / RW_HEAD_DIM
    d = o - mu
    var = head_sum(d * d) / RW_HEAD_DIM
    normed = d * lax.rsqrt(var + RW_GN_EPS) * gng_ref[...] + gnb_ref[...]
    bonus = head_sum(r * k * rk_ref[...]) * v
    o_ref[...] = ((normed + bonus) * _silu(g_ref[...])).astype(o_ref.dtype)


def _rw_scan(r, lw, k, v, na, b, p, r_k, gn_g, gn_b, batch, seq):
    c = RW_CHUNK
    nc = seq // c
    pairs = RW_WIDTH // LANES
    tok = pl.BlockSpec((c, LANES), lambda bi, pr, ci: (bi * nc + ci, pr))
    vec = pl.BlockSpec((1, LANES), lambda bi, pr, ci: (0, pr))
    gate = pl.BlockSpec((c, LANES), lambda bi, pr, ci: (bi * nc + ci, SEC_RG // LANES + pr))
    return pl.pallas_call(
        _rw_scan_kernel,
        out_shape=jax.ShapeDtypeStruct((batch * seq, RW_WIDTH), BF16),
        grid=(batch, pairs, nc),
        in_specs=[tok] * 6 + [gate, vec, vec, vec],
        out_specs=tok,
        scratch_shapes=[pltpu.VMEM((LANES, LANES), F32)],
        compiler_params=_params(("parallel", "parallel", "arbitrary")),
        name="rw_scan",
    )(r, lw, k, v, na, b, p, r_k, gn_g, gn_b)


def _merge_kernel(ysb_ref, ymla_ref, yrw_ref, wsb_ref, wmla_ref, wrw_ref, g1_ref, g2_ref, g3_ref, o_ref):
    o = _sigmoid(g1_ref[...]) * _dot(ysb_ref[...], wsb_ref[...])
    o = o + _sigmoid(g2_ref[...]) * _dot(ymla_ref[...], wmla_ref[...])
    o = o + _sigmoid(g3_ref[...]) * _dot(yrw_ref[...], wrw_ref[...])
    o_ref[...] = o.astype(o_ref.dtype)


def _merge(y_sb, y_mla, y_rw, w_sb, w_mla, w_rw, p, tm=512, tn=512):
    m = p.shape[0]
    rows = lambda width: pl.BlockSpec((tm, width), lambda i, j: (i, 0))
    wcol = lambda depth: pl.BlockSpec((depth, tn), lambda i, j: (0, j))
    gate = lambda br: pl.BlockSpec((tm, tn), lambda i, j: (i, (SEC_GATE + br * D_MODEL) // tn + j))
    return pl.pallas_call(
        _merge_kernel,
        out_shape=jax.ShapeDtypeStruct((m, D_MODEL), BF16),
        grid=(m // tm, D_MODEL // tn),
        in_specs=[rows(SB_WIDTH), rows(MLA_WIDTH), rows(RW_WIDTH),
                  wcol(SB_WIDTH), wcol(MLA_WIDTH), wcol(RW_WIDTH),
                  gate(0), gate(1), gate(2)],
        out_specs=pl.BlockSpec((tm, tn), lambda i, j: (i, j)),
        compiler_params=_params(("parallel", "arbitrary")),
        name="merge",
    )(y_sb, y_mla, y_rw, w_sb, w_mla, w_rw, p, p, p)


def _out_proj_kernel(x_ref, m_ref, w_ref, o_ref):
    o_ref[...] = x_ref[...] + _dot(m_ref[...], w_ref[...])


def _out_proj(x2, merged, w, tm=512, tn=1024):
    m = x2.shape[0]
    return pl.pallas_call(
        _out_proj_kernel,
        out_shape=jax.ShapeDtypeStruct((m, D_MODEL), F32),
        grid=(m // tm, D_MODEL // tn),
        in_specs=[
            pl.BlockSpec((tm, tn), lambda i, j: (i, j)),
            pl.BlockSpec((tm, D_MODEL), lambda i, j: (i, 0)),
            pl.BlockSpec((D_MODEL, tn), lambda i, j: (0, j)),
        ],
        out_specs=pl.BlockSpec((tm, tn), lambda i, j: (i, j)),
        compiler_params=_params(("parallel", "arbitrary")),
        name="out_proj",
    )(x2, merged, w)


def _reorder_w_in(w):
    kr = 3072
    mg = kr + MLA_ROPE
    rw = mg + MLA_WIDTH
    lo = rw + 3 * RW_WIDTH
    rg = lo + DECAY_LORA + ICL_LORA
    gt = rg + RW_WIDTH
    x1 = w[:, kr:kr + HALF_ROPE]
    x2 = w[:, kr + HALF_ROPE:mg]
    return jnp.concatenate([
        w[:, :kr], w[:, mg:rw], w[:, rg:gt], w[:, rw:lo], w[:, gt:],
        x1, x1, x2, x2, w[:, lo:rg]], axis=1)


def _pair_rope_cols(t):
    lead = t.shape[:-2]
    t = t.reshape(lead + (MLA_HEADS // 2, 2, 2, HALF_ROPE))
    t = jnp.swapaxes(t, -3, -2)
    return t.reshape(lead + (MLA_HEADS // 2 * LANES,))


def _pair_rope_gain(g):
    g1, g2 = g[:HALF_ROPE], g[HALF_ROPE:]
    return jnp.concatenate([g1, g1, g2, g2])[None, :]


def _rope_tables(seq):
    freqs = ROPE_THETA ** (-jnp.arange(HALF_ROPE, dtype=F32) / HALF_ROPE)
    ang = jnp.arange(seq, dtype=F32)[:, None] * freqs[None, :]
    c, s = jnp.cos(ang), jnp.sin(ang)
    return jnp.concatenate([c, c, c, c], axis=1), jnp.concatenate([-s, -s, s, s], axis=1)


def kernel(x, norm_g, w_in, mla_q_norm_g, mla_kv_norm_g, mla_w_uq, mla_w_ukv, mla_qn_g, mla_kn_g,
           rw_mu, rw_w0, rw_w_up, rw_a0, rw_a_up, rw_k_k, rw_k_a, rw_r_k, rw_gn_g, rw_gn_b,
           w_br_sb, w_br_mla, w_br_rw, w_out):
    batch, seq, _ = x.shape
    depth = w_in.shape[0]
    x2 = x.reshape(batch * seq, D_MODEL)
    cos, sin = _rope_tables(seq)
    head_ones = jnp.kron(jnp.eye(RW_HEADS, dtype=F32), jnp.ones((RW_HEAD_DIM, RW_HEAD_DIM), F32)).astype(BF16)
    row = lambda t: t[None, :]

    for l in range(depth):
        p = _in_proj(x2, row(norm_g[l]), _reorder_w_in(w_in[l]).astype(BF16))

        y_sb = _sb_attn(p, batch, seq)

        uq = mla_w_uq[l].reshape(Q_LORA, MLA_HEADS, MLA_QK)
        wuq = jnp.concatenate([uq[:, :, :MLA_NOPE].reshape(Q_LORA, -1), _pair_rope_cols(uq[:, :, MLA_NOPE:])],
                              axis=1).astype(BF16)
        ukv = mla_w_ukv[l].reshape(KV_LORA, MLA_HEADS, MLA_NOPE + MLA_V)
        wukv = jnp.concatenate([ukv[:, :, :MLA_NOPE].reshape(KV_LORA, -1),
                                ukv[:, :, MLA_NOPE:].reshape(KV_LORA, -1)], axis=1).astype(BF16)
        q, k, v = _mla_prep(p, cos, sin, row(mla_q_norm_g[l]), row(mla_kv_norm_g[l]), wuq, wukv,
                            row(mla_qn_g[l][:MLA_NOPE]), _pair_rope_gain(mla_qn_g[l][MLA_NOPE:]),
                            row(mla_kn_g[l][:MLA_NOPE]), _pair_rope_gain(mla_kn_g[l][MLA_NOPE:]), seq)
        y_mla = _mla_attn(q, k, v, p, batch, seq)

        mu = rw_mu[l]
        mu_small = jnp.concatenate([jnp.zeros((LANES,), F32), mu[3 * RW_WIDTH:]])
        zeros = jnp.zeros((DECAY_LORA, RW_WIDTH), F32)
        wlora = jnp.concatenate([jnp.concatenate([rw_w_up[l], zeros], axis=1),
                                 jnp.concatenate([zeros, rw_a_up[l]], axis=1)], axis=0).astype(BF16)
        rr, lw, rk, rv, na, rb = _rw_prep(p, row(mu[:3 * RW_WIDTH]), row(mu_small), wlora, row(rw_w0[l]),
                                          row(rw_a0[l]), row(rw_k_k[l]), row(rw_k_a[l]), head_ones, seq)
        y_rw = _rw_scan(rr, lw, rk, rv, na, rb, p, rw_r_k[l].reshape(1, RW_WIDTH), row(rw_gn_g[l]),
                        row(rw_gn_b[l]), batch, seq)

        merged = _merge(y_sb, y_mla, y_rw, w_br_sb[l].astype(BF16), w_br_mla[l].astype(BF16),
                        w_br_rw[l].astype(BF16), p)
        x2 = _out_proj(x2, merged, w_out[l].astype(BF16))

    return x2.reshape(batch, seq, D_MODEL)
```

```python
import functools
import math

import jax
import jax.numpy as jnp
from jax import lax
from jax.experimental import pallas as pl
from jax.experimental.pallas import tpu as pltpu

F32 = jnp.float32
BF16 = jnp.bfloat16

D_MODEL = 2048
EPS = 1e-6
CHUNK = 64

SB_HEADS = 4
SB_HEAD_DIM = 128
SB_WIDTH = SB_HEADS * SB_HEAD_DIM

MLA_HEADS = 8
MLA_NOPE = 128
MLA_ROPE = 64
MLA_QK = MLA_NOPE + MLA_ROPE
MLA_V = 128
MLA_WIDTH = MLA_HEADS * MLA_V
Q_LORA = 512
KV_LORA = 512
ROPE_THETA = 10000.0
HALF_ROPE = MLA_ROPE // 2

RW_HEADS = 8
RW_HEAD_DIM = 64
RW_WIDTH = RW_HEADS * RW_HEAD_DIM
DECAY_LORA = 64
ICL_LORA = 64
RW_GN_EPS = 64e-5
RW_CHUNK = 64

LANES = 128
VMEM_LIMIT = 56 * 1024 * 1024

SEC_SB = 0
SEC_MLA = 2048
SEC_RG = 4096
SEC_RW = 4608
SEC_GATE = 6144
SEC_SMALL = 12288
N_PROJ = SEC_SMALL + 256

NEG_BIG = -1e30

ATTN_TILE = 256
SB_GROUP = 2
MLA_GROUP = 4


def _dot(a, b, precision=None):
    return jnp.dot(a, b, preferred_element_type=F32, precision=precision)


def _dot_nt(a, b, precision=None):
    return lax.dot_general(a, b, (((1,), (1,)), ((), ())), preferred_element_type=F32,
                           precision=precision)


def _dot_tn(a, b, precision=None):
    return lax.dot_general(a, b, (((0,), (0,)), ((), ())), preferred_element_type=F32,
                           precision=precision)


def _split_bf16(x):
    hi = x.astype(BF16)
    lo = (x - hi.astype(F32)).astype(BF16)
    return hi, lo


def _softplus(z):
    return jnp.maximum(z, 0.0) + jnp.log(1.0 + jnp.exp(-jnp.abs(z)))


def _sigmoid(z):
    return 1.0 / (1.0 + jnp.exp(-z))


def _silu(z):
    return z * _sigmoid(z)


def _params(semantics):
    return pltpu.CompilerParams(dimension_semantics=semantics, vmem_limit_bytes=VMEM_LIMIT)


def _in_proj_kernel(x_ref, g_ref, w_ref, o_ref, h_ref):
    @pl.when(pl.program_id(1) == 0)
    def _():
        x = x_ref[...]
        rs = lax.rsqrt(jnp.mean(x * x, axis=-1, keepdims=True) + EPS)
        h_ref[...] = (x * rs * g_ref[...]).astype(BF16)

    o_ref[...] = _dot(h_ref[...], w_ref[...])


def _in_proj(x2, g, w, tm=512, tn=1792):
    m = x2.shape[0]
    return pl.pallas_call(
        _in_proj_kernel,
        out_shape=jax.ShapeDtypeStruct((m, N_PROJ), F32),
        grid=(m // tm, N_PROJ // tn),
        in_specs=[
            pl.BlockSpec((tm, D_MODEL), lambda i, j: (i, 0)),
            pl.BlockSpec((1, D_MODEL), lambda i, j: (0, 0)),
            pl.BlockSpec((D_MODEL, tn), lambda i, j: (0, j)),
        ],
        out_specs=pl.BlockSpec((tm, tn), lambda i, j: (i, j)),
        scratch_shapes=[pltpu.VMEM((tm, D_MODEL), BF16)],
        compiler_params=_params(("parallel", "arbitrary")),
        name="in_proj",
    )(x2, g, w)


def _sb_attn_kernel(q_ref, k_ref, v_ref, g_ref, o_ref, *, t, heads):
    i = pl.program_id(2)
    scale = SB_HEAD_DIM ** -0.5
    row = lax.broadcasted_iota(jnp.int32, (t, t), 0)
    col = lax.broadcasted_iota(jnp.int32, (t, t), 1)
    after = (row > col).astype(BF16)
    causal = col < row
    qs = [q_ref[:, h * LANES:(h + 1) * LANES].astype(BF16) for h in range(heads)]

    def step(j, carry, diagonal):
        start = pl.multiple_of(j * t, t)
        head = lambda h: slice(h * LANES, (h + 1) * LANES)
        zs = [_dot_nt(qs[h], k_ref[pl.ds(start, t), head(h)].astype(BF16)) for h in range(heads)]
        stage = []
        for h in range(heads):
            z = zs[h] * scale
            sp = _softplus(z)
            log_fail = jnp.where(causal, -sp, 0.0) if diagonal else -sp
            hi, lo = _split_bf16(log_fail)
            stage.append((z - sp, log_fail, hi, lo))
        sums = [_dot(hi, after) + _dot(lo, after) for _, _, hi, lo in stage]
        ws = []
        for h in range(heads):
            w = jnp.exp(stage[h][0] + sums[h] + carry[h][0])
            if diagonal:
                w = jnp.where(causal, w, 0.0)
            ws.append(w.astype(BF16))
        out = []
        for h in range(heads):
            run, acc = carry[h]
            vb = v_ref[pl.ds(start, t), head(h)].astype(BF16)
            out.append((run + jnp.sum(stage[h][1], axis=-1, keepdims=True), acc + _dot(ws[h], vb)))
        return tuple(out)

    init = tuple((jnp.zeros((t, 1), F32), jnp.zeros((t, SB_HEAD_DIM), F32)) for _ in range(heads))
    carry = step(i, init, True)
    carry = lax.fori_loop(0, i, lambda n, c: step(i - 1 - n, c, False), carry)
    for h in range(heads):
        hs = slice(h * LANES, (h + 1) * LANES)
        o_ref[:, hs] = (carry[h][1] * _silu(g_ref[:, hs])).astype(o_ref.dtype)


def _sb_attn(p, batch, seq, t=ATTN_TILE, heads=SB_GROUP):
    nq = seq // t
    w = heads * LANES
    sec = SB_WIDTH // w
    tile = lambda s: pl.BlockSpec((t, w), lambda b, h, i: (b * nq + i, s * sec + h))
    full = lambda s: pl.BlockSpec((seq, w), lambda b, h, i: (b, s * sec + h))
    return pl.pallas_call(
        functools.partial(_sb_attn_kernel, t=t, heads=heads),
        out_shape=jax.ShapeDtypeStruct((batch * seq, SB_WIDTH), BF16),
        grid=(batch, SB_HEADS // heads, nq),
        in_specs=[tile(0), full(1), full(2), tile(3)],
        out_specs=pl.BlockSpec((t, w), lambda b, h, i: (b * nq + i, h)),
        compiler_params=_params(("parallel", "parallel", "arbitrary")),
        name="sb_attn",
    )(p, p, p, p)


def _mla_prep_kernel(cq_ref, ckv_ref, sm_ref, cos_ref, sin_ref, gq_ref, gkv_ref, wuq_ref, wukv_ref,
                     gqn_ref, gqr_ref, gkn_ref, gkr_ref, q_ref, k_ref, v_ref):
    lane = lax.broadcasted_iota(jnp.int32, (1, LANES), 1)
    first = (lane % MLA_ROPE) < HALF_ROPE
    cos = cos_ref[...]
    sin = sin_ref[...]

    def latent_norm(c_ref, g_ref):
        c = c_ref[...]
        rs = lax.rsqrt(jnp.mean(c * c, axis=-1, keepdims=True) + EPS)
        return (c * rs * g_ref[...]).astype(BF16)

    def rotary(y):
        return y * cos + pltpu.roll(y, MLA_ROPE, axis=1) * sin

    def head_sums(sq):
        s_first = jnp.sum(jnp.where(first, sq, 0.0), axis=-1, keepdims=True)
        s_second = jnp.sum(jnp.where(first, 0.0, sq), axis=-1, keepdims=True)
        return s_first, s_second

    qfull = _dot(latent_norm(cq_ref, gq_ref), wuq_ref[...])
    kvfull = _dot(latent_norm(ckv_ref, gkv_ref), wukv_ref[...])
    v_ref[...] = kvfull[:, MLA_HEADS * MLA_NOPE:].astype(BF16)

    kr = sm_ref[...][:, :LANES]
    kr_ss, _ = head_sums(kr * kr)
    kr_rot = rotary(kr * gkr_ref[...])

    for pair in range(MLA_HEADS // 2):
        qr = qfull[:, MLA_HEADS * MLA_NOPE + pair * LANES:MLA_HEADS * MLA_NOPE + (pair + 1) * LANES]
        qr_ss = head_sums(qr * qr)
        q_rs, k_rs = [], []
        for e in range(2):
            h = 2 * pair + e
            qn = qfull[:, h * MLA_NOPE:(h + 1) * MLA_NOPE]
            kn = kvfull[:, h * MLA_NOPE:(h + 1) * MLA_NOPE]
            qs = lax.rsqrt((jnp.sum(qn * qn, axis=-1, keepdims=True) + qr_ss[e]) / MLA_QK + EPS)
            ks = lax.rsqrt((jnp.sum(kn * kn, axis=-1, keepdims=True) + kr_ss) / MLA_QK + EPS)
            q_rs.append(qs)
            k_rs.append(ks)
            q_ref[:, 2 * h * LANES:(2 * h + 1) * LANES] = (qn * qs * gqn_ref[...]).astype(BF16)
            k_ref[:, 2 * h * LANES:(2 * h + 1) * LANES] = (kn * ks * gkn_ref[...]).astype(BF16)
        q_rot = rotary(qr * jnp.where(first, q_rs[0], q_rs[1]) * gqr_ref[...])
        k_rot = (kr_rot * jnp.where(first, k_rs[0], k_rs[1])).astype(BF16)
        for e in range(2):
            h = 2 * pair + e
            own = first if e == 0 else jnp.logical_not(first)
            q_ref[:, (2 * h + 1) * LANES:(2 * h + 2) * LANES] = jnp.where(own, q_rot, 0.0).astype(BF16)
            k_ref[:, (2 * h + 1) * LANES:(2 * h + 2) * LANES] = k_rot


def _mla_prep(p, cos, sin, gq, gkv, wuq, wukv, gqn, gqr, gkn, gkr, seq, tm=512):
    m = p.shape[0]
    sblk = seq // tm
    row = lambda width, cb: pl.BlockSpec((tm, width), lambda i: (i, cb))
    const = lambda shape: pl.BlockSpec(shape, lambda i: (0, 0))
    pos = pl.BlockSpec((tm, LANES), lambda i: (i % sblk, 0))
    qk_width = MLA_HEADS * 2 * LANES
    return pl.pallas_call(
        _mla_prep_kernel,
        out_shape=(jax.ShapeDtypeStruct((m, qk_width), BF16),
                   jax.ShapeDtypeStruct((m, qk_width), BF16),
                   jax.ShapeDtypeStruct((m, MLA_WIDTH), BF16)),
        grid=(m // tm,),
        in_specs=[
            row(Q_LORA, SEC_MLA // Q_LORA),
            row(KV_LORA, SEC_MLA // KV_LORA + 1),
            row(256, SEC_SMALL // 256),
            pos, pos,
            const((1, Q_LORA)), const((1, KV_LORA)),
            const(wuq.shape), const(wukv.shape),
            const((1, LANES)), const((1, LANES)), const((1, LANES)), const((1, LANES)),
        ],
        out_specs=(row(qk_width, 0), row(qk_width, 0), row(MLA_WIDTH, 0)),
        compiler_params=_params(("parallel",)),
        name="mla_prep",
    )(p, p, p, cos, sin, gq, gkv, wuq, wukv, gqn, gqr, gkn, gkr)


def _mla_attn_kernel(q_ref, k_ref, v_ref, g_ref, o_ref, vt_ref, *, t, heads):
    i = pl.program_id(2)
    nblk = v_ref.shape[0] // t
    exp2_scale = (MLA_QK ** -0.5) * math.log2(math.e)
    qw = 2 * LANES
    key_chunk = lax.broadcasted_iota(jnp.int32, (t, t), 0) // CHUNK
    qry_chunk = lax.broadcasted_iota(jnp.int32, (t, t), 1) // CHUNK
    visible = key_chunk <= qry_chunk
    qs = [q_ref[:, h * qw:(h + 1) * qw] for h in range(heads)]

    @pl.when(i == 0)
    def _():
        for h in range(heads):
            for blk in range(nblk):
                vb = v_ref[blk * t:(blk + 1) * t, h * LANES:(h + 1) * LANES]
                vt_ref[h, blk] = vb.astype(F32).T.astype(BF16)

    def step(j, carry, diagonal):
        start = pl.multiple_of(j * t, t)
        scores = [_dot_nt(k_ref[pl.ds(start, t), h * qw:(h + 1) * qw], qs[h]) for h in range(heads)]
        soft = []
        for h in range(heads):
            m, l, _ = carry[h]
            s = scores[h]
            if diagonal:
                s = jnp.where(visible, s, NEG_BIG)
            m_new = jnp.maximum(m, jnp.max(s, axis=0, keepdims=True))
            alpha = jnp.exp2((m - m_new) * exp2_scale)
            pr = jnp.exp2((s - m_new) * exp2_scale)
            l = alpha * l + jnp.sum(pr, axis=0, keepdims=True)
            soft.append((m_new, l, alpha, pr.astype(BF16)))
        out = []
        for h in range(heads):
            m_new, l, alpha, pr = soft[h]
            out.append((m_new, l, alpha * carry[h][2] + _dot(vt_ref[h, j], pr)))
        return tuple(out)

    init = tuple((jnp.full((1, t), NEG_BIG, F32), jnp.zeros((1, t), F32), jnp.zeros((MLA_V, t), F32))
                 for _ in range(heads))
    carry = lax.fori_loop(0, i, lambda j, c: step(j, c, False), init)
    carry = step(i, carry, True)
    for h in range(heads):
        hs = slice(h * LANES, (h + 1) * LANES)
        _, l, acc = carry[h]
        o_ref[:, hs] = ((acc / l).T * _silu(g_ref[:, hs])).astype(o_ref.dtype)


def _mla_attn(q, k, v, p, batch, seq, t=ATTN_TILE, heads=MLA_GROUP):
    nq = seq // t
    qw = heads * 2 * LANES
    vw = heads * LANES
    gate0 = (SEC_MLA + Q_LORA + KV_LORA) // vw
    return pl.pallas_call(
        functools.partial(_mla_attn_kernel, t=t, heads=heads),
        out_shape=jax.ShapeDtypeStruct((batch * seq, MLA_WIDTH), BF16),
        grid=(batch, MLA_HEADS // heads, nq),
        in_specs=[
            pl.BlockSpec((t, qw), lambda b, h, i: (b * nq + i, h)),
            pl.BlockSpec((seq, qw), lambda b, h, i: (b, h)),
            pl.BlockSpec((seq, vw), lambda b, h, i: (b, h)),
            pl.BlockSpec((t, vw), lambda b, h, i: (b * nq + i, gate0 + h)),
        ],
        out_specs=pl.BlockSpec((t, vw), lambda b, h, i: (b * nq + i, h)),
        scratch_shapes=[pltpu.VMEM((heads, nq, MLA_V, t), BF16)],
        compiler_params=_params(("parallel", "parallel", "arbitrary")),
        name="mla_attn",
    )(q, k, v, p)


def _rw_prep_kernel(cur_ref, prev_ref, smc_ref, smp_ref, mu_ref, mus_ref, wlora_ref, w0_ref, a0_ref,
                    kk_ref, ka_ref, ones_ref, r_ref, lw_ref, k_ref, v_ref, na_ref, b_ref,
                    *, tm, seq):
    i = pl.program_id(0)
    at_start = (i * tm) % seq == 0
    row = lax.broadcasted_iota(jnp.int32, (tm, 1), 0)

    def shifted(c_ref, p_ref, mu):
        cur = c_ref[...]
        last = jnp.where(at_start, 0.0, p_ref[...][7:8, :])
        prev = jnp.where(row == 0, last, pltpu.roll(cur, 1, axis=0))
        return cur + mu * (prev - cur)

    main = shifted(cur_ref, prev_ref, mu_ref[...])
    small = shifted(smc_ref, smp_ref, mus_ref[...])[:, LANES:]
    r = main[:, :RW_WIDTH]
    k = main[:, RW_WIDTH:2 * RW_WIDTH]
    v = main[:, 2 * RW_WIDTH:]

    lane = lax.broadcasted_iota(jnp.int32, (1, LANES), 1)
    lora_in = jnp.where(lane < DECAY_LORA, jnp.tanh(small), small).astype(BF16)
    lora = _dot(lora_in, wlora_ref[...])
    w_log = -_softplus(-(w0_ref[...] + lora[:, :RW_WIDTH])) - 0.5
    a = _sigmoid(a0_ref[...] + lora[:, RW_WIDTH:])

    kk = k * kk_ref[...]
    hi, lo = _split_bf16(kk * kk)
    ss = _dot(hi, ones_ref[...]) + _dot(lo, ones_ref[...])
    kk = kk / jnp.maximum(jnp.sqrt(ss), 1e-12)

    r_ref[...] = r
    lw_ref[...] = -jnp.exp(w_log)
    k_ref[...] = k * (1.0 + (a - 1.0) * ka_ref[...])
    v_ref[...] = v
    na_ref[...] = -kk
    b_ref[...] = kk * a


def _rw_prep(p, mu, mus, wlora, w0, a0, k_k, k_a, ones_bd, seq, tm=512):
    m = p.shape[0]
    vec = lambda width: pl.BlockSpec((1, width), lambda i: (0, 0))
    out = jax.ShapeDtypeStruct((m, RW_WIDTH), F32)
    main_w = 3 * RW_WIDTH
    prev_blk = lambda i: (jnp.maximum(i * (tm // 8) - 1, 0))
    return pl.pallas_call(
        functools.partial(_rw_prep_kernel, tm=tm, seq=seq),
        out_shape=(out,) * 6,
        grid=(m // tm,),
        in_specs=[
            pl.BlockSpec((tm, main_w), lambda i: (i, SEC_RW // main_w)),
            pl.BlockSpec((8, main_w), lambda i: (prev_blk(i), SEC_RW // main_w)),
            pl.BlockSpec((tm, 256), lambda i: (i, SEC_SMALL // 256)),
            pl.BlockSpec((8, 256), lambda i: (prev_blk(i), SEC_SMALL // 256)),
            vec(main_w), vec(256),
            pl.BlockSpec(wlora.shape, lambda i: (0, 0)),
            vec(RW_WIDTH), vec(RW_WIDTH), vec(RW_WIDTH), vec(RW_WIDTH),
            pl.BlockSpec((RW_WIDTH, RW_WIDTH), lambda i: (0, 0)),
        ],
        out_specs=(pl.BlockSpec((tm, RW_WIDTH), lambda i: (i, 0)),) * 6,
        compiler_params=_params(("parallel",)),
        name="rw_prep",
    )(p, p, p, p, mu, mus, wlora, w0, a0, k_k, k_a, ones_bd)


_HI = lax.Precision.HIGHEST

RW_MODE_SCORE = "b1"
RW_MODE_INV = "b1"
RW_MODE_MID = "b1"
RW_MODE_STATE = "b1"


def _mm(a, b, mode, form="nn"):
    f = {"nn": _dot, "nt": _dot_nt, "tn": _dot_tn}[form]
    if mode == "hi":
        return f(a, b, _HI)
    if mode == "b1":
        return f(a.astype(BF16), b.astype(BF16))
    a_hi, a_lo = _split_bf16(a)
    b_hi, b_lo = _split_bf16(b)
    return f(a_hi, b_hi) + (f(a_hi, b_lo) + f(a_lo, b_hi))


def _rw_chunks(probs, states):
    c = RW_CHUNK
    n = 2 * c
    lane = lax.broadcasted_iota(jnp.int32, (1, LANES), 1)
    first = lane < RW_HEAD_DIM
    ti = lax.broadcasted_iota(jnp.int32, (c, c), 0)
    si = lax.broadcasted_iota(jnp.int32, (c, c), 1)
    tri = (si <= ti).astype(BF16)
    ri = lax.broadcasted_iota(jnp.int32, (n, n), 0)
    ci = lax.broadcasted_iota(jnp.int32, (n, n), 1)
    strict = (ci % c) < (ri % c)
    incl = (ci % c) <= (ri % c)
    eye = ri == ci

    def stack(x):
        return jnp.concatenate([jnp.where(first, x, 0.0), jnp.where(first, 0.0, x)], axis=0)

    splits = []
    for _, lw, _, _, _, _ in probs:
        lw_hi, lw_lo = _split_bf16(lw)
        splits.append((lw_hi, lw_lo, (lw - lw_hi.astype(F32) - lw_lo.astype(F32)).astype(BF16)))
    cums = [_dot(tri, h) + (_dot(tri, l) + _dot(tri, l2)) for h, l, l2 in splits]

    feats = []
    for (r, lw, k, v, na, b), cum in zip(probs, cums):
        total = cum[c - 1:c, :]
        p_inv = jnp.exp(-cum)
        p_end = jnp.exp(total - cum)
        at = stack(na * jnp.exp(cum - lw))
        rt = stack(r * jnp.exp(cum))
        lhs = jnp.concatenate([at, rt], axis=0)
        rhs = jnp.concatenate([stack(b * p_inv), stack(k * p_inv)], axis=0)
        feats.append(dict(at=at, rt=rt, lhs=lhs, rhs=rhs, bh=stack(b * p_end), kh=stack(k * p_end),
                          vs=stack(v), decay=jnp.exp(total)))

    gs = [_mm(f["lhs"], f["rhs"], RW_MODE_SCORE, "nt") for f in feats]
    l_ab = [jnp.where(strict, g[:n, :n], 0.0) for g in gs]
    l_ak = [jnp.where(strict, g[:n, n:], 0.0) for g in gs]
    m_rb = [jnp.where(incl, g[n:, :n], 0.0) for g in gs]
    m_rk = [jnp.where(incl, g[n:, n:], 0.0) for g in gs]

    xs = l_ab
    tinv = [jnp.where(eye, 1.0, 0.0) + x for x in xs]
    for _ in range(int(math.log2(c)) - 1):
        xs = [_mm(x, x, RW_MODE_INV) for x in xs]
        tinv = [t + _mm(t, x, RW_MODE_INV) for t, x in zip(tinv, xs)]

    lv = [_mm(l, f["vs"], RW_MODE_MID) for l, f in zip(l_ak, feats)]
    wu = [_mm(t, jnp.concatenate([f["at"], y], axis=1), RW_MODE_MID)
          for t, f, y in zip(tinv, feats, lv)]
    ro = [_mm(m, w, RW_MODE_MID) for m, w in zip(m_rb, wu)]
    rv = [_mm(m, f["vs"], RW_MODE_MID) for m, f in zip(m_rk, feats)]
    ab = [_mm(w, f["bh"], RW_MODE_STATE, "tn") for w, f in zip(wu, feats)]
    vk = [_mm(f["vs"], f["kh"], RW_MODE_STATE, "tn") for f in feats]

    outs, new_states = [], []
    for i, f in enumerate(feats):
        rh = f["rt"] + ro[i][:, :LANES]
        o0 = ro[i][:, LANES:] + rv[i]
        a_mat = jnp.where(eye, f["decay"], 0.0) + ab[i][:LANES]
        b_mat = ab[i][LANES:] + vk[i]
        outs.append(_mm(rh[:c] + rh[c:], states[i], RW_MODE_STATE, "nt") + o0[:c] + o0[c:])
        new_states.append(_mm(states[i], a_mat, RW_MODE_STATE) + b_mat)
    return outs, new_states


def _rw_scan_kernel(r_ref, lw_ref, k_ref, v_ref, na_ref, b_ref, g_ref, rk_ref, gng_ref, gnb_ref,
                    o_ref, s_ref):
    @pl.when(pl.program_id(0) == 0)
    def _():
        s_ref[...] = jnp.zeros_like(s_ref)

    lane = lax.broadcasted_iota(jnp.int32, (1, LANES), 1)
    first = lane < RW_HEAD_DIM
    batch = r_ref.shape[0]
    pairs = RW_WIDTH // LANES
    where = [(bi, slice(pr * LANES, (pr + 1) * LANES)) for bi in range(batch) for pr in range(pairs)]

    def head_sum(y):
        s_first = jnp.sum(jnp.where(first, y, 0.0), axis=-1, keepdims=True)
        s_second = jnp.sum(jnp.where(first, 0.0, y), axis=-1, keepdims=True)
        return jnp.where(first, s_first, s_second)

    probs = [(r_ref[bi, :, sl], lw_ref[bi, :, sl], k_ref[bi, :, sl], v_ref[bi, :, sl], na_ref[bi, :, sl],
              b_ref[bi, :, sl]) for bi, sl in where]
    outs, new_states = _rw_chunks(probs, [s_ref[i] for i in range(len(where))])
    for i, (bi, sl) in enumerate(where):
        s_ref[i] = new_states[i]
        r, _, k, v, _, _ = probs[i]
        o = outs[i]
        mu = head_sum(o) / RW_HEAD_DIM
        d = o - mu
        var = head_sum(d * d) / RW_HEAD_DIM
        normed = d * lax.rsqrt(var + RW_GN_EPS) * gng_ref[:, sl] + gnb_ref[:, sl]
        bonus = head_sum(r * k * rk_ref[:, sl]) * v
        o_ref[bi, :, sl] = ((normed + bonus) * _silu(g_ref[bi, :, sl])).astype(o_ref.dtype)


def _rw_scan(r, lw, k, v, na, b, p, r_k, gn_g, gn_b, batch, seq):
    c = RW_CHUNK
    as3d = lambda t: t.reshape(batch, seq, t.shape[-1])
    tok = pl.BlockSpec((batch, c, RW_WIDTH), lambda ci: (0, ci, 0))
    vec = pl.BlockSpec((1, RW_WIDTH), lambda ci: (0, 0))
    gate = pl.BlockSpec((batch, c, RW_WIDTH), lambda ci: (0, ci, SEC_RG // RW_WIDTH))
    out = pl.pallas_call(
        _rw_scan_kernel,
        out_shape=jax.ShapeDtypeStruct((batch, seq, RW_WIDTH), BF16),
        grid=(seq // c,),
        in_specs=[tok] * 6 + [gate, vec, vec, vec],
        out_specs=tok,
        scratch_shapes=[pltpu.VMEM((batch * RW_WIDTH // LANES, LANES, LANES), F32)],
        compiler_params=_params(("arbitrary",)),
        name="rw_scan",
    )(*[as3d(t) for t in (r, lw, k, v, na, b, p)], r_k, gn_g, gn_b)
    return out.reshape(batch * seq, RW_WIDTH)


def _rw_branch(p, mu, w0, w_up, a0, a_up, k_k, k_a, r_k, gn_g, gn_b, batch, seq):
    row = lambda t: t[None, :]
    head_ones = jnp.kron(jnp.eye(RW_HEADS, dtype=F32), jnp.ones((RW_HEAD_DIM, RW_HEAD_DIM), F32)).astype(BF16)
    mu_small = jnp.concatenate([jnp.zeros((LANES,), F32), mu[3 * RW_WIDTH:]])
    zeros = jnp.zeros((DECAY_LORA, RW_WIDTH), F32)
    wlora = jnp.concatenate([jnp.concatenate([w_up, zeros], axis=1),
                             jnp.concatenate([zeros, a_up], axis=1)], axis=0).astype(BF16)
    rr, lw, rk, rv, na, rb = _rw_prep(p, row(mu[:3 * RW_WIDTH]), row(mu_small), wlora, row(w0), row(a0),
                                      row(k_k), row(k_a), head_ones, seq)
    return _rw_scan(rr, lw, rk, rv, na, rb, p, r_k.reshape(1, RW_WIDTH), row(gn_g), row(gn_b), batch, seq)


def _merge_kernel(ysb_ref, ymla_ref, yrw_ref, wsb_ref, wmla_ref, wrw_ref, g1_ref, g2_ref, g3_ref, o_ref):
    o = _sigmoid(g1_ref[...]) * _dot(ysb_ref[...], wsb_ref[...])
    o = o + _sigmoid(g2_ref[...]) * _dot(ymla_ref[...], wmla_ref[...])
    o = o + _sigmoid(g3_ref[...]) * _dot(yrw_ref[...], wrw_ref[...])
    o_ref[...] = o.astype(o_ref.dtype)


def _merge(y_sb, y_mla, y_rw, w_sb, w_mla, w_rw, p, tm=512, tn=512):
    m = p.shape[0]
    rows = lambda width: pl.BlockSpec((tm, width), lambda i, j: (i, 0))
    wcol = lambda depth: pl.BlockSpec((depth, tn), lambda i, j: (0, j))
    gate = lambda br: pl.BlockSpec((tm, tn), lambda i, j: (i, (SEC_GATE + br * D_MODEL) // tn + j))
    return pl.pallas_call(
        _merge_kernel,
        out_shape=jax.ShapeDtypeStruct((m, D_MODEL), BF16),
        grid=(m // tm, D_MODEL // tn),
        in_specs=[rows(SB_WIDTH), rows(MLA_WIDTH), rows(RW_WIDTH),
                  wcol(SB_WIDTH), wcol(MLA_WIDTH), wcol(RW_WIDTH),
                  gate(0), gate(1), gate(2)],
        out_specs=pl.BlockSpec((tm, tn), lambda i, j: (i, j)),
        compiler_params=_params(("parallel", "arbitrary")),
        name="merge",
    )(y_sb, y_mla, y_rw, w_sb, w_mla, w_rw, p, p, p)


def _out_proj_kernel(x_ref, m_ref, w_ref, o_ref):
    o_ref[...] = x_ref[...] + _dot(m_ref[...], w_ref[...])


def _out_proj(x2, merged, w, tm=512, tn=1024):
    m = x2.shape[0]
    return pl.pallas_call(
        _out_proj_kernel,
        out_shape=jax.ShapeDtypeStruct((m, D_MODEL), F32),
        grid=(m // tm, D_MODEL // tn),
        in_specs=[
            pl.BlockSpec((tm, tn), lambda i, j: (i, j)),
            pl.BlockSpec((tm, D_MODEL), lambda i, j: (i, 0)),
            pl.BlockSpec((D_MODEL, tn), lambda i, j: (0, j)),
        ],
        out_specs=pl.BlockSpec((tm, tn), lambda i, j: (i, j)),
        compiler_params=_params(("parallel", "arbitrary")),
        name="out_proj",
    )(x2, merged, w)


def _reorder_w_in(w):
    kr = 3072
    mg = kr + MLA_ROPE
    rw = mg + MLA_WIDTH
    lo = rw + 3 * RW_WIDTH
    rg = lo + DECAY_LORA + ICL_LORA
    gt = rg + RW_WIDTH
    x1 = w[:, kr:kr + HALF_ROPE]
    x2 = w[:, kr + HALF_ROPE:mg]
    return jnp.concatenate([
        w[:, :kr], w[:, mg:rw], w[:, rg:gt], w[:, rw:lo], w[:, gt:],
        x1, x1, x2, x2, w[:, lo:rg]], axis=1)


def _pair_rope_cols(t):
    lead = t.shape[:-2]
    t = t.reshape(lead + (MLA_HEADS // 2, 2, 2, HALF_ROPE))
    t = jnp.swapaxes(t, -3, -2)
    return t.reshape(lead + (MLA_HEADS // 2 * LANES,))


def _pair_rope_gain(g):
    g1, g2 = g[:HALF_ROPE], g[HALF_ROPE:]
    return jnp.concatenate([g1, g1, g2, g2])[None, :]


def _rope_tables(seq):
    freqs = ROPE_THETA ** (-jnp.arange(HALF_ROPE, dtype=F32) / HALF_ROPE)
    ang = jnp.arange(seq, dtype=F32)[:, None] * freqs[None, :]
    c, s = jnp.cos(ang), jnp.sin(ang)
    return jnp.concatenate([c, c, c, c], axis=1), jnp.concatenate([-s, -s, s, s], axis=1)


def _mla_branch(p, cos, sin, q_norm_g, kv_norm_g, w_uq, w_ukv, qn_g, kn_g, batch, seq):
    row = lambda t: t[None, :]
    uq = w_uq.reshape(Q_LORA, MLA_HEADS, MLA_QK)
    wuq = jnp.concatenate([uq[:, :, :MLA_NOPE].reshape(Q_LORA, -1), _pair_rope_cols(uq[:, :, MLA_NOPE:])],
                          axis=1).astype(BF16)
    ukv = w_ukv.reshape(KV_LORA, MLA_HEADS, MLA_NOPE + MLA_V)
    wukv = jnp.concatenate([ukv[:, :, :MLA_NOPE].reshape(KV_LORA, -1),
                            ukv[:, :, MLA_NOPE:].reshape(KV_LORA, -1)], axis=1).astype(BF16)
    q, k, v = _mla_prep(p, cos, sin, row(q_norm_g), row(kv_norm_g), wuq, wukv,
                        row(qn_g[:MLA_NOPE]), _pair_rope_gain(qn_g[MLA_NOPE:]),
                        row(kn_g[:MLA_NOPE]), _pair_rope_gain(kn_g[MLA_NOPE:]), seq)
    return _mla_attn(q, k, v, p, batch, seq)


def kernel(x, norm_g, w_in, mla_q_norm_g, mla_kv_norm_g, mla_w_uq, mla_w_ukv, mla_qn_g, mla_kn_g,
           rw_mu, rw_w0, rw_w_up, rw_a0, rw_a_up, rw_k_k, rw_k_a, rw_r_k, rw_gn_g, rw_gn_b,
           w_br_sb, w_br_mla, w_br_rw, w_out):
    batch, seq, _ = x.shape
    depth = w_in.shape[0]
    x2 = x.reshape(batch * seq, D_MODEL)
    cos, sin = _rope_tables(seq)

    for l in range(depth):
        p = _in_proj(x2, norm_g[l][None, :], _reorder_w_in(w_in[l]).astype(BF16))
        y_sb = _sb_attn(p, batch, seq)
        y_mla = _mla_branch(p, cos, sin, mla_q_norm_g[l], mla_kv_norm_g[l], mla_w_uq[l], mla_w_ukv[l],
                            mla_qn_g[l], mla_kn_g[l], batch, seq)
        y_rw = _rw_branch(p, rw_mu[l], rw_w0[l], rw_w_up[l], rw_a0[l], rw_a_up[l], rw_k_k[l], rw_k_a[l],
                          rw_r_k[l], rw_gn_g[l], rw_gn_b[l], batch, seq)
        merged = _merge(y_sb, y_mla, y_rw, w_br_sb[l].astype(BF16), w_br_mla[l].astype(BF16),
                        w_br_rw[l].astype(BF16), p)
        x2 = _out_proj(x2, merged, w_out[l].astype(BF16))

    return x2.reshape(batch, seq, D_MODEL)
```

```python
import functools
import math

import jax
import jax.numpy as jnp
from jax import lax
from jax.experimental import pallas as pl
from jax.experimental.pallas import tpu as pltpu

F32 = jnp.float32
BF16 = jnp.bfloat16

D_MODEL = 2048
EPS = 1e-6
CHUNK = 64

SB_HEADS = 4
SB_HEAD_DIM = 128
SB_WIDTH = SB_HEADS * SB_HEAD_DIM

MLA_HEADS = 8
MLA_NOPE = 128
MLA_ROPE = 64
MLA_QK = MLA_NOPE + MLA_ROPE
MLA_V = 128
MLA_WIDTH = MLA_HEADS * MLA_V
Q_LORA = 512
KV_LORA = 512
ROPE_THETA = 10000.0
HALF_ROPE = MLA_ROPE // 2

RW_HEADS = 8
RW_HEAD_DIM = 64
RW_WIDTH = RW_HEADS * RW_HEAD_DIM
DECAY_LORA = 64
ICL_LORA = 64
RW_GN_EPS = 64e-5
RW_CHUNK = 64

LANES = 128
VMEM_LIMIT = 56 * 1024 * 1024

SEC_SB = 0
SEC_MLA = 2048
SEC_RG = 4096
SEC_RW = 4608
SEC_GATE = 6144
SEC_SMALL = 12288
N_PROJ = SEC_SMALL + 256

NEG_BIG = -1e30

SB_DEAD_LOG_WEIGHT = -104.0

ATTN_TILE = 256
SB_GROUP = 4
MLA_GROUP = 4


def _dot(a, b, precision=None):
    return jnp.dot(a, b, preferred_element_type=F32, precision=precision)


def _dot_nt(a, b, precision=None):
    return lax.dot_general(a, b, (((1,), (1,)), ((), ())), preferred_element_type=F32,
                           precision=precision)


def _dot_tn(a, b, precision=None):
    return lax.dot_general(a, b, (((0,), (0,)), ((), ())), preferred_element_type=F32,
                           precision=precision)


def _split_bf16(x):
    hi = x.astype(BF16)
    lo = (x - hi.astype(F32)).astype(BF16)
    return hi, lo


def _softplus(z):
    return jnp.maximum(z, 0.0) + jnp.log(1.0 + jnp.exp(-jnp.abs(z)))


def _sigmoid(z):
    return 1.0 / (1.0 + jnp.exp(-z))


def _silu(z):
    return z * _sigmoid(z)


def _params(semantics):
    return pltpu.CompilerParams(dimension_semantics=semantics, vmem_limit_bytes=VMEM_LIMIT)


def _in_proj_kernel(x_ref, g_ref, w_ref, o_ref, h_ref):
    @pl.when(pl.program_id(1) == 0)
    def _():
        x = x_ref[...]
        rs = lax.rsqrt(jnp.mean(x * x, axis=-1, keepdims=True) + EPS)
        h_ref[...] = (x * rs * g_ref[...]).astype(BF16)

    o_ref[...] = _dot(h_ref[...], w_ref[...])


def _in_proj(x2, g, w, tm=512, tn=1792):
    m = x2.shape[0]
    return pl.pallas_call(
        _in_proj_kernel,
        out_shape=jax.ShapeDtypeStruct((m, N_PROJ), F32),
        grid=(m // tm, N_PROJ // tn),
        in_specs=[
            pl.BlockSpec((tm, D_MODEL), lambda i, j: (i, 0)),
            pl.BlockSpec((1, D_MODEL), lambda i, j: (0, 0)),
            pl.BlockSpec((D_MODEL, tn), lambda i, j: (0, j)),
        ],
        out_specs=pl.BlockSpec((tm, tn), lambda i, j: (i, j)),
        scratch_shapes=[pltpu.VMEM((tm, D_MODEL), BF16)],
        compiler_params=_params(("parallel", "arbitrary")),
        name="in_proj",
    )(x2, g, w)


def _sb_attn_kernel(q_ref, k_ref, v_ref, g_ref, o_ref, kb_ref, vt_ref, *, t, heads):
    i = pl.program_id(2)
    nblk = k_ref.shape[0] // t
    scale = SB_HEAD_DIM ** -0.5
    key = lax.broadcasted_iota(jnp.int32, (t, t), 0)
    qry = lax.broadcasted_iota(jnp.int32, (t, t), 1)
    later = (qry > key).astype(BF16)
    causal = key < qry
    head = lambda h: slice(h * LANES, (h + 1) * LANES)
    qs = [q_ref[:, head(h)].astype(BF16) for h in range(heads)]

    @pl.when(i == 0)
    def _():
        for h in range(heads):
            for blk in range(nblk):
                rows = slice(blk * t, (blk + 1) * t)
                kb_ref[h, blk] = k_ref[rows, head(h)].astype(BF16)
                vt_ref[h, blk] = v_ref[rows, head(h)].T.astype(BF16)

    def step(j, carry, diagonal):
        zs = [_dot_nt(kb_ref[h, j], qs[h]) for h in range(heads)]
        stage = []
        for h in range(heads):
            z = zs[h] * scale
            sp = _softplus(z)
            log_fail = jnp.where(causal, -sp, 0.0) if diagonal else -sp
            hi, lo = _split_bf16(log_fail)
            stage.append((z - sp, log_fail, hi, lo))
        sums = [_dot(later, hi) + _dot(later, lo) for _, _, hi, lo in stage]
        ws = []
        for h in range(heads):
            w = jnp.exp(stage[h][0] + sums[h] + carry[h][0])
            if diagonal:
                w = jnp.where(causal, w, 0.0)
            ws.append(w.astype(BF16))
        out = []
        for h in range(heads):
            run, acc = carry[h]
            out.append((run + jnp.sum(stage[h][1], axis=0, keepdims=True), acc + _dot(vt_ref[h, j], ws[h])))
        return tuple(out)

    init = tuple((jnp.zeros((1, t), F32), jnp.zeros((SB_HEAD_DIM, t), F32)) for _ in range(heads))
    carry = step(i, init, True)

    def alive(c):
        return functools.reduce(jnp.maximum, [jnp.max(c[h][0]) for h in range(heads)]) >= SB_DEAD_LOG_WEIGHT

    def body(state):
        n, _, c = state
        c = step(i - 1 - n, c, False)
        return n + 1, alive(c), c

    _, _, carry = lax.while_loop(lambda s: jnp.logical_and(s[0] < i, s[1]), body, (0, alive(carry), carry))
    for h in range(heads):
        o_ref[:, head(h)] = (carry[h][1].T * _silu(g_ref[:, head(h)])).astype(o_ref.dtype)


def _sb_attn(p, batch, seq, t=ATTN_TILE, heads=SB_GROUP):
    nq = seq // t
    w = heads * LANES
    sec = SB_WIDTH // w
    tile = lambda s: pl.BlockSpec((t, w), lambda b, h, i: (b * nq + i, s * sec + h))
    full = lambda s: pl.BlockSpec((seq, w), lambda b, h, i: (b, s * sec + h))
    return pl.pallas_call(
        functools.partial(_sb_attn_kernel, t=t, heads=heads),
        out_shape=jax.ShapeDtypeStruct((batch * seq, SB_WIDTH), BF16),
        grid=(batch, SB_HEADS // heads, nq),
        in_specs=[tile(0), full(1), full(2), tile(3)],
        out_specs=pl.BlockSpec((t, w), lambda b, h, i: (b * nq + i, h)),
        scratch_shapes=[pltpu.VMEM((heads, nq, t, SB_HEAD_DIM), BF16),
                        pltpu.VMEM((heads, nq, SB_HEAD_DIM, t), BF16)],
        compiler_params=_params(("parallel", "parallel", "arbitrary")),
        name="sb_attn",
    )(p, p, p, p)


def _mla_prep_kernel(cq_ref, ckv_ref, sm_ref, cos_ref, sin_ref, gq_ref, gkv_ref, wuq_ref, wukv_ref,
                     gqn_ref, gqr_ref, gkn_ref, gkr_ref, q_ref, k_ref, v_ref):
    lane = lax.broadcasted_iota(jnp.int32, (1, LANES), 1)
    first = (lane % MLA_ROPE) < HALF_ROPE
    cos = cos_ref[...]
    sin = sin_ref[...]

    def latent_norm(c_ref, g_ref):
        c = c_ref[...]
        rs = lax.rsqrt(jnp.mean(c * c, axis=-1, keepdims=True) + EPS)
        return (c * rs * g_ref[...]).astype(BF16)

    def rotary(y):
        return y * cos + pltpu.roll(y, MLA_ROPE, axis=1) * sin

    def head_sums(sq):
        s_first = jnp.sum(jnp.where(first, sq, 0.0), axis=-1, keepdims=True)
        s_second = jnp.sum(jnp.where(first, 0.0, sq), axis=-1, keepdims=True)
        return s_first, s_second

    qfull = _dot(latent_norm(cq_ref, gq_ref), wuq_ref[...])
    kvfull = _dot(latent_norm(ckv_ref, gkv_ref), wukv_ref[...])
    v_ref[...] = kvfull[:, MLA_HEADS * MLA_NOPE:].astype(BF16)

    kr = sm_ref[...][:, :LANES]
    kr_ss, _ = head_sums(kr * kr)
    kr_rot = rotary(kr * gkr_ref[...])

    for pair in range(MLA_HEADS // 2):
        qr = qfull[:, MLA_HEADS * MLA_NOPE + pair * LANES:MLA_HEADS * MLA_NOPE + (pair + 1) * LANES]
        qr_ss = head_sums(qr * qr)
        q_rs, k_rs = [], []
        for e in range(2):
            h = 2 * pair + e
            qn = qfull[:, h * MLA_NOPE:(h + 1) * MLA_NOPE]
            kn = kvfull[:, h * MLA_NOPE:(h + 1) * MLA_NOPE]
            qs = lax.rsqrt((jnp.sum(qn * qn, axis=-1, keepdims=True) + qr_ss[e]) / MLA_QK + EPS)
            ks = lax.rsqrt((jnp.sum(kn * kn, axis=-1, keepdims=True) + kr_ss) / MLA_QK + EPS)
            q_rs.append(qs)
            k_rs.append(ks)
            q_ref[:, 2 * h * LANES:(2 * h + 1) * LANES] = (qn * qs * gqn_ref[...]).astype(BF16)
            k_ref[:, 2 * h * LANES:(2 * h + 1) * LANES] = (kn * ks * gkn_ref[...]).astype(BF16)
        q_rot = rotary(qr * jnp.where(first, q_rs[0], q_rs[1]) * gqr_ref[...])
        k_rot = (kr_rot * jnp.where(first, k_rs[0], k_rs[1])).astype(BF16)
        for e in range(2):
            h = 2 * pair + e
            own = first if e == 0 else jnp.logical_not(first)
            q_ref[:, (2 * h + 1) * LANES:(2 * h + 2) * LANES] = jnp.where(own, q_rot, 0.0).astype(BF16)
            k_ref[:, (2 * h + 1) * LANES:(2 * h + 2) * LANES] = k_rot


def _mla_prep(p, cos, sin, gq, gkv, wuq, wukv, gqn, gqr, gkn, gkr, seq, tm=512):
    m = p.shape[0]
    sblk = seq // tm
    row = lambda width, cb: pl.BlockSpec((tm, width), lambda i: (i, cb))
    const = lambda shape: pl.BlockSpec(shape, lambda i: (0, 0))
    pos = pl.BlockSpec((tm, LANES), lambda i: (i % sblk, 0))
    qk_width = MLA_HEADS * 2 * LANES
    return pl.pallas_call(
        _mla_prep_kernel,
        out_shape=(jax.ShapeDtypeStruct((m, qk_width), BF16),
                   jax.ShapeDtypeStruct((m, qk_width), BF16),
                   jax.ShapeDtypeStruct((m, MLA_WIDTH), BF16)),
        grid=(m // tm,),
        in_specs=[
            row(Q_LORA, SEC_MLA // Q_LORA),
            row(KV_LORA, SEC_MLA // KV_LORA + 1),
            row(256, SEC_SMALL // 256),
            pos, pos,
            const((1, Q_LORA)), const((1, KV_LORA)),
            const(wuq.shape), const(wukv.shape),
            const((1, LANES)), const((1, LANES)), const((1, LANES)), const((1, LANES)),
        ],
        out_specs=(row(qk_width, 0), row(qk_width, 0), row(MLA_WIDTH, 0)),
        compiler_params=_params(("parallel",)),
        name="mla_prep",
    )(p, p, p, cos, sin, gq, gkv, wuq, wukv, gqn, gqr, gkn, gkr)


def _mla_attn_kernel(q_ref, k_ref, v_ref, g_ref, o_ref, vt_ref, *, t, heads):
    i = pl.program_id(2)
    nblk = v_ref.shape[0] // t
    exp2_scale = (MLA_QK ** -0.5) * math.log2(math.e)
    qw = 2 * LANES
    key_chunk = lax.broadcasted_iota(jnp.int32, (t, t), 0) // CHUNK
    qry_chunk = lax.broadcasted_iota(jnp.int32, (t, t), 1) // CHUNK
    visible = key_chunk <= qry_chunk
    qs = [q_ref[:, h * qw:(h + 1) * qw] for h in range(heads)]

    @pl.when(i == 0)
    def _():
        for h in range(heads):
            for blk in range(nblk):
                vb = v_ref[blk * t:(blk + 1) * t, h * LANES:(h + 1) * LANES]
                vt_ref[h, blk] = vb.astype(F32).T.astype(BF16)

    def step(j, carry, diagonal):
        start = pl.multiple_of(j * t, t)
        scores = [_dot_nt(k_ref[pl.ds(start, t), h * qw:(h + 1) * qw], qs[h]) for h in range(heads)]
        soft = []
        for h in range(heads):
            m, l, _ = carry[h]
            s = scores[h]
            if diagonal:
                s = jnp.where(visible, s, NEG_BIG)
            m_new = jnp.maximum(m, jnp.max(s, axis=0, keepdims=True))
            alpha = jnp.exp2((m - m_new) * exp2_scale)
            pr = jnp.exp2((s - m_new) * exp2_scale)
            l = alpha * l + jnp.sum(pr, axis=0, keepdims=True)
            soft.append((m_new, l, alpha, pr.astype(BF16)))
        out = []
        for h in range(heads):
            m_new, l, alpha, pr = soft[h]
            out.append((m_new, l, alpha * carry[h][2] + _dot(vt_ref[h, j], pr)))
        return tuple(out)

    init = tuple((jnp.full((1, t), NEG_BIG, F32), jnp.zeros((1, t), F32), jnp.zeros((MLA_V, t), F32))
                 for _ in range(heads))
    carry = lax.fori_loop(0, i, lambda j, c: step(j, c, False), init)
    carry = step(i, carry, True)
    for h in range(heads):
        hs = slice(h * LANES, (h + 1) * LANES)
        _, l, acc = carry[h]
        o_ref[:, hs] = ((acc / l).T * _silu(g_ref[:, hs])).astype(o_ref.dtype)


def _mla_attn(q, k, v, p, batch, seq, t=ATTN_TILE, heads=MLA_GROUP):
    nq = seq // t
    qw = heads * 2 * LANES
    vw = heads * LANES
    gate0 = (SEC_MLA + Q_LORA + KV_LORA) // vw
    return pl.pallas_call(
        functools.partial(_mla_attn_kernel, t=t, heads=heads),
        out_shape=jax.ShapeDtypeStruct((batch * seq, MLA_WIDTH), BF16),
        grid=(batch, MLA_HEADS // heads, nq),
        in_specs=[
            pl.BlockSpec((t, qw), lambda b, h, i: (b * nq + i, h)),
            pl.BlockSpec((seq, qw), lambda b, h, i: (b, h)),
            pl.BlockSpec((seq, vw), lambda b, h, i: (b, h)),
            pl.BlockSpec((t, vw), lambda b, h, i: (b * nq + i, gate0 + h)),
        ],
        out_specs=pl.BlockSpec((t, vw), lambda b, h, i: (b * nq + i, h)),
        scratch_shapes=[pltpu.VMEM((heads, nq, MLA_V, t), BF16)],
        compiler_params=_params(("parallel", "parallel", "arbitrary")),
        name="mla_attn",
    )(q, k, v, p)


def _rw_prep_kernel(cur_ref, prev_ref, smc_ref, smp_ref, mu_ref, mus_ref, wlora_ref, w0_ref, a0_ref,
                    kk_ref, ka_ref, ones_ref, r_ref, lw_ref, k_ref, v_ref, na_ref, b_ref,
                    *, tm, seq):
    i = pl.program_id(0)
    at_start = (i * tm) % seq == 0
    row = lax.broadcasted_iota(jnp.int32, (tm, 1), 0)

    def shifted(c_ref, p_ref, mu):
        cur = c_ref[...]
        last = jnp.where(at_start, 0.0, p_ref[...][7:8, :])
        prev = jnp.where(row == 0, last, pltpu.roll(cur, 1, axis=0))
        return cur + mu * (prev - cur)

    main = shifted(cur_ref, prev_ref, mu_ref[...])
    small = shifted(smc_ref, smp_ref, mus_ref[...])[:, LANES:]
    r = main[:, :RW_WIDTH]
    k = main[:, RW_WIDTH:2 * RW_WIDTH]
    v = main[:, 2 * RW_WIDTH:]

    lane = lax.broadcasted_iota(jnp.int32, (1, LANES), 1)
    lora_in = jnp.where(lane < DECAY_LORA, jnp.tanh(small), small).astype(BF16)
    lora = _dot(lora_in, wlora_ref[...])
    w_log = -_softplus(-(w0_ref[...] + lora[:, :RW_WIDTH])) - 0.5
    a = _sigmoid(a0_ref[...] + lora[:, RW_WIDTH:])

    kk = k * kk_ref[...]
    hi, lo = _split_bf16(kk * kk)
    ss = _dot(hi, ones_ref[...]) + _dot(lo, ones_ref[...])
    kk = kk / jnp.maximum(jnp.sqrt(ss), 1e-12)

    r_ref[...] = r
    lw_ref[...] = -jnp.exp(w_log)
    k_ref[...] = k * (1.0 + (a - 1.0) * ka_ref[...])
    v_ref[...] = v
    na_ref[...] = -kk
    b_ref[...] = kk * a


def _rw_prep(p, mu, mus, wlora, w0, a0, k_k, k_a, ones_bd, seq, tm=512):
    m = p.shape[0]
    vec = lambda width: pl.BlockSpec((1, width), lambda i: (0, 0))
    out = jax.ShapeDtypeStruct((m, RW_WIDTH), F32)
    main_w = 3 * RW_WIDTH
    prev_blk = lambda i: (jnp.maximum(i * (tm // 8) - 1, 0))
    return pl.pallas_call(
        functools.partial(_rw_prep_kernel, tm=tm, seq=seq),
        out_shape=(out,) * 6,
        grid=(m // tm,),
        in_specs=[
            pl.BlockSpec((tm, main_w), lambda i: (i, SEC_RW // main_w)),
            pl.BlockSpec((8, main_w), lambda i: (prev_blk(i), SEC_RW // main_w)),
            pl.BlockSpec((tm, 256), lambda i: (i, SEC_SMALL // 256)),
            pl.BlockSpec((8, 256), lambda i: (prev_blk(i), SEC_SMALL // 256)),
            vec(main_w), vec(256),
            pl.BlockSpec(wlora.shape, lambda i: (0, 0)),
            vec(RW_WIDTH), vec(RW_WIDTH), vec(RW_WIDTH), vec(RW_WIDTH),
            pl.BlockSpec((RW_WIDTH, RW_WIDTH), lambda i: (0, 0)),
        ],
        out_specs=(pl.BlockSpec((tm, RW_WIDTH), lambda i: (i, 0)),) * 6,
        compiler_params=_params(("parallel",)),
        name="rw_prep",
    )(p, p, p, p, mu, mus, wlora, w0, a0, k_k, k_a, ones_bd)


_HI = lax.Precision.HIGHEST

RW_MODE_SCORE = "b1"
RW_MODE_INV = "b1"
RW_MODE_MID = "b1"
RW_MODE_STATE = "b1"


def _mm(a, b, mode, form="nn"):
    f = {"nn": _dot, "nt": _dot_nt, "tn": _dot_tn}[form]
    if mode == "hi":
        return f(a, b, _HI)
    if mode == "b1":
        return f(a.astype(BF16), b.astype(BF16))
    a_hi, a_lo = _split_bf16(a)
    b_hi, b_lo = _split_bf16(b)
    return f(a_hi, b_hi) + (f(a_hi, b_lo) + f(a_lo, b_hi))


def _rw_chunks(probs, states):
    c = RW_CHUNK
    n = 2 * c
    lane = lax.broadcasted_iota(jnp.int32, (1, LANES), 1)
    first = lane < RW_HEAD_DIM
    ti = lax.broadcasted_iota(jnp.int32, (c, c), 0)
    si = lax.broadcasted_iota(jnp.int32, (c, c), 1)
    tri = (si <= ti).astype(BF16)
    ri = lax.broadcasted_iota(jnp.int32, (n, n), 0)
    ci = lax.broadcasted_iota(jnp.int32, (n, n), 1)
    strict = (ci % c) < (ri % c)
    incl = (ci % c) <= (ri % c)
    eye = ri == ci

    def stack(x):
        return jnp.concatenate([jnp.where(first, x, 0.0), jnp.where(first, 0.0, x)], axis=0)

    splits = []
    for _, lw, _, _, _, _ in probs:
        lw_hi, lw_lo = _split_bf16(lw)
        splits.append((lw_hi, lw_lo, (lw - lw_hi.astype(F32) - lw_lo.astype(F32)).astype(BF16)))
    cums = [_dot(tri, h) + (_dot(tri, l) + _dot(tri, l2)) for h, l, l2 in splits]

    feats = []
    for (r, lw, k, v, na, b), cum in zip(probs, cums):
        total = cum[c - 1:c, :]
        p_inv = jnp.exp(-cum)
        p_end = jnp.exp(total - cum)
        at = stack(na * jnp.exp(cum - lw))
        rt = stack(r * jnp.exp(cum))
        lhs = jnp.concatenate([at, rt], axis=0)
        rhs = jnp.concatenate([stack(b * p_inv), stack(k * p_inv)], axis=0)
        feats.append(dict(at=at, rt=rt, lhs=lhs, rhs=rhs, bh=stack(b * p_end), kh=stack(k * p_end),
                          vs=stack(v), decay=jnp.exp(total)))

    gs = [_mm(f["lhs"], f["rhs"], RW_MODE_SCORE, "nt") for f in feats]
    l_ab = [jnp.where(strict, g[:n, :n], 0.0) for g in gs]
    l_ak = [jnp.where(strict, g[:n, n:], 0.0) for g in gs]
    m_rb = [jnp.where(incl, g[n:, :n], 0.0) for g in gs]
    m_rk = [jnp.where(incl, g[n:, n:], 0.0) for g in gs]

    xs = l_ab
    tinv = [jnp.where(eye, 1.0, 0.0) + x for x in xs]
    for _ in range(int(math.log2(c)) - 1):
        xs = [_mm(x, x, RW_MODE_INV) for x in xs]
        tinv = [t + _mm(t, x, RW_MODE_INV) for t, x in zip(tinv, xs)]

    lv = [_mm(l, f["vs"], RW_MODE_MID) for l, f in zip(l_ak, feats)]
    wu = [_mm(t, jnp.concatenate([f["at"], y], axis=1), RW_MODE_MID)
          for t, f, y in zip(tinv, feats, lv)]
    ro = [_mm(m, w, RW_MODE_MID) for m, w in zip(m_rb, wu)]
    rv = [_mm(m, f["vs"], RW_MODE_MID) for m, f in zip(m_rk, feats)]
    ab = [_mm(w, f["bh"], RW_MODE_STATE, "tn") for w, f in zip(wu, feats)]
    vk = [_mm(f["vs"], f["kh"], RW_MODE_STATE, "tn") for f in feats]

    outs, new_states = [], []
    for i, f in enumerate(feats):
        rh = f["rt"] + ro[i][:, :LANES]
        o0 = ro[i][:, LANES:] + rv[i]
        a_mat = jnp.where(eye, f["decay"], 0.0) + ab[i][:LANES]
        b_mat = ab[i][LANES:] + vk[i]
        outs.append(_mm(rh[:c] + rh[c:], states[i], RW_MODE_STATE, "nt") + o0[:c] + o0[c:])
        new_states.append(_mm(states[i], a_mat, RW_MODE_STATE) + b_mat)
    return outs, new_states


def _rw_scan_kernel(r_ref, lw_ref, k_ref, v_ref, na_ref, b_ref, g_ref, rk_ref, gng_ref, gnb_ref,
                    o_ref, s_ref):
    @pl.when(pl.program_id(0) == 0)
    def _():
        s_ref[...] = jnp.zeros_like(s_ref)

    lane = lax.broadcasted_iota(jnp.int32, (1, LANES), 1)
    first = lane < RW_HEAD_DIM
    batch = r_ref.shape[0]
    pairs = RW_WIDTH // LANES
    where = [(bi, slice(pr * LANES, (pr + 1) * LANES)) for bi in range(batch) for pr in range(pairs)]

    def head_sum(y):
        s_first = jnp.sum(jnp.where(first, y, 0.0), axis=-1, keepdims=True)
        s_second = jnp.sum(jnp.where(first, 0.0, y), axis=-1, keepdims=True)
        return jnp.where(first, s_first, s_second)

    probs = [(r_ref[bi, :, sl], lw_ref[bi, :, sl], k_ref[bi, :, sl], v_ref[bi, :, sl], na_ref[bi, :, sl],
              b_ref[bi, :, sl]) for bi, sl in where]
    outs, new_states = _rw_chunks(probs, [s_ref[i] for i in range(len(where))])
    for i, (bi, sl) in enumerate(where):
        s_ref[i] = new_states[i]
        r, _, k, v, _, _ = probs[i]
        o = outs[i]
        mu = head_sum(o) / RW_HEAD_DIM
        d = o - mu
        var = head_sum(d * d) / RW_HEAD_DIM
        normed = d * lax.rsqrt(var + RW_GN_EPS) * gng_ref[:, sl] + gnb_ref[:, sl]
        bonus = head_sum(r * k * rk_ref[:, sl]) * v
        o_ref[bi, :, sl] = ((normed + bonus) * _silu(g_ref[bi, :, sl])).astype(o_ref.dtype)


def _rw_scan(r, lw, k, v, na, b, p, r_k, gn_g, gn_b, batch, seq):
    c = RW_CHUNK
    as3d = lambda t: t.reshape(batch, seq, t.shape[-1])
    tok = pl.BlockSpec((batch, c, RW_WIDTH), lambda ci: (0, ci, 0))
    vec = pl.BlockSpec((1, RW_WIDTH), lambda ci: (0, 0))
    gate = pl.BlockSpec((batch, c, RW_WIDTH), lambda ci: (0, ci, SEC_RG // RW_WIDTH))
    out = pl.pallas_call(
        _rw_scan_kernel,
        out_shape=jax.ShapeDtypeStruct((batch, seq, RW_WIDTH), BF16),
        grid=(seq // c,),
        in_specs=[tok] * 6 + [gate, vec, vec, vec],
        out_specs=tok,
        scratch_shapes=[pltpu.VMEM((batch * RW_WIDTH // LANES, LANES, LANES), F32)],
        compiler_params=_params(("arbitrary",)),
        name="rw_scan",
    )(*[as3d(t) for t in (r, lw, k, v, na, b, p)], r_k, gn_g, gn_b)
    return out.reshape(batch * seq, RW_WIDTH)


def _rw_branch(p, mu, w0, w_up, a0, a_up, k_k, k_a, r_k, gn_g, gn_b, batch, seq):
    row = lambda t: t[None, :]
    head_ones = jnp.kron(jnp.eye(RW_HEADS, dtype=F32), jnp.ones((RW_HEAD_DIM, RW_HEAD_DIM), F32)).astype(BF16)
    mu_small = jnp.concatenate([jnp.zeros((LANES,), F32), mu[3 * RW_WIDTH:]])
    zeros = jnp.zeros((DECAY_LORA, RW_WIDTH), F32)
    wlora = jnp.concatenate([jnp.concatenate([w_up, zeros], axis=1),
                             jnp.concatenate([zeros, a_up], axis=1)], axis=0).astype(BF16)
    rr, lw, rk, rv, na, rb = _rw_prep(p, row(mu[:3 * RW_WIDTH]), row(mu_small), wlora, row(w0), row(a0),
                                      row(k_k), row(k_a), head_ones, seq)
    return _rw_scan(rr, lw, rk, rv, na, rb, p, r_k.reshape(1, RW_WIDTH), row(gn_g), row(gn_b), batch, seq)


def _merge_out_kernel(x_ref, ysb_ref, ymla_ref, yrw_ref, wsb_ref, wmla_ref, wrw_ref, g1_ref, g2_ref, g3_ref,
                      wout_ref, o_ref):
    merged = _sigmoid(g1_ref[...]) * _dot(ysb_ref[...], wsb_ref[...])
    merged = merged + _sigmoid(g2_ref[...]) * _dot(ymla_ref[...], wmla_ref[...])
    merged = merged + _sigmoid(g3_ref[...]) * _dot(yrw_ref[...], wrw_ref[...])
    o_ref[...] = x_ref[...] + _dot(merged.astype(BF16), wout_ref[...])


def _merge_out(x2, y_sb, y_mla, y_rw, w_sb, w_mla, w_rw, p, w_out, tm=256):
    m = p.shape[0]
    rows = lambda width: pl.BlockSpec((tm, width), lambda i: (i, 0))
    resident = lambda depth: pl.BlockSpec((depth, D_MODEL), lambda i: (0, 0), pipeline_mode=pl.Buffered(1))
    gate = lambda br: pl.BlockSpec((tm, D_MODEL), lambda i: (i, SEC_GATE // D_MODEL + br))
    return pl.pallas_call(
        _merge_out_kernel,
        out_shape=jax.ShapeDtypeStruct((m, D_MODEL), F32),
        grid=(m // tm,),
        in_specs=[rows(D_MODEL), rows(SB_WIDTH), rows(MLA_WIDTH), rows(RW_WIDTH),
                  resident(SB_WIDTH), resident(MLA_WIDTH), resident(RW_WIDTH),
                  gate(0), gate(1), gate(2), resident(D_MODEL)],
        out_specs=rows(D_MODEL),
        compiler_params=_params(("parallel",)),
        name="merge_out",
    )(x2, y_sb, y_mla, y_rw, w_sb, w_mla, w_rw, p, p, p, w_out)


def _reorder_w_in(w):
    kr = 3072
    mg = kr + MLA_ROPE
    rw = mg + MLA_WIDTH
    lo = rw + 3 * RW_WIDTH
    rg = lo + DECAY_LORA + ICL_LORA
    gt = rg + RW_WIDTH
    x1 = w[:, kr:kr + HALF_ROPE]
    x2 = w[:, kr + HALF_ROPE:mg]
    parts = [w[:, :kr], w[:, mg:rw], w[:, rg:gt], w[:, rw:lo], w[:, gt:], x1, x1, x2, x2, w[:, lo:rg]]
    return jnp.concatenate([part.astype(BF16) for part in parts], axis=1)


def _pair_rope_cols(t):
    lead = t.shape[:-2]
    t = t.reshape(lead + (MLA_HEADS // 2, 2, 2, HALF_ROPE))
    t = jnp.swapaxes(t, -3, -2)
    return t.reshape(lead + (MLA_HEADS // 2 * LANES,))


def _pair_rope_gain(g):
    g1, g2 = g[:HALF_ROPE], g[HALF_ROPE:]
    return jnp.concatenate([g1, g1, g2, g2])[None, :]


def _rope_tables(seq):
    freqs = ROPE_THETA ** (-jnp.arange(HALF_ROPE, dtype=F32) / HALF_ROPE)
    ang = jnp.arange(seq, dtype=F32)[:, None] * freqs[None, :]
    c, s = jnp.cos(ang), jnp.sin(ang)
    return jnp.concatenate([c, c, c, c], axis=1), jnp.concatenate([-s, -s, s, s], axis=1)


def _mla_branch(p, cos, sin, q_norm_g, kv_norm_g, w_uq, w_ukv, qn_g, kn_g, batch, seq):
    row = lambda t: t[None, :]
    uq = w_uq.reshape(Q_LORA, MLA_HEADS, MLA_QK)
    wuq = jnp.concatenate([uq[:, :, :MLA_NOPE].reshape(Q_LORA, -1), _pair_rope_cols(uq[:, :, MLA_NOPE:])],
                          axis=1).astype(BF16)
    ukv = w_ukv.reshape(KV_LORA, MLA_HEADS, MLA_NOPE + MLA_V)
    wukv = jnp.concatenate([ukv[:, :, :MLA_NOPE].reshape(KV_LORA, -1),
                            ukv[:, :, MLA_NOPE:].reshape(KV_LORA, -1)], axis=1).astype(BF16)
    q, k, v = _mla_prep(p, cos, sin, row(q_norm_g), row(kv_norm_g), wuq, wukv,
                        row(qn_g[:MLA_NOPE]), _pair_rope_gain(qn_g[MLA_NOPE:]),
                        row(kn_g[:MLA_NOPE]), _pair_rope_gain(kn_g[MLA_NOPE:]), seq)
    return _mla_attn(q, k, v, p, batch, seq)


def kernel(x, norm_g, w_in, mla_q_norm_g, mla_kv_norm_g, mla_w_uq, mla_w_ukv, mla_qn_g, mla_kn_g,
           rw_mu, rw_w0, rw_w_up, rw_a0, rw_a_up, rw_k_k, rw_k_a, rw_r_k, rw_gn_g, rw_gn_b,
           w_br_sb, w_br_mla, w_br_rw, w_out):
    batch, seq, _ = x.shape
    depth = w_in.shape[0]
    x2 = x.reshape(batch * seq, D_MODEL)
    cos, sin = _rope_tables(seq)

    for l in range(depth):
        p = _in_proj(x2, norm_g[l][None, :], _reorder_w_in(w_in[l]))
        y_sb = _sb_attn(p, batch, seq)
        y_mla = _mla_branch(p, cos, sin, mla_q_norm_g[l], mla_kv_norm_g[l], mla_w_uq[l], mla_w_ukv[l],
                            mla_qn_g[l], mla_kn_g[l], batch, seq)
        y_rw = _rw_branch(p, rw_mu[l], rw_w0[l], rw_w_up[l], rw_a0[l], rw_a_up[l], rw_k_k[l], rw_k_a[l],
                          rw_r_k[l], rw_gn_g[l], rw_gn_b[l], batch, seq)
        x2 = _merge_out(x2, y_sb, y_mla, y_rw, w_br_sb[l].astype(BF16), w_br_mla[l].astype(BF16),
                        w_br_rw[l].astype(BF16), p, w_out[l].astype(BF16))

    return x2.reshape(batch, seq, D_MODEL)
```

```python
import functools
import math

import jax
import jax.numpy as jnp
from jax import lax
from jax.experimental import pallas as pl
from jax.experimental.pallas import tpu as pltpu

F32 = jnp.float32
BF16 = jnp.bfloat16

D_MODEL = 2048
EPS = 1e-6
CHUNK = 64

SB_HEADS = 4
SB_HEAD_DIM = 128
SB_WIDTH = SB_HEADS * SB_HEAD_DIM

MLA_HEADS = 8
MLA_NOPE = 128
MLA_ROPE = 64
MLA_QK = MLA_NOPE + MLA_ROPE
MLA_V = 128
MLA_WIDTH = MLA_HEADS * MLA_V
Q_LORA = 512
KV_LORA = 512
ROPE_THETA = 10000.0
HALF_ROPE = MLA_ROPE // 2

RW_HEADS = 8
RW_HEAD_DIM = 64
RW_WIDTH = RW_HEADS * RW_HEAD_DIM
DECAY_LORA = 64
ICL_LORA = 64
RW_GN_EPS = 64e-5
RW_CHUNK = 64

LANES = 128
VMEM_LIMIT = 56 * 1024 * 1024

SEC_SB = 0
SEC_MLA = 2048
SEC_RG = 4096
SEC_RW = 4608
SEC_GATE = 6144
SEC_SMALL = 12288
N_PROJ = SEC_SMALL + 256

_ORIG_KR = 4 * SB_WIDTH + Q_LORA + KV_LORA
_ORIG_MG = _ORIG_KR + MLA_ROPE
_ORIG_RW = _ORIG_MG + MLA_WIDTH
_ORIG_LORA = _ORIG_RW + 3 * RW_WIDTH
_ORIG_RG = _ORIG_LORA + DECAY_LORA + ICL_LORA
_ORIG_GATE = _ORIG_RG + RW_WIDTH
W_IN_MOVES = (
    (0, 0, _ORIG_KR),
    (SEC_MLA + Q_LORA + KV_LORA, _ORIG_MG, MLA_WIDTH),
    (SEC_RG, _ORIG_RG, RW_WIDTH),
    (SEC_RW, _ORIG_RW, 3 * RW_WIDTH),
    (SEC_GATE, _ORIG_GATE, 3 * D_MODEL),
    (SEC_SMALL, _ORIG_KR, HALF_ROPE),
    (SEC_SMALL + HALF_ROPE, _ORIG_KR, HALF_ROPE),
    (SEC_SMALL + 2 * HALF_ROPE, _ORIG_KR + HALF_ROPE, HALF_ROPE),
    (SEC_SMALL + 3 * HALF_ROPE, _ORIG_KR + HALF_ROPE, HALF_ROPE),
    (SEC_SMALL + LANES, _ORIG_LORA, DECAY_LORA + ICL_LORA),
)

NEG_BIG = -1e30

SB_DEAD_LOG_WEIGHT = -104.0

ATTN_TILE = 256
SB_GROUP = 4
MLA_GROUP = 4


def _dot(a, b, precision=None):
    return jnp.dot(a, b, preferred_element_type=F32, precision=precision)


def _dot_nt(a, b, precision=None):
    return lax.dot_general(a, b, (((1,), (1,)), ((), ())), preferred_element_type=F32,
                           precision=precision)


def _dot_tn(a, b, precision=None):
    return lax.dot_general(a, b, (((0,), (0,)), ((), ())), preferred_element_type=F32,
                           precision=precision)


def _split_bf16(x):
    hi = x.astype(BF16)
    lo = (x - hi.astype(F32)).astype(BF16)
    return hi, lo


def _softplus(z):
    return jnp.maximum(z, 0.0) + jnp.log(1.0 + jnp.exp(-jnp.abs(z)))


def _sigmoid(z):
    return 1.0 / (1.0 + jnp.exp(-z))


def _silu(z):
    return z * _sigmoid(z)


def _params(semantics):
    return pltpu.CompilerParams(dimension_semantics=semantics, vmem_limit_bytes=VMEM_LIMIT)


def _in_proj_kernel(x_ref, g_ref, w_ref, o_ref, h_ref):
    @pl.when(pl.program_id(1) == 0)
    def _():
        x = x_ref[...]
        rs = lax.rsqrt(jnp.mean(x * x, axis=-1, keepdims=True) + EPS)
        h_ref[...] = (x * rs * g_ref[...]).astype(BF16)

    o_ref[...] = _dot(h_ref[...], w_ref[...])


def _w_in_relayout_kernel(w_ref, o_ref):
    for dst, src, width in W_IN_MOVES:
        o_ref[:, dst:dst + width] = w_ref[:, src:src + width].astype(BF16)


def _w_in_relayout(w_in, tk=128):
    depth = w_in.shape[0]
    return pl.pallas_call(
        _w_in_relayout_kernel,
        out_shape=jax.ShapeDtypeStruct((depth, D_MODEL, N_PROJ), BF16),
        grid=(depth, D_MODEL // tk),
        in_specs=[pl.BlockSpec((None, tk, w_in.shape[2]), lambda l, i: (l, i, 0))],
        out_specs=pl.BlockSpec((None, tk, N_PROJ), lambda l, i: (l, i, 0)),
        compiler_params=_params(("parallel", "parallel")),
        name="w_in_relayout",
    )(w_in)


def _in_proj(x2, g, w, layer, tm=1024, tn=1792):
    m = x2.shape[0]
    return pl.pallas_call(
        _in_proj_kernel,
        out_shape=jax.ShapeDtypeStruct((m, N_PROJ), F32),
        grid=(m // tm, N_PROJ // tn),
        in_specs=[
            pl.BlockSpec((tm, D_MODEL), lambda i, j: (i, 0), pipeline_mode=pl.Buffered(1)),
            pl.BlockSpec((1, D_MODEL), lambda i, j: (0, 0)),
            pl.BlockSpec((None, D_MODEL, tn), lambda i, j: (layer, 0, j)),
        ],
        out_specs=pl.BlockSpec((tm, tn), lambda i, j: (i, j)),
        scratch_shapes=[pltpu.VMEM((tm, D_MODEL), BF16)],
        compiler_params=_params(("parallel", "arbitrary")),
        name="in_proj",
    )(x2, g, w)


def _sb_attn_kernel(q_ref, k_ref, v_ref, g_ref, o_ref, kb_ref, vt_ref, *, t, heads):
    i = pl.program_id(2)
    nblk = k_ref.shape[0] // t
    scale = SB_HEAD_DIM ** -0.5
    key = lax.broadcasted_iota(jnp.int32, (t, t), 0)
    qry = lax.broadcasted_iota(jnp.int32, (t, t), 1)
    later = (qry > key).astype(BF16)
    causal = key < qry
    head = lambda h: slice(h * LANES, (h + 1) * LANES)
    qs = [q_ref[:, head(h)].astype(BF16) for h in range(heads)]

    @pl.when(i == 0)
    def _():
        for h in range(heads):
            for blk in range(nblk):
                rows = slice(blk * t, (blk + 1) * t)
                kb_ref[h, blk] = k_ref[rows, head(h)].astype(BF16)
                vt_ref[h, blk] = v_ref[rows, head(h)].T.astype(BF16)

    def step(j, carry, diagonal):
        zs = [_dot_nt(kb_ref[h, j], qs[h]) for h in range(heads)]
        stage = []
        for h in range(heads):
            z = zs[h] * scale
            sp = _softplus(z)
            log_fail = jnp.where(causal, -sp, 0.0) if diagonal else -sp
            hi, lo = _split_bf16(log_fail)
            stage.append((z - sp, log_fail, hi, lo))
        sums = [_dot(later, hi) + _dot(later, lo) for _, _, hi, lo in stage]
        ws = []
        for h in range(heads):
            w = jnp.exp(stage[h][0] + sums[h] + carry[h][0])
            if diagonal:
                w = jnp.where(causal, w, 0.0)
            ws.append(w.astype(BF16))
        out = []
        for h in range(heads):
            run, acc = carry[h]
            out.append((run + jnp.sum(stage[h][1], axis=0, keepdims=True), acc + _dot(vt_ref[h, j], ws[h])))
        return tuple(out)

    init = tuple((jnp.zeros((1, t), F32), jnp.zeros((SB_HEAD_DIM, t), F32)) for _ in range(heads))
    carry = step(i, init, True)

    def alive(c):
        return functools.reduce(jnp.maximum, [jnp.max(c[h][0]) for h in range(heads)]) >= SB_DEAD_LOG_WEIGHT

    def body(state):
        n, _, c = state
        c = step(i - 1 - n, c, False)
        return n + 1, alive(c), c

    _, _, carry = lax.while_loop(lambda s: jnp.logical_and(s[0] < i, s[1]), body, (0, alive(carry), carry))
    for h in range(heads):
        o_ref[:, head(h)] = (carry[h][1].T * _silu(g_ref[:, head(h)])).astype(o_ref.dtype)


def _sb_attn(p, batch, seq, t=ATTN_TILE, heads=SB_GROUP):
    nq = seq // t
    w = heads * LANES
    sec = SB_WIDTH // w
    tile = lambda s: pl.BlockSpec((t, w), lambda b, h, i: (b * nq + i, s * sec + h))
    full = lambda s: pl.BlockSpec((seq, w), lambda b, h, i: (b, s * sec + h))
    return pl.pallas_call(
        functools.partial(_sb_attn_kernel, t=t, heads=heads),
        out_shape=jax.ShapeDtypeStruct((batch * seq, SB_WIDTH), BF16),
        grid=(batch, SB_HEADS // heads, nq),
        in_specs=[tile(0), full(1), full(2), tile(3)],
        out_specs=pl.BlockSpec((t, w), lambda b, h, i: (b * nq + i, h)),
        scratch_shapes=[pltpu.VMEM((heads, nq, t, SB_HEAD_DIM), BF16),
                        pltpu.VMEM((heads, nq, SB_HEAD_DIM, t), BF16)],
        compiler_params=_params(("parallel", "parallel", "arbitrary")),
        name="sb_attn",
    )(p, p, p, p)


def _mla_prep_kernel(cq_ref, ckv_ref, sm_ref, cos_ref, sin_ref, gq_ref, gkv_ref, wuq_ref, wukv_ref,
                     gqn_ref, gqr_ref, gkn_ref, gkr_ref, q_ref, k_ref, v_ref):
    lane = lax.broadcasted_iota(jnp.int32, (1, LANES), 1)
    first = (lane % MLA_ROPE) < HALF_ROPE
    cos = cos_ref[...]
    sin = sin_ref[...]

    def latent_norm(c_ref, g_ref):
        c = c_ref[...]
        rs = lax.rsqrt(jnp.mean(c * c, axis=-1, keepdims=True) + EPS)
        return (c * rs * g_ref[...]).astype(BF16)

    def rotary(y):
        return y * cos + pltpu.roll(y, MLA_ROPE, axis=1) * sin

    def head_sums(sq):
        s_first = jnp.sum(jnp.where(first, sq, 0.0), axis=-1, keepdims=True)
        s_second = jnp.sum(jnp.where(first, 0.0, sq), axis=-1, keepdims=True)
        return s_first, s_second

    qfull = _dot(latent_norm(cq_ref, gq_ref), wuq_ref[...])
    kvfull = _dot(latent_norm(ckv_ref, gkv_ref), wukv_ref[...])
    v_ref[...] = kvfull[:, MLA_HEADS * MLA_NOPE:].astype(BF16)

    kr = sm_ref[...][:, :LANES]
    kr_ss, _ = head_sums(kr * kr)
    kr_rot = rotary(kr * gkr_ref[...])

    for pair in range(MLA_HEADS // 2):
        qr = qfull[:, MLA_HEADS * MLA_NOPE + pair * LANES:MLA_HEADS * MLA_NOPE + (pair + 1) * LANES]
        qr_ss = head_sums(qr * qr)
        q_rs, k_rs = [], []
        for e in range(2):
            h = 2 * pair + e
            qn = qfull[:, h * MLA_NOPE:(h + 1) * MLA_NOPE]
            kn = kvfull[:, h * MLA_NOPE:(h + 1) * MLA_NOPE]
            qs = lax.rsqrt((jnp.sum(qn * qn, axis=-1, keepdims=True) + qr_ss[e]) / MLA_QK + EPS)
            ks = lax.rsqrt((jnp.sum(kn * kn, axis=-1, keepdims=True) + kr_ss) / MLA_QK + EPS)
            q_rs.append(qs)
            k_rs.append(ks)
            q_ref[:, 2 * h * LANES:(2 * h + 1) * LANES] = (qn * qs * gqn_ref[...]).astype(BF16)
            k_ref[:, 2 * h * LANES:(2 * h + 1) * LANES] = (kn * ks * gkn_ref[...]).astype(BF16)
        q_rot = rotary(qr * jnp.where(first, q_rs[0], q_rs[1]) * gqr_ref[...])
        k_rot = (kr_rot * jnp.where(first, k_rs[0], k_rs[1])).astype(BF16)
        for e in range(2):
            h = 2 * pair + e
            own = first if e == 0 else jnp.logical_not(first)
            q_ref[:, (2 * h + 1) * LANES:(2 * h + 2) * LANES] = jnp.where(own, q_rot, 0.0).astype(BF16)
            k_ref[:, (2 * h + 1) * LANES:(2 * h + 2) * LANES] = k_rot


def _mla_prep(p, cos, sin, gq, gkv, wuq, wukv, gqn, gqr, gkn, gkr, seq, tm=512):
    m = p.shape[0]
    sblk = seq // tm
    row = lambda width, cb: pl.BlockSpec((tm, width), lambda i: (i, cb))
    const = lambda shape: pl.BlockSpec(shape, lambda i: (0, 0))
    pos = pl.BlockSpec((tm, LANES), lambda i: (i % sblk, 0))
    qk_width = MLA_HEADS * 2 * LANES
    return pl.pallas_call(
        _mla_prep_kernel,
        out_shape=(jax.ShapeDtypeStruct((m, qk_width), BF16),
                   jax.ShapeDtypeStruct((m, qk_width), BF16),
                   jax.ShapeDtypeStruct((m, MLA_WIDTH), BF16)),
        grid=(m // tm,),
        in_specs=[
            row(Q_LORA, SEC_MLA // Q_LORA),
            row(KV_LORA, SEC_MLA // KV_LORA + 1),
            row(256, SEC_SMALL // 256),
            pos, pos,
            const((1, Q_LORA)), const((1, KV_LORA)),
            const(wuq.shape), const(wukv.shape),
            const((1, LANES)), const((1, LANES)), const((1, LANES)), const((1, LANES)),
        ],
        out_specs=(row(qk_width, 0), row(qk_width, 0), row(MLA_WIDTH, 0)),
        compiler_params=_params(("parallel",)),
        name="mla_prep",
    )(p, p, p, cos, sin, gq, gkv, wuq, wukv, gqn, gqr, gkn, gkr)


def _mla_attn_kernel(q_ref, k_ref, v_ref, g_ref, o_ref, vt_ref, sa_ref, sb_ref, sd_ref, m_ref, l_ref, acc_ref,
                     *, t, heads):
    i = pl.program_id(2)
    nblk = v_ref.shape[0] // t
    exp2_scale = (MLA_QK ** -0.5) * math.log2(math.e)
    qw = 2 * LANES
    key_chunk = lax.broadcasted_iota(jnp.int32, (t, t), 0) // CHUNK
    qry_chunk = lax.broadcasted_iota(jnp.int32, (t, t), 1) // CHUNK
    visible = key_chunk <= qry_chunk
    qs = [q_ref[:, h * qw:(h + 1) * qw] for h in range(heads)]

    @pl.when(i == 0)
    def _():
        for h in range(heads):
            for blk in range(nblk):
                vb = v_ref[blk * t:(blk + 1) * t, h * LANES:(h + 1) * LANES]
                vt_ref[h, blk] = vb.astype(F32).T.astype(BF16)

    m_ref[...] = jnp.full(m_ref.shape, NEG_BIG, F32)
    l_ref[...] = jnp.zeros(l_ref.shape, F32)
    acc_ref[...] = jnp.zeros(acc_ref.shape, F32)

    def issue_scores(j, buf):
        start = pl.multiple_of(j * t, t)
        for h in range(heads):
            buf[h] = _dot_nt(k_ref[pl.ds(start, t), h * qw:(h + 1) * qw], qs[h])

    def absorb(buf, j, diagonal):
        soft = []
        for h in range(heads):
            s = jnp.where(visible, buf[h], NEG_BIG) if diagonal else buf[h]
            m = m_ref[h]
            m_new = jnp.maximum(m, jnp.max(s, axis=0, keepdims=True))
            alpha = jnp.exp2((m - m_new) * exp2_scale)
            pr = jnp.exp2((s - m_new) * exp2_scale)
            m_ref[h] = m_new
            l_ref[h] = alpha * l_ref[h] + jnp.sum(pr, axis=0, keepdims=True)
            soft.append((alpha, pr.astype(BF16)))
        for h in range(heads):
            alpha, pr = soft[h]
            acc_ref[h] = alpha * acc_ref[h] + _dot(vt_ref[h, j], pr)

    issue_scores(i, sd_ref)
    issue_scores(0, sa_ref)
    last = jnp.maximum(i - 1, 0)

    def pair(n, _):
        issue_scores(2 * n + 1, sb_ref)
        absorb(sa_ref, 2 * n, False)
        issue_scores(jnp.minimum(2 * n + 2, last), sa_ref)
        absorb(sb_ref, 2 * n + 1, False)
        return 0

    lax.fori_loop(0, i // 2, pair, 0)

    @pl.when(i % 2 == 1)
    def _():
        absorb(sa_ref, i - 1, False)

    absorb(sd_ref, i, True)
    for h in range(heads):
        hs = slice(h * LANES, (h + 1) * LANES)
        o_ref[:, hs] = ((acc_ref[h] / l_ref[h]).T * _silu(g_ref[:, hs])).astype(o_ref.dtype)


def _mla_attn(q, k, v, p, batch, seq, t=ATTN_TILE, heads=MLA_GROUP):
    nq = seq // t
    qw = heads * 2 * LANES
    vw = heads * LANES
    gate0 = (SEC_MLA + Q_LORA + KV_LORA) // vw
    return pl.pallas_call(
        functools.partial(_mla_attn_kernel, t=t, heads=heads),
        out_shape=jax.ShapeDtypeStruct((batch * seq, MLA_WIDTH), BF16),
        grid=(batch, MLA_HEADS // heads, nq),
        in_specs=[
            pl.BlockSpec((t, qw), lambda b, h, i: (b * nq + i, h)),
            pl.BlockSpec((seq, qw), lambda b, h, i: (b, h)),
            pl.BlockSpec((seq, vw), lambda b, h, i: (b, h)),
            pl.BlockSpec((t, vw), lambda b, h, i: (b * nq + i, gate0 + h)),
        ],
        out_specs=pl.BlockSpec((t, vw), lambda b, h, i: (b * nq + i, h)),
        scratch_shapes=[pltpu.VMEM((heads, nq, MLA_V, t), BF16)]
        + [pltpu.VMEM((heads, t, t), F32)] * 3
        + [pltpu.VMEM((heads, 1, t), F32)] * 2
        + [pltpu.VMEM((heads, MLA_V, t), F32)],
        compiler_params=_params(("parallel", "parallel", "arbitrary")),
        name="mla_attn",
    )(q, k, v, p)


def _rw_prep_kernel(cur_ref, prev_ref, smc_ref, smp_ref, mu_ref, mus_ref, wlora_ref, w0_ref, a0_ref,
                    kk_ref, ka_ref, ones_ref, r_ref, lw_ref, k_ref, v_ref, na_ref, b_ref,
                    *, tm, seq):
    i = pl.program_id(0)
    at_start = (i * tm) % seq == 0
    row = lax.broadcasted_iota(jnp.int32, (tm, 1), 0)

    def shifted(c_ref, p_ref, mu):
        cur = c_ref[...]
        last = jnp.where(at_start, 0.0, p_ref[...][7:8, :])
        prev = jnp.where(row == 0, last, pltpu.roll(cur, 1, axis=0))
        return cur + mu * (prev - cur)

    main = shifted(cur_ref, prev_ref, mu_ref[...])
    small = shifted(smc_ref, smp_ref, mus_ref[...])[:, LANES:]
    r = main[:, :RW_WIDTH]
    k = main[:, RW_WIDTH:2 * RW_WIDTH]
    v = main[:, 2 * RW_WIDTH:]

    lane = lax.broadcasted_iota(jnp.int32, (1, LANES), 1)
    lora_in = jnp.where(lane < DECAY_LORA, jnp.tanh(small), small).astype(BF16)
    lora = _dot(lora_in, wlora_ref[...])
    w_log = -_softplus(-(w0_ref[...] + lora[:, :RW_WIDTH])) - 0.5
    a = _sigmoid(a0_ref[...] + lora[:, RW_WIDTH:])

    kk = k * kk_ref[...]
    hi, lo = _split_bf16(kk * kk)
    ss = _dot(hi, ones_ref[...]) + _dot(lo, ones_ref[...])
    kk = kk / jnp.maximum(jnp.sqrt(ss), 1e-12)

    r_ref[...] = r
    lw_ref[...] = -jnp.exp(w_log)
    k_ref[...] = k * (1.0 + (a - 1.0) * ka_ref[...])
    v_ref[...] = v
    na_ref[...] = -kk
    b_ref[...] = kk * a


def _rw_prep(p, mu, mus, wlora, w0, a0, k_k, k_a, ones_bd, seq, tm=512):
    m = p.shape[0]
    vec = lambda width: pl.BlockSpec((1, width), lambda i: (0, 0))
    out = jax.ShapeDtypeStruct((m, RW_WIDTH), F32)
    main_w = 3 * RW_WIDTH
    prev_blk = lambda i: (jnp.maximum(i * (tm // 8) - 1, 0))
    return pl.pallas_call(
        functools.partial(_rw_prep_kernel, tm=tm, seq=seq),
        out_shape=(out,) * 6,
        grid=(m // tm,),
        in_specs=[
            pl.BlockSpec((tm, main_w), lambda i: (i, SEC_RW // main_w)),
            pl.BlockSpec((8, main_w), lambda i: (prev_blk(i), SEC_RW // main_w)),
            pl.BlockSpec((tm, 256), lambda i: (i, SEC_SMALL // 256)),
            pl.BlockSpec((8, 256), lambda i: (prev_blk(i), SEC_SMALL // 256)),
            vec(main_w), vec(256),
            pl.BlockSpec(wlora.shape, lambda i: (0, 0)),
            vec(RW_WIDTH), vec(RW_WIDTH), vec(RW_WIDTH), vec(RW_WIDTH),
            pl.BlockSpec((RW_WIDTH, RW_WIDTH), lambda i: (0, 0)),
        ],
        out_specs=(pl.BlockSpec((tm, RW_WIDTH), lambda i: (i, 0)),) * 6,
        compiler_params=_params(("parallel",)),
        name="rw_prep",
    )(p, p, p, p, mu, mus, wlora, w0, a0, k_k, k_a, ones_bd)


_HI = lax.Precision.HIGHEST

RW_MODE_SCORE = "b1"
RW_MODE_INV = "b1"
RW_MODE_MID = "b1"
RW_MODE_STATE = "b1"


def _mm(a, b, mode, form="nn"):
    f = {"nn": _dot, "nt": _dot_nt, "tn": _dot_tn}[form]
    if mode == "hi":
        return f(a, b, _HI)
    if mode == "b1":
        return f(a.astype(BF16), b.astype(BF16))
    a_hi, a_lo = _split_bf16(a)
    b_hi, b_lo = _split_bf16(b)
    return f(a_hi, b_hi) + (f(a_hi, b_lo) + f(a_lo, b_hi))


def _rw_chunks(probs, states):
    c = RW_CHUNK
    n = 2 * c
    lane = lax.broadcasted_iota(jnp.int32, (1, LANES), 1)
    first = lane < RW_HEAD_DIM
    ti = lax.broadcasted_iota(jnp.int32, (c, c), 0)
    si = lax.broadcasted_iota(jnp.int32, (c, c), 1)
    tri = (si <= ti).astype(BF16)
    ri = lax.broadcasted_iota(jnp.int32, (n, n), 0)
    ci = lax.broadcasted_iota(jnp.int32, (n, n), 1)
    strict = (ci % c) < (ri % c)
    incl = (ci % c) <= (ri % c)
    eye = ri == ci

    def stack(x):
        return jnp.concatenate([jnp.where(first, x, 0.0), jnp.where(first, 0.0, x)], axis=0)

    splits = [_split_bf16(lw) for _, lw, _, _, _, _ in probs]
    cums = [_dot(tri, h) + _dot(tri, l) for h, l in splits]

    feats = []
    for (r, lw, k, v, na, b), cum in zip(probs, cums):
        total = cum[c - 1:c, :]
        p_inv = jnp.exp(-cum)
        p_end = jnp.exp(total - cum)
        at = stack(na * jnp.exp(cum - lw))
        rt = stack(r * jnp.exp(cum))
        lhs = jnp.concatenate([at, rt], axis=0)
        rhs = jnp.concatenate([stack(b * p_inv), stack(k * p_inv)], axis=0)
        feats.append(dict(at=at, rt=rt, lhs=lhs, rhs=rhs, bh=stack(b * p_end), kh=stack(k * p_end),
                          vs=stack(v), decay=jnp.exp(total)))

    gs = [_mm(f["lhs"], f["rhs"], RW_MODE_SCORE, "nt") for f in feats]
    l_ab = [jnp.where(strict, g[:n, :n], 0.0) for g in gs]
    l_ak = [jnp.where(strict, g[:n, n:], 0.0) for g in gs]
    m_rb = [jnp.where(incl, g[n:, :n], 0.0) for g in gs]
    m_rk = [jnp.where(incl, g[n:, n:], 0.0) for g in gs]

    xs = l_ab
    tinv = [jnp.where(eye, 1.0, 0.0) + x for x in xs]
    for _ in range(int(math.log2(c)) - 1):
        xs = [_mm(x, x, RW_MODE_INV) for x in xs]
        tinv = [t + _mm(t, x, RW_MODE_INV) for t, x in zip(tinv, xs)]

    lv = [_mm(l, f["vs"], RW_MODE_MID) for l, f in zip(l_ak, feats)]
    wu = [_mm(t, jnp.concatenate([f["at"], y], axis=1), RW_MODE_MID)
          for t, f, y in zip(tinv, feats, lv)]
    ro = [_mm(m, w, RW_MODE_MID) for m, w in zip(m_rb, wu)]
    rv = [_mm(m, f["vs"], RW_MODE_MID) for m, f in zip(m_rk, feats)]
    ab = [_mm(w, f["bh"], RW_MODE_STATE, "tn") for w, f in zip(wu, feats)]
    vk = [_mm(f["vs"], f["kh"], RW_MODE_STATE, "tn") for f in feats]

    outs, new_states = [], []
    for i, f in enumerate(feats):
        rh = f["rt"] + ro[i][:, :LANES]
        o0 = ro[i][:, LANES:] + rv[i]
        a_mat = jnp.where(eye, f["decay"], 0.0) + ab[i][:LANES]
        b_mat = ab[i][LANES:] + vk[i]
        outs.append(_mm(rh[:c] + rh[c:], states[i], RW_MODE_STATE, "nt") + o0[:c] + o0[c:])
        new_states.append(_mm(states[i], a_mat, RW_MODE_STATE) + b_mat)
    return outs, new_states


def _rw_scan_kernel(r_ref, lw_ref, k_ref, v_ref, na_ref, b_ref, g_ref, rk_ref, gng_ref, gnb_ref,
                    o_ref, s_ref):
    @pl.when(pl.program_id(0) == 0)
    def _():
        s_ref[...] = jnp.zeros_like(s_ref)

    lane = lax.broadcasted_iota(jnp.int32, (1, LANES), 1)
    first = lane < RW_HEAD_DIM
    batch = r_ref.shape[0]
    pairs = RW_WIDTH // LANES
    where = [(bi, slice(pr * LANES, (pr + 1) * LANES)) for bi in range(batch) for pr in range(pairs)]

    def head_sum(y):
        s_first = jnp.sum(jnp.where(first, y, 0.0), axis=-1, keepdims=True)
        s_second = jnp.sum(jnp.where(first, 0.0, y), axis=-1, keepdims=True)
        return jnp.where(first, s_first, s_second)

    probs = [(r_ref[bi, :, sl], lw_ref[bi, :, sl], k_ref[bi, :, sl], v_ref[bi, :, sl], na_ref[bi, :, sl],
              b_ref[bi, :, sl]) for bi, sl in where]
    outs, new_states = _rw_chunks(probs, [s_ref[i] for i in range(len(where))])
    for i, (bi, sl) in enumerate(where):
        s_ref[i] = new_states[i]
        r, _, k, v, _, _ = probs[i]
        o = outs[i]
        mu = head_sum(o) / RW_HEAD_DIM
        d = o - mu
        var = head_sum(d * d) / RW_HEAD_DIM
        normed = d * lax.rsqrt(var + RW_GN_EPS) * gng_ref[:, sl] + gnb_ref[:, sl]
        bonus = head_sum(r * k * rk_ref[:, sl]) * v
        o_ref[bi, :, sl] = ((normed + bonus) * _silu(g_ref[bi, :, sl])).astype(o_ref.dtype)


def _rw_scan(r, lw, k, v, na, b, p, r_k, gn_g, gn_b, batch, seq):
    c = RW_CHUNK
    as3d = lambda t: t.reshape(batch, seq, t.shape[-1])
    tok = pl.BlockSpec((batch, c, RW_WIDTH), lambda ci: (0, ci, 0))
    vec = pl.BlockSpec((1, RW_WIDTH), lambda ci: (0, 0))
    gate = pl.BlockSpec((batch, c, RW_WIDTH), lambda ci: (0, ci, SEC_RG // RW_WIDTH))
    out = pl.pallas_call(
        _rw_scan_kernel,
        out_shape=jax.ShapeDtypeStruct((batch, seq, RW_WIDTH), BF16),
        grid=(seq // c,),
        in_specs=[tok] * 6 + [gate, vec, vec, vec],
        out_specs=tok,
        scratch_shapes=[pltpu.VMEM((batch * RW_WIDTH // LANES, LANES, LANES), F32)],
        compiler_params=_params(("arbitrary",)),
        name="rw_scan",
    )(*[as3d(t) for t in (r, lw, k, v, na, b, p)], r_k, gn_g, gn_b)
    return out.reshape(batch * seq, RW_WIDTH)


def _rw_branch(p, mu, w0, w_up, a0, a_up, k_k, k_a, r_k, gn_g, gn_b, batch, seq):
    row = lambda t: t[None, :]
    head_ones = jnp.kron(jnp.eye(RW_HEADS, dtype=F32), jnp.ones((RW_HEAD_DIM, RW_HEAD_DIM), F32)).astype(BF16)
    mu_small = jnp.concatenate([jnp.zeros((LANES,), F32), mu[3 * RW_WIDTH:]])
    zeros = jnp.zeros((DECAY_LORA, RW_WIDTH), F32)
    wlora = jnp.concatenate([jnp.concatenate([w_up, zeros], axis=1),
                             jnp.concatenate([zeros, a_up], axis=1)], axis=0).astype(BF16)
    rr, lw, rk, rv, na, rb = _rw_prep(p, row(mu[:3 * RW_WIDTH]), row(mu_small), wlora, row(w0), row(a0),
                                      row(k_k), row(k_a), head_ones, seq)
    return _rw_scan(rr, lw, rk, rv, na, rb, p, r_k.reshape(1, RW_WIDTH), row(gn_g), row(gn_b), batch, seq)


def _merge_out_kernel(x_ref, ysb_ref, ymla_ref, yrw_ref, wsb_ref, wmla_ref, wrw_ref, g1_ref, g2_ref, g3_ref,
                      wout_ref, o_ref):
    merged = _sigmoid(g1_ref[...]) * _dot(ysb_ref[...], wsb_ref[...])
    merged = merged + _sigmoid(g2_ref[...]) * _dot(ymla_ref[...], wmla_ref[...])
    merged = merged + _sigmoid(g3_ref[...]) * _dot(yrw_ref[...], wrw_ref[...])
    o_ref[...] = x_ref[...] + _dot(merged.astype(BF16), wout_ref[...])


def _merge_out(x2, y_sb, y_mla, y_rw, w_sb, w_mla, w_rw, p, w_out, tm=256):
    m = p.shape[0]
    rows = lambda width: pl.BlockSpec((tm, width), lambda i: (i, 0))
    resident = lambda depth: pl.BlockSpec((depth, D_MODEL), lambda i: (0, 0), pipeline_mode=pl.Buffered(1))
    gate = lambda br: pl.BlockSpec((tm, D_MODEL), lambda i: (i, SEC_GATE // D_MODEL + br))
    return pl.pallas_call(
        _merge_out_kernel,
        out_shape=jax.ShapeDtypeStruct((m, D_MODEL), F32),
        grid=(m // tm,),
        in_specs=[rows(D_MODEL), rows(SB_WIDTH), rows(MLA_WIDTH), rows(RW_WIDTH),
                  resident(SB_WIDTH), resident(MLA_WIDTH), resident(RW_WIDTH),
                  gate(0), gate(1), gate(2), resident(D_MODEL)],
        out_specs=rows(D_MODEL),
        compiler_params=_params(("parallel",)),
        name="merge_out",
    )(x2, y_sb, y_mla, y_rw, w_sb, w_mla, w_rw, p, p, p, w_out)


def _pair_rope_cols(t):
    lead = t.shape[:-2]
    t = t.reshape(lead + (MLA_HEADS // 2, 2, 2, HALF_ROPE))
    t = jnp.swapaxes(t, -3, -2)
    return t.reshape(lead + (MLA_HEADS // 2 * LANES,))


def _pair_rope_gain(g):
    g1, g2 = g[:HALF_ROPE], g[HALF_ROPE:]
    return jnp.concatenate([g1, g1, g2, g2])[None, :]


def _rope_tables(seq):
    freqs = ROPE_THETA ** (-jnp.arange(HALF_ROPE, dtype=F32) / HALF_ROPE)
    ang = jnp.arange(seq, dtype=F32)[:, None] * freqs[None, :]
    c, s = jnp.cos(ang), jnp.sin(ang)
    return jnp.concatenate([c, c, c, c], axis=1), jnp.concatenate([-s, -s, s, s], axis=1)


def _mla_branch(p, cos, sin, q_norm_g, kv_norm_g, w_uq, w_ukv, qn_g, kn_g, batch, seq):
    row = lambda t: t[None, :]
    uq = w_uq.reshape(Q_LORA, MLA_HEADS, MLA_QK)
    wuq = jnp.concatenate([uq[:, :, :MLA_NOPE].reshape(Q_LORA, -1), _pair_rope_cols(uq[:, :, MLA_NOPE:])],
                          axis=1).astype(BF16)
    ukv = w_ukv.reshape(KV_LORA, MLA_HEADS, MLA_NOPE + MLA_V)
    wukv = jnp.concatenate([ukv[:, :, :MLA_NOPE].reshape(KV_LORA, -1),
                            ukv[:, :, MLA_NOPE:].reshape(KV_LORA, -1)], axis=1).astype(BF16)
    q, k, v = _mla_prep(p, cos, sin, row(q_norm_g), row(kv_norm_g), wuq, wukv,
                        row(qn_g[:MLA_NOPE]), _pair_rope_gain(qn_g[MLA_NOPE:]),
                        row(kn_g[:MLA_NOPE]), _pair_rope_gain(kn_g[MLA_NOPE:]), seq)
    return _mla_attn(q, k, v, p, batch, seq)


def kernel(x, norm_g, w_in, mla_q_norm_g, mla_kv_norm_g, mla_w_uq, mla_w_ukv, mla_qn_g, mla_kn_g,
           rw_mu, rw_w0, rw_w_up, rw_a0, rw_a_up, rw_k_k, rw_k_a, rw_r_k, rw_gn_g, rw_gn_b,
           w_br_sb, w_br_mla, w_br_rw, w_out):
    batch, seq, _ = x.shape
    depth = w_in.shape[0]
    x2 = x.reshape(batch * seq, D_MODEL)
    cos, sin = _rope_tables(seq)
    w_in_bf16 = _w_in_relayout(w_in)

    for l in range(depth):
        p = _in_proj(x2, norm_g[l][None, :], w_in_bf16, l)
        y_sb = _sb_attn(p, batch, seq)
        y_mla = _mla_branch(p, cos, sin, mla_q_norm_g[l], mla_kv_norm_g[l], mla_w_uq[l], mla_w_ukv[l],
                            mla_qn_g[l], mla_kn_g[l], batch, seq)
        y_rw = _rw_branch(p, rw_mu[l], rw_w0[l], rw_w_up[l], rw_a0[l], rw_a_up[l], rw_k_k[l], rw_k_a[l],
                          rw_r_k[l], rw_gn_g[l], rw_gn_b[l], batch, seq)
        x2 = _merge_out(x2, y_sb, y_mla, y_rw, w_br_sb[l].astype(BF16), w_br_mla[l].astype(BF16),
                        w_br_rw[l].astype(BF16), p, w_out[l].astype(BF16))

    return x2.reshape(batch, seq, D_MODEL)
```

```python
import functools
import math

import jax
import jax.numpy as jnp
from jax import lax
from jax.experimental import pallas as pl
from jax.experimental.pallas import tpu as pltpu

F32 = jnp.float32
BF16 = jnp.bfloat16

D_MODEL = 2048
EPS = 1e-6
CHUNK = 64

SB_HEADS = 4
SB_HEAD_DIM = 128
SB_WIDTH = SB_HEADS * SB_HEAD_DIM

MLA_HEADS = 8
MLA_NOPE = 128
MLA_ROPE = 64
MLA_QK = MLA_NOPE + MLA_ROPE
MLA_V = 128
MLA_WIDTH = MLA_HEADS * MLA_V
Q_LORA = 512
KV_LORA = 512
ROPE_THETA = 10000.0
HALF_ROPE = MLA_ROPE // 2

RW_HEADS = 8
RW_HEAD_DIM = 64
RW_WIDTH = RW_HEADS * RW_HEAD_DIM
DECAY_LORA = 64
ICL_LORA = 64
RW_GN_EPS = 64e-5
RW_CHUNK = 64

LANES = 128
VMEM_LIMIT = 56 * 1024 * 1024

SEC_SB = 0
SEC_MLA = 2048
SEC_RG = 4096
SEC_RW = 4608
SEC_GATE = 6144
SEC_SMALL = 12288
N_PROJ = SEC_SMALL + 256

_ORIG_KR = 4 * SB_WIDTH + Q_LORA + KV_LORA
_ORIG_MG = _ORIG_KR + MLA_ROPE
_ORIG_RW = _ORIG_MG + MLA_WIDTH
_ORIG_LORA = _ORIG_RW + 3 * RW_WIDTH
_ORIG_RG = _ORIG_LORA + DECAY_LORA + ICL_LORA
_ORIG_GATE = _ORIG_RG + RW_WIDTH
W_IN_MOVES = (
    (0, 0, _ORIG_KR),
    (SEC_MLA + Q_LORA + KV_LORA, _ORIG_MG, MLA_WIDTH),
    (SEC_RG, _ORIG_RG, RW_WIDTH),
    (SEC_RW, _ORIG_RW, 3 * RW_WIDTH),
    (SEC_GATE, _ORIG_GATE, 3 * D_MODEL),
    (SEC_SMALL, _ORIG_KR, HALF_ROPE),
    (SEC_SMALL + HALF_ROPE, _ORIG_KR, HALF_ROPE),
    (SEC_SMALL + 2 * HALF_ROPE, _ORIG_KR + HALF_ROPE, HALF_ROPE),
    (SEC_SMALL + 3 * HALF_ROPE, _ORIG_KR + HALF_ROPE, HALF_ROPE),
    (SEC_SMALL + LANES, _ORIG_LORA, DECAY_LORA + ICL_LORA),
)

NEG_BIG = -1e30

SB_DEAD_LOG_WEIGHT = -104.0

ATTN_TILE = 256
MLA_TILE = 512
SB_GROUP = 4
MLA_GROUP = 4


def _dot(a, b, precision=None):
    return jnp.dot(a, b, preferred_element_type=F32, precision=precision)


def _dot_nt(a, b, precision=None):
    return lax.dot_general(a, b, (((1,), (1,)), ((), ())), preferred_element_type=F32,
                           precision=precision)


def _dot_tn(a, b, precision=None):
    return lax.dot_general(a, b, (((0,), (0,)), ((), ())), preferred_element_type=F32,
                           precision=precision)


def _split_bf16(x):
    hi = x.astype(BF16)
    lo = (x - hi.astype(F32)).astype(BF16)
    return hi, lo


def _softplus(z):
    return jnp.maximum(z, 0.0) + jnp.log(1.0 + jnp.exp(-jnp.abs(z)))


def _sigmoid(z):
    return 1.0 / (1.0 + jnp.exp(-z))


def _silu(z):
    return z * _sigmoid(z)


def _params(semantics):
    return pltpu.CompilerParams(dimension_semantics=semantics, vmem_limit_bytes=VMEM_LIMIT)


def _in_proj_kernel(x_ref, g_ref, w_ref, o_ref, h_ref):
    @pl.when(pl.program_id(1) == 0)
    def _():
        x = x_ref[...]
        rs = lax.rsqrt(jnp.mean(x * x, axis=-1, keepdims=True) + EPS)
        h_ref[...] = (x * rs * g_ref[...]).astype(BF16)

    o_ref[...] = _dot(h_ref[...], w_ref[...])


def _w_in_relayout_kernel(wt_ref, o_ref):
    for dst, src, width in W_IN_MOVES:
        for c in range(0, width, LANES):
            cw = min(LANES, width - c)
            o_ref[:, dst + c:dst + c + cw] = wt_ref[src + c:src + c + cw, :].T.astype(BF16)


def _w_in_relayout(w_in, tk=LANES):
    depth, _, n_in = w_in.shape
    return pl.pallas_call(
        _w_in_relayout_kernel,
        out_shape=jax.ShapeDtypeStruct((depth, D_MODEL, N_PROJ), BF16),
        grid=(depth, D_MODEL // tk),
        in_specs=[pl.BlockSpec((None, n_in, tk), lambda l, i: (l, 0, i))],
        out_specs=pl.BlockSpec((None, tk, N_PROJ), lambda l, i: (l, i, 0)),
        compiler_params=_params(("parallel", "parallel")),
        name="w_in_relayout",
    )(jnp.swapaxes(w_in, 1, 2))


def _in_proj(x2, g, w, layer, tm=1024, tn=1792):
    m = x2.shape[0]
    return pl.pallas_call(
        _in_proj_kernel,
        out_shape=jax.ShapeDtypeStruct((m, N_PROJ), F32),
        grid=(m // tm, N_PROJ // tn),
        in_specs=[
            pl.BlockSpec((tm, D_MODEL), lambda i, j: (i, 0), pipeline_mode=pl.Buffered(1)),
            pl.BlockSpec((1, D_MODEL), lambda i, j: (0, 0)),
            pl.BlockSpec((None, D_MODEL, tn), lambda i, j: (layer, 0, j)),
        ],
        out_specs=pl.BlockSpec((tm, tn), lambda i, j: (i, j)),
        scratch_shapes=[pltpu.VMEM((tm, D_MODEL), BF16)],
        compiler_params=_params(("parallel", "arbitrary")),
        name="in_proj",
    )(x2, g, w)


def _sb_attn_kernel(q_ref, k_ref, v_ref, g_ref, o_ref, kb_ref, vt_ref, *, t, heads):
    i = pl.program_id(2)
    nblk = k_ref.shape[0] // t
    scale = SB_HEAD_DIM ** -0.5
    key = lax.broadcasted_iota(jnp.int32, (t, t), 0)
    qry = lax.broadcasted_iota(jnp.int32, (t, t), 1)
    later = (qry > key).astype(BF16)
    causal = key < qry
    head = lambda h: slice(h * LANES, (h + 1) * LANES)
    qs = [q_ref[:, head(h)].astype(BF16) for h in range(heads)]

    @pl.when(i == 0)
    def _():
        for h in range(heads):
            for blk in range(nblk):
                rows = slice(blk * t, (blk + 1) * t)
                kb_ref[h, blk] = k_ref[rows, head(h)].astype(BF16)
                vt_ref[h, blk] = v_ref[rows, head(h)].T.astype(BF16)

    def step(j, carry, diagonal):
        zs = [_dot_nt(kb_ref[h, j], qs[h]) for h in range(heads)]
        stage = []
        for h in range(heads):
            z = zs[h] * scale
            sp = _softplus(z)
            log_fail = jnp.where(causal, -sp, 0.0) if diagonal else -sp
            hi, lo = _split_bf16(log_fail)
            stage.append((z - sp, log_fail, hi, lo))
        sums = [_dot(later, hi) + _dot(later, lo) for _, _, hi, lo in stage]
        ws = []
        for h in range(heads):
            w = jnp.exp(stage[h][0] + sums[h] + carry[h][0])
            if diagonal:
                w = jnp.where(causal, w, 0.0)
            ws.append(w.astype(BF16))
        out = []
        for h in range(heads):
            run, acc = carry[h]
            out.append((run + jnp.sum(stage[h][1], axis=0, keepdims=True), acc + _dot(vt_ref[h, j], ws[h])))
        return tuple(out)

    init = tuple((jnp.zeros((1, t), F32), jnp.zeros((SB_HEAD_DIM, t), F32)) for _ in range(heads))
    carry = step(i, init, True)

    def alive(c):
        return functools.reduce(jnp.maximum, [jnp.max(c[h][0]) for h in range(heads)]) >= SB_DEAD_LOG_WEIGHT

    def body(state):
        n, _, c = state
        c = step(i - 1 - n, c, False)
        return n + 1, alive(c), c

    _, _, carry = lax.while_loop(lambda s: jnp.logical_and(s[0] < i, s[1]), body, (0, alive(carry), carry))
    for h in range(heads):
        o_ref[:, head(h)] = (carry[h][1].T * _silu(g_ref[:, head(h)])).astype(o_ref.dtype)


def _sb_attn(p, batch, seq, t=ATTN_TILE, heads=SB_GROUP):
    nq = seq // t
    w = heads * LANES
    sec = SB_WIDTH // w
    tile = lambda s: pl.BlockSpec((t, w), lambda b, h, i: (b * nq + i, s * sec + h))
    full = lambda s: pl.BlockSpec((seq, w), lambda b, h, i: (b, s * sec + h))
    return pl.pallas_call(
        functools.partial(_sb_attn_kernel, t=t, heads=heads),
        out_shape=jax.ShapeDtypeStruct((batch * seq, SB_WIDTH), BF16),
        grid=(batch, SB_HEADS // heads, nq),
        in_specs=[tile(0), full(1), full(2), tile(3)],
        out_specs=pl.BlockSpec((t, w), lambda b, h, i: (b * nq + i, h)),
        scratch_shapes=[pltpu.VMEM((heads, nq, t, SB_HEAD_DIM), BF16),
                        pltpu.VMEM((heads, nq, SB_HEAD_DIM, t), BF16)],
        compiler_params=_params(("parallel", "parallel", "arbitrary")),
        name="sb_attn",
    )(p, p, p, p)


def _mla_prep_kernel(cq_ref, ckv_ref, sm_ref, cos_ref, sin_ref, gq_ref, gkv_ref, wuq_ref, wukv_ref,
                     gqn_ref, gqr_ref, gkn_ref, gkr_ref, q_ref, k_ref, v_ref):
    lane = lax.broadcasted_iota(jnp.int32, (1, LANES), 1)
    first = (lane % MLA_ROPE) < HALF_ROPE
    cos = cos_ref[...]
    sin = sin_ref[...]

    def latent_norm(c_ref, g_ref):
        c = c_ref[...]
        rs = lax.rsqrt(jnp.mean(c * c, axis=-1, keepdims=True) + EPS)
        return (c * rs * g_ref[...]).astype(BF16)

    def rotary(y):
        return y * cos + pltpu.roll(y, MLA_ROPE, axis=1) * sin

    def head_sums(sq):
        s_first = jnp.sum(jnp.where(first, sq, 0.0), axis=-1, keepdims=True)
        s_second = jnp.sum(jnp.where(first, 0.0, sq), axis=-1, keepdims=True)
        return s_first, s_second

    qfull = _dot(latent_norm(cq_ref, gq_ref), wuq_ref[...])
    kvfull = _dot(latent_norm(ckv_ref, gkv_ref), wukv_ref[...])
    v_ref[...] = kvfull[:, MLA_HEADS * MLA_NOPE:].astype(BF16)

    kr = sm_ref[...][:, :LANES]
    kr_ss, _ = head_sums(kr * kr)
    kr_rot = rotary(kr * gkr_ref[...])

    for pair in range(MLA_HEADS // 2):
        qr = qfull[:, MLA_HEADS * MLA_NOPE + pair * LANES:MLA_HEADS * MLA_NOPE + (pair + 1) * LANES]
        qr_ss = head_sums(qr * qr)
        q_rs, k_rs = [], []
        for e in range(2):
            h = 2 * pair + e
            qn = qfull[:, h * MLA_NOPE:(h + 1) * MLA_NOPE]
            kn = kvfull[:, h * MLA_NOPE:(h + 1) * MLA_NOPE]
            qs = lax.rsqrt((jnp.sum(qn * qn, axis=-1, keepdims=True) + qr_ss[e]) / MLA_QK + EPS)
            ks = lax.rsqrt((jnp.sum(kn * kn, axis=-1, keepdims=True) + kr_ss) / MLA_QK + EPS)
            q_rs.append(qs)
            k_rs.append(ks)
            q_ref[:, 2 * h * LANES:(2 * h + 1) * LANES] = (qn * qs * gqn_ref[...]).astype(BF16)
            k_ref[:, 2 * h * LANES:(2 * h + 1) * LANES] = (kn * ks * gkn_ref[...]).astype(BF16)
        q_rot = rotary(qr * jnp.where(first, q_rs[0], q_rs[1]) * gqr_ref[...])
        k_rot = (kr_rot * jnp.where(first, k_rs[0], k_rs[1])).astype(BF16)
        for e in range(2):
            h = 2 * pair + e
            own = first if e == 0 else jnp.logical_not(first)
            q_ref[:, (2 * h + 1) * LANES:(2 * h + 2) * LANES] = jnp.where(own, q_rot, 0.0).astype(BF16)
            k_ref[:, (2 * h + 1) * LANES:(2 * h + 2) * LANES] = k_rot


def _mla_prep(p, cos, sin, gq, gkv, wuq, wukv, gqn, gqr, gkn, gkr, seq, tm=512):
    m = p.shape[0]
    sblk = seq // tm
    row = lambda width, cb: pl.BlockSpec((tm, width), lambda i: (i, cb))
    const = lambda shape: pl.BlockSpec(shape, lambda i: (0, 0))
    pos = pl.BlockSpec((tm, LANES), lambda i: (i % sblk, 0))
    qk_width = MLA_HEADS * 2 * LANES
    return pl.pallas_call(
        _mla_prep_kernel,
        out_shape=(jax.ShapeDtypeStruct((m, qk_width), BF16),
                   jax.ShapeDtypeStruct((m, qk_width), BF16),
                   jax.ShapeDtypeStruct((m, MLA_WIDTH), BF16)),
        grid=(m // tm,),
        in_specs=[
            row(Q_LORA, SEC_MLA // Q_LORA),
            row(KV_LORA, SEC_MLA // KV_LORA + 1),
            row(256, SEC_SMALL // 256),
            pos, pos,
            const((1, Q_LORA)), const((1, KV_LORA)),
            const(wuq.shape), const(wukv.shape),
            const((1, LANES)), const((1, LANES)), const((1, LANES)), const((1, LANES)),
        ],
        out_specs=(row(qk_width, 0), row(qk_width, 0), row(MLA_WIDTH, 0)),
        compiler_params=_params(("parallel",)),
        name="mla_prep",
    )(p, p, p, cos, sin, gq, gkv, wuq, wukv, gqn, gqr, gkn, gkr)


def _mla_attn_kernel(q_ref, k_ref, v_ref, g_ref, o_ref, vt_ref, sa_ref, sb_ref, sd_ref, m_ref, l_ref, acc_ref,
                     *, t, heads):
    i = pl.program_id(2)
    nblk = v_ref.shape[0] // t
    exp2_scale = (MLA_QK ** -0.5) * math.log2(math.e)
    qw = 2 * LANES
    key_chunk = lax.broadcasted_iota(jnp.int32, (t, t), 0) // CHUNK
    qry_chunk = lax.broadcasted_iota(jnp.int32, (t, t), 1) // CHUNK
    visible = key_chunk <= qry_chunk
    qs = [q_ref[:, h * qw:(h + 1) * qw] for h in range(heads)]

    @pl.when(i == 0)
    def _():
        for h in range(heads):
            for blk in range(nblk):
                vb = v_ref[blk * t:(blk + 1) * t, h * LANES:(h + 1) * LANES]
                vt_ref[h, blk] = vb.astype(F32).T.astype(BF16)

    m_ref[...] = jnp.full(m_ref.shape, NEG_BIG, F32)
    l_ref[...] = jnp.zeros(l_ref.shape, F32)
    acc_ref[...] = jnp.zeros(acc_ref.shape, F32)

    def issue_scores(j, buf):
        start = pl.multiple_of(j * t, t)
        for h in range(heads):
            buf[h] = _dot_nt(k_ref[pl.ds(start, t), h * qw:(h + 1) * qw], qs[h])

    def absorb(buf, j, diagonal):
        soft = []
        for h in range(heads):
            s = jnp.where(visible, buf[h], NEG_BIG) if diagonal else buf[h]
            m = m_ref[h]
            m_new = jnp.maximum(m, jnp.max(s, axis=0, keepdims=True))
            alpha = jnp.exp2((m - m_new) * exp2_scale)
            pr = jnp.exp2((s - m_new) * exp2_scale)
            m_ref[h] = m_new
            l_ref[h] = alpha * l_ref[h] + jnp.sum(pr, axis=0, keepdims=True)
            soft.append((alpha, pr.astype(BF16)))
        for h in range(heads):
            alpha, pr = soft[h]
            acc_ref[h] = alpha * acc_ref[h] + _dot(vt_ref[h, j], pr)

    issue_scores(i, sd_ref)
    issue_scores(0, sa_ref)
    last = jnp.maximum(i - 1, 0)

    def pair(n, _):
        issue_scores(2 * n + 1, sb_ref)
        absorb(sa_ref, 2 * n, False)
        issue_scores(jnp.minimum(2 * n + 2, last), sa_ref)
        absorb(sb_ref, 2 * n + 1, False)
        return 0

    lax.fori_loop(0, i // 2, pair, 0)

    @pl.when(i % 2 == 1)
    def _():
        absorb(sa_ref, i - 1, False)

    absorb(sd_ref, i, True)
    for h in range(heads):
        hs = slice(h * LANES, (h + 1) * LANES)
        o_ref[:, hs] = ((acc_ref[h] / l_ref[h]).T * _silu(g_ref[:, hs])).astype(o_ref.dtype)


def _mla_attn(q, k, v, p, batch, seq, t=MLA_TILE, heads=MLA_GROUP):
    nq = seq // t
    qw = heads * 2 * LANES
    vw = heads * LANES
    gate0 = (SEC_MLA + Q_LORA + KV_LORA) // vw
    return pl.pallas_call(
        functools.partial(_mla_attn_kernel, t=t, heads=heads),
        out_shape=jax.ShapeDtypeStruct((batch * seq, MLA_WIDTH), BF16),
        grid=(batch, MLA_HEADS // heads, nq),
        in_specs=[
            pl.BlockSpec((t, qw), lambda b, h, i: (b * nq + i, h)),
            pl.BlockSpec((seq, qw), lambda b, h, i: (b, h)),
            pl.BlockSpec((seq, vw), lambda b, h, i: (b, h)),
            pl.BlockSpec((t, vw), lambda b, h, i: (b * nq + i, gate0 + h)),
        ],
        out_specs=pl.BlockSpec((t, vw), lambda b, h, i: (b * nq + i, h)),
        scratch_shapes=[pltpu.VMEM((heads, nq, MLA_V, t), BF16)]
        + [pltpu.VMEM((heads, t, t), F32)] * 3
        + [pltpu.VMEM((heads, 1, t), F32)] * 2
        + [pltpu.VMEM((heads, MLA_V, t), F32)],
        compiler_params=_params(("parallel", "parallel", "arbitrary")),
        name="mla_attn",
    )(q, k, v, p)


def _rw_prep_kernel(cur_ref, prev_ref, smc_ref, smp_ref, mu_ref, mus_ref, wlora_ref, w0_ref, a0_ref,
                    kk_ref, ka_ref, ones_ref, r_ref, lw_ref, k_ref, v_ref, na_ref, b_ref,
                    *, tm, seq):
    i = pl.program_id(0)
    at_start = (i * tm) % seq == 0
    row = lax.broadcasted_iota(jnp.int32, (tm, 1), 0)

    def shifted(c_ref, p_ref, mu):
        cur = c_ref[...]
        last = jnp.where(at_start, 0.0, p_ref[...][7:8, :])
        prev = jnp.where(row == 0, last, pltpu.roll(cur, 1, axis=0))
        return cur + mu * (prev - cur)

    main = shifted(cur_ref, prev_ref, mu_ref[...])
    small = shifted(smc_ref, smp_ref, mus_ref[...])[:, LANES:]
    r = main[:, :RW_WIDTH]
    k = main[:, RW_WIDTH:2 * RW_WIDTH]
    v = main[:, 2 * RW_WIDTH:]

    lane = lax.broadcasted_iota(jnp.int32, (1, LANES), 1)
    lora_in = jnp.where(lane < DECAY_LORA, jnp.tanh(small), small).astype(BF16)
    lora = _dot(lora_in, wlora_ref[...])
    w_log = -_softplus(-(w0_ref[...] + lora[:, :RW_WIDTH])) - 0.5
    a = _sigmoid(a0_ref[...] + lora[:, RW_WIDTH:])

    kk = k * kk_ref[...]
    hi, lo = _split_bf16(kk * kk)
    ss = _dot(hi, ones_ref[...]) + _dot(lo, ones_ref[...])
    kk = kk / jnp.maximum(jnp.sqrt(ss), 1e-12)

    r_ref[...] = r
    lw_ref[...] = -jnp.exp(w_log)
    k_ref[...] = k * (1.0 + (a - 1.0) * ka_ref[...])
    v_ref[...] = v
    na_ref[...] = -kk
    b_ref[...] = kk * a


def _rw_prep(p, mu, mus, wlora, w0, a0, k_k, k_a, ones_bd, seq, tm=512):
    m = p.shape[0]
    vec = lambda width: pl.BlockSpec((1, width), lambda i: (0, 0))
    out = jax.ShapeDtypeStruct((m, RW_WIDTH), F32)
    main_w = 3 * RW_WIDTH
    prev_blk = lambda i: (jnp.maximum(i * (tm // 8) - 1, 0))
    return pl.pallas_call(
        functools.partial(_rw_prep_kernel, tm=tm, seq=seq),
        out_shape=(out,) * 6,
        grid=(m // tm,),
        in_specs=[
            pl.BlockSpec((tm, main_w), lambda i: (i, SEC_RW // main_w)),
            pl.BlockSpec((8, main_w), lambda i: (prev_blk(i), SEC_RW // main_w)),
            pl.BlockSpec((tm, 256), lambda i: (i, SEC_SMALL // 256)),
            pl.BlockSpec((8, 256), lambda i: (prev_blk(i), SEC_SMALL // 256)),
            vec(main_w), vec(256),
            pl.BlockSpec(wlora.shape, lambda i: (0, 0)),
            vec(RW_WIDTH), vec(RW_WIDTH), vec(RW_WIDTH), vec(RW_WIDTH),
            pl.BlockSpec((RW_WIDTH, RW_WIDTH), lambda i: (0, 0)),
        ],
        out_specs=(pl.BlockSpec((tm, RW_WIDTH), lambda i: (i, 0)),) * 6,
        compiler_params=_params(("parallel",)),
        name="rw_prep",
    )(p, p, p, p, mu, mus, wlora, w0, a0, k_k, k_a, ones_bd)


_HI = lax.Precision.HIGHEST

RW_MODE_SCORE = "b1"
RW_MODE_INV = "b1"
RW_MODE_MID = "b1"
RW_MODE_STATE = "b1"


def _mm(a, b, mode, form="nn"):
    f = {"nn": _dot, "nt": _dot_nt, "tn": _dot_tn}[form]
    if mode == "hi":
        return f(a, b, _HI)
    if mode == "b1":
        return f(a.astype(BF16), b.astype(BF16))
    a_hi, a_lo = _split_bf16(a)
    b_hi, b_lo = _split_bf16(b)
    return f(a_hi, b_hi) + (f(a_hi, b_lo) + f(a_lo, b_hi))


def _rw_chunks(probs, states):
    c = RW_CHUNK
    n = 2 * c
    lane = lax.broadcasted_iota(jnp.int32, (1, LANES), 1)
    first = lane < RW_HEAD_DIM
    ti = lax.broadcasted_iota(jnp.int32, (c, c), 0)
    si = lax.broadcasted_iota(jnp.int32, (c, c), 1)
    tri = (si <= ti).astype(BF16)
    ri = lax.broadcasted_iota(jnp.int32, (n, n), 0)
    ci = lax.broadcasted_iota(jnp.int32, (n, n), 1)
    strict = (ci % c) < (ri % c)
    incl = (ci % c) <= (ri % c)
    eye = ri == ci

    def stack(x):
        return jnp.concatenate([jnp.where(first, x, 0.0), jnp.where(first, 0.0, x)], axis=0)

    splits = [_split_bf16(lw) for _, lw, _, _, _, _ in probs]
    cums = [_dot(tri, h) + _dot(tri, l) for h, l in splits]

    feats = []
    for (r, lw, k, v, na, b), cum in zip(probs, cums):
        total = cum[c - 1:c, :]
        p_inv = jnp.exp(-cum)
        p_end = jnp.exp(total - cum)
        at = stack(na * jnp.exp(cum - lw))
        rt = stack(r * jnp.exp(cum))
        lhs = jnp.concatenate([at, rt], axis=0)
        rhs = jnp.concatenate([stack(b * p_inv), stack(k * p_inv)], axis=0)
        feats.append(dict(at=at, rt=rt, lhs=lhs, rhs=rhs, bh=stack(b * p_end), kh=stack(k * p_end),
                          vs=stack(v), decay=jnp.exp(total)))

    gs = [_mm(f["lhs"], f["rhs"], RW_MODE_SCORE, "nt") for f in feats]
    l_ab = [jnp.where(strict, g[:n, :n], 0.0) for g in gs]
    l_ak = [jnp.where(strict, g[:n, n:], 0.0) for g in gs]
    m_rb = [jnp.where(incl, g[n:, :n], 0.0) for g in gs]
    m_rk = [jnp.where(incl, g[n:, n:], 0.0) for g in gs]

    xs = l_ab
    tinv = [jnp.where(eye, 1.0, 0.0) + x for x in xs]
    for _ in range(int(math.log2(c)) - 1):
        xs = [_mm(x, x, RW_MODE_INV) for x in xs]
        tinv = [t + _mm(t, x, RW_MODE_INV) for t, x in zip(tinv, xs)]

    lv = [_mm(l, f["vs"], RW_MODE_MID) for l, f in zip(l_ak, feats)]
    wu = [_mm(t, jnp.concatenate([f["at"], y], axis=1), RW_MODE_MID)
          for t, f, y in zip(tinv, feats, lv)]
    ro = [_mm(m, w, RW_MODE_MID) for m, w in zip(m_rb, wu)]
    rv = [_mm(m, f["vs"], RW_MODE_MID) for m, f in zip(m_rk, feats)]
    ab = [_mm(w, f["bh"], RW_MODE_STATE, "tn") for w, f in zip(wu, feats)]
    vk = [_mm(f["vs"], f["kh"], RW_MODE_STATE, "tn") for f in feats]

    outs, new_states = [], []
    for i, f in enumerate(feats):
        rh = f["rt"] + ro[i][:, :LANES]
        o0 = ro[i][:, LANES:] + rv[i]
        a_mat = jnp.where(eye, f["decay"], 0.0) + ab[i][:LANES]
        b_mat = ab[i][LANES:] + vk[i]
        outs.append(_mm(rh[:c] + rh[c:], states[i], RW_MODE_STATE, "nt") + o0[:c] + o0[c:])
        new_states.append(_mm(states[i], a_mat, RW_MODE_STATE) + b_mat)
    return outs, new_states


def _rw_scan_kernel(r_ref, lw_ref, k_ref, v_ref, na_ref, b_ref, g_ref, rk_ref, gng_ref, gnb_ref,
                    o_ref, s_ref):
    @pl.when(pl.program_id(0) == 0)
    def _():
        s_ref[...] = jnp.zeros_like(s_ref)

    lane = lax.broadcasted_iota(jnp.int32, (1, LANES), 1)
    first = lane < RW_HEAD_DIM
    batch = r_ref.shape[0]
    pairs = RW_WIDTH // LANES
    where = [(bi, slice(pr * LANES, (pr + 1) * LANES)) for bi in range(batch) for pr in range(pairs)]

    def head_sum(y):
        s_first = jnp.sum(jnp.where(first, y, 0.0), axis=-1, keepdims=True)
        s_second = jnp.sum(jnp.where(first, 0.0, y), axis=-1, keepdims=True)
        return jnp.where(first, s_first, s_second)

    probs = [(r_ref[bi, :, sl], lw_ref[bi, :, sl], k_ref[bi, :, sl], v_ref[bi, :, sl], na_ref[bi, :, sl],
              b_ref[bi, :, sl]) for bi, sl in where]
    outs, new_states = _rw_chunks(probs, [s_ref[i] for i in range(len(where))])
    for i, (bi, sl) in enumerate(where):
        s_ref[i] = new_states[i]
        r, _, k, v, _, _ = probs[i]
        o = outs[i]
        mu = head_sum(o) / RW_HEAD_DIM
        d = o - mu
        var = head_sum(d * d) / RW_HEAD_DIM
        normed = d * lax.rsqrt(var + RW_GN_EPS) * gng_ref[:, sl] + gnb_ref[:, sl]
        bonus = head_sum(r * k * rk_ref[:, sl]) * v
        o_ref[bi, :, sl] = ((normed + bonus) * _silu(g_ref[bi, :, sl])).astype(o_ref.dtype)


def _rw_scan(r, lw, k, v, na, b, p, r_k, gn_g, gn_b, batch, seq):
    c = RW_CHUNK
    as3d = lambda t: t.reshape(batch, seq, t.shape[-1])
    tok = pl.BlockSpec((batch, c, RW_WIDTH), lambda ci: (0, ci, 0))
    vec = pl.BlockSpec((1, RW_WIDTH), lambda ci: (0, 0))
    gate = pl.BlockSpec((batch, c, RW_WIDTH), lambda ci: (0, ci, SEC_RG // RW_WIDTH))
    out = pl.pallas_call(
        _rw_scan_kernel,
        out_shape=jax.ShapeDtypeStruct((batch, seq, RW_WIDTH), BF16),
        grid=(seq // c,),
        in_specs=[tok] * 6 + [gate, vec, vec, vec],
        out_specs=tok,
        scratch_shapes=[pltpu.VMEM((batch * RW_WIDTH // LANES, LANES, LANES), F32)],
        compiler_params=_params(("arbitrary",)),
        name="rw_scan",
    )(*[as3d(t) for t in (r, lw, k, v, na, b, p)], r_k, gn_g, gn_b)
    return out.reshape(batch * seq, RW_WIDTH)


def _rw_branch(p, mu, w0, w_up, a0, a_up, k_k, k_a, r_k, gn_g, gn_b, batch, seq):
    row = lambda t: t[None, :]
    head_ones = jnp.kron(jnp.eye(RW_HEADS, dtype=F32), jnp.ones((RW_HEAD_DIM, RW_HEAD_DIM), F32)).astype(BF16)
    mu_small = jnp.concatenate([jnp.zeros((LANES,), F32), mu[3 * RW_WIDTH:]])
    zeros = jnp.zeros((DECAY_LORA, RW_WIDTH), F32)
    wlora = jnp.concatenate([jnp.concatenate([w_up, zeros], axis=1),
                             jnp.concatenate([zeros, a_up], axis=1)], axis=0).astype(BF16)
    rr, lw, rk, rv, na, rb = _rw_prep(p, row(mu[:3 * RW_WIDTH]), row(mu_small), wlora, row(w0), row(a0),
                                      row(k_k), row(k_a), head_ones, seq)
    return _rw_scan(rr, lw, rk, rv, na, rb, p, r_k.reshape(1, RW_WIDTH), row(gn_g), row(gn_b), batch, seq)


def _merge_out_kernel(x_ref, ysb_ref, ymla_ref, yrw_ref, wsb_ref, wmla_ref, wrw_ref, g1_ref, g2_ref, g3_ref,
                      wout_ref, o_ref):
    merged = _sigmoid(g1_ref[...]) * _dot(ysb_ref[...], wsb_ref[...])
    merged = merged + _sigmoid(g2_ref[...]) * _dot(ymla_ref[...], wmla_ref[...])
    merged = merged + _sigmoid(g3_ref[...]) * _dot(yrw_ref[...], wrw_ref[...])
    o_ref[...] = x_ref[...] + _dot(merged.astype(BF16), wout_ref[...])


def _merge_out(x2, y_sb, y_mla, y_rw, w_sb, w_mla, w_rw, p, w_out, tm=256):
    m = p.shape[0]
    rows = lambda width: pl.BlockSpec((tm, width), lambda i: (i, 0))
    resident = lambda depth: pl.BlockSpec((depth, D_MODEL), lambda i: (0, 0), pipeline_mode=pl.Buffered(1))
    gate = lambda br: pl.BlockSpec((tm, D_MODEL), lambda i: (i, SEC_GATE // D_MODEL + br))
    return pl.pallas_call(
        _merge_out_kernel,
        out_shape=jax.ShapeDtypeStruct((m, D_MODEL), F32),
        grid=(m // tm,),
        in_specs=[rows(D_MODEL), rows(SB_WIDTH), rows(MLA_WIDTH), rows(RW_WIDTH),
                  resident(SB_WIDTH), resident(MLA_WIDTH), resident(RW_WIDTH),
                  gate(0), gate(1), gate(2), resident(D_MODEL)],
        out_specs=rows(D_MODEL),
        compiler_params=_params(("parallel",)),
        name="merge_out",
    )(x2, y_sb, y_mla, y_rw, w_sb, w_mla, w_rw, p, p, p, w_out)


def _pair_rope_cols(t):
    lead = t.shape[:-2]
    t = t.reshape(lead + (MLA_HEADS // 2, 2, 2, HALF_ROPE))
    t = jnp.swapaxes(t, -3, -2)
    return t.reshape(lead + (MLA_HEADS // 2 * LANES,))


def _pair_rope_gain(g):
    g1, g2 = g[:HALF_ROPE], g[HALF_ROPE:]
    return jnp.concatenate([g1, g1, g2, g2])[None, :]


def _rope_tables(seq):
    freqs = ROPE_THETA ** (-jnp.arange(HALF_ROPE, dtype=F32) / HALF_ROPE)
    ang = jnp.arange(seq, dtype=F32)[:, None] * freqs[None, :]
    c, s = jnp.cos(ang), jnp.sin(ang)
    return jnp.concatenate([c, c, c, c], axis=1), jnp.concatenate([-s, -s, s, s], axis=1)


def _mla_branch(p, cos, sin, q_norm_g, kv_norm_g, w_uq, w_ukv, qn_g, kn_g, batch, seq):
    row = lambda t: t[None, :]
    uq = w_uq.reshape(Q_LORA, MLA_HEADS, MLA_QK)
    wuq = jnp.concatenate([uq[:, :, :MLA_NOPE].reshape(Q_LORA, -1), _pair_rope_cols(uq[:, :, MLA_NOPE:])],
                          axis=1).astype(BF16)
    ukv = w_ukv.reshape(KV_LORA, MLA_HEADS, MLA_NOPE + MLA_V)
    wukv = jnp.concatenate([ukv[:, :, :MLA_NOPE].reshape(KV_LORA, -1),
                            ukv[:, :, MLA_NOPE:].reshape(KV_LORA, -1)], axis=1).astype(BF16)
    q, k, v = _mla_prep(p, cos, sin, row(q_norm_g), row(kv_norm_g), wuq, wukv,
                        row(qn_g[:MLA_NOPE]), _pair_rope_gain(qn_g[MLA_NOPE:]),
                        row(kn_g[:MLA_NOPE]), _pair_rope_gain(kn_g[MLA_NOPE:]), seq)
    return _mla_attn(q, k, v, p, batch, seq)


def kernel(x, norm_g, w_in, mla_q_norm_g, mla_kv_norm_g, mla_w_uq, mla_w_ukv, mla_qn_g, mla_kn_g,
           rw_mu, rw_w0, rw_w_up, rw_a0, rw_a_up, rw_k_k, rw_k_a, rw_r_k, rw_gn_g, rw_gn_b,
           w_br_sb, w_br_mla, w_br_rw, w_out):
    batch, seq, _ = x.shape
    depth = w_in.shape[0]
    x2 = x.reshape(batch * seq, D_MODEL)
    cos, sin = _rope_tables(seq)
    w_in_bf16 = _w_in_relayout(w_in)

    for l in range(depth):
        p = _in_proj(x2, norm_g[l][None, :], w_in_bf16, l)
        y_sb = _sb_attn(p, batch, seq)
        y_mla = _mla_branch(p, cos, sin, mla_q_norm_g[l], mla_kv_norm_g[l], mla_w_uq[l], mla_w_ukv[l],
                            mla_qn_g[l], mla_kn_g[l], batch, seq)
        y_rw = _rw_branch(p, rw_mu[l], rw_w0[l], rw_w_up[l], rw_a0[l], rw_a_up[l], rw_k_k[l], rw_k_a[l],
                          rw_r_k[l], rw_gn_g[l], rw_gn_b[l], batch, seq)
        x2 = _merge_out(x2, y_sb, y_mla, y_rw, w_br_sb[l].astype(BF16), w_br_mla[l].astype(BF16),
                        w_br_rw[l].astype(BF16), p, w_out[l].astype(BF16))

    return x2.reshape(batch, seq, D_MODEL)
```

```python
import functools
import math

import jax
import jax.numpy as jnp
from jax import lax
from jax.experimental import pallas as pl
from jax.experimental.pallas import tpu as pltpu

F32 = jnp.float32
BF16 = jnp.bfloat16

D_MODEL = 2048
EPS = 1e-6
CHUNK = 64

SB_HEADS = 4
SB_HEAD_DIM = 128
SB_WIDTH = SB_HEADS * SB_HEAD_DIM

MLA_HEADS = 8
MLA_NOPE = 128
MLA_ROPE = 64
MLA_QK = MLA_NOPE + MLA_ROPE
MLA_V = 128
MLA_WIDTH = MLA_HEADS * MLA_V
Q_LORA = 512
KV_LORA = 512
ROPE_THETA = 10000.0
HALF_ROPE = MLA_ROPE // 2

RW_HEADS = 8
RW_HEAD_DIM = 64
RW_WIDTH = RW_HEADS * RW_HEAD_DIM
DECAY_LORA = 64
ICL_LORA = 64
RW_GN_EPS = 64e-5
RW_CHUNK = 64

LANES = 128
VMEM_LIMIT = 56 * 1024 * 1024

SEC_SB = 0
SEC_MLA = 2048
SEC_RG = 4096
SEC_RW = 4608
SEC_GATE = 6144
SEC_SMALL = 12288
N_PROJ = SEC_SMALL + 256

_ORIG_KR = 4 * SB_WIDTH + Q_LORA + KV_LORA
_ORIG_MG = _ORIG_KR + MLA_ROPE
_ORIG_RW = _ORIG_MG + MLA_WIDTH
_ORIG_LORA = _ORIG_RW + 3 * RW_WIDTH
_ORIG_RG = _ORIG_LORA + DECAY_LORA + ICL_LORA
_ORIG_GATE = _ORIG_RG + RW_WIDTH
W_IN_MOVES = (
    (0, 0, _ORIG_KR),
    (SEC_MLA + Q_LORA + KV_LORA, _ORIG_MG, MLA_WIDTH),
    (SEC_RG, _ORIG_RG, RW_WIDTH),
    (SEC_RW, _ORIG_RW, 3 * RW_WIDTH),
    (SEC_GATE, _ORIG_GATE, 3 * D_MODEL),
    (SEC_SMALL, _ORIG_KR, HALF_ROPE),
    (SEC_SMALL + HALF_ROPE, _ORIG_KR, HALF_ROPE),
    (SEC_SMALL + 2 * HALF_ROPE, _ORIG_KR + HALF_ROPE, HALF_ROPE),
    (SEC_SMALL + 3 * HALF_ROPE, _ORIG_KR + HALF_ROPE, HALF_ROPE),
    (SEC_SMALL + LANES, _ORIG_LORA, DECAY_LORA + ICL_LORA),
)

NEG_BIG = -1e30

SB_DEAD_LOG_WEIGHT = -104.0

ATTN_TILE = 256
MLA_TILE = 512
MLA_EXP2_SCALE = (MLA_QK ** -0.5) * math.log2(math.e)
MLA_SUM_ROWS = 16
SB_GROUP = 4
MLA_GROUP = 4


def _dot(a, b, precision=None):
    return jnp.dot(a, b, preferred_element_type=F32, precision=precision)


def _dot_nt(a, b, precision=None):
    return lax.dot_general(a, b, (((1,), (1,)), ((), ())), preferred_element_type=F32,
                           precision=precision)


def _dot_tn(a, b, precision=None):
    return lax.dot_general(a, b, (((0,), (0,)), ((), ())), preferred_element_type=F32,
                           precision=precision)


def _split_bf16(x):
    hi = x.astype(BF16)
    lo = (x - hi.astype(F32)).astype(BF16)
    return hi, lo


def _softplus(z):
    return jnp.maximum(z, 0.0) + jnp.log(1.0 + jnp.exp(-jnp.abs(z)))


def _sigmoid(z):
    return 1.0 / (1.0 + jnp.exp(-z))


def _silu(z):
    return z * _sigmoid(z)


def _params(semantics):
    return pltpu.CompilerParams(dimension_semantics=semantics, vmem_limit_bytes=VMEM_LIMIT)


def _in_proj_kernel(x_ref, g_ref, w_ref, o_ref, h_ref):
    @pl.when(pl.program_id(1) == 0)
    def _():
        x = x_ref[...]
        rs = lax.rsqrt(jnp.mean(x * x, axis=-1, keepdims=True) + EPS)
        h_ref[...] = (x * rs * g_ref[...]).astype(BF16)

    o_ref[...] = _dot(h_ref[...], w_ref[...])


def _w_in_relayout_kernel(wt_ref, o_ref):
    for dst, src, width in W_IN_MOVES:
        for c in range(0, width, LANES):
            cw = min(LANES, width - c)
            o_ref[:, dst + c:dst + c + cw] = wt_ref[src + c:src + c + cw, :].T.astype(BF16)


def _w_in_relayout(w_in, tk=LANES):
    depth, _, n_in = w_in.shape
    return pl.pallas_call(
        _w_in_relayout_kernel,
        out_shape=jax.ShapeDtypeStruct((depth, D_MODEL, N_PROJ), BF16),
        grid=(depth, D_MODEL // tk),
        in_specs=[pl.BlockSpec((None, n_in, tk), lambda l, i: (l, 0, i))],
        out_specs=pl.BlockSpec((None, tk, N_PROJ), lambda l, i: (l, i, 0)),
        compiler_params=_params(("parallel", "parallel")),
        name="w_in_relayout",
    )(jnp.swapaxes(w_in, 1, 2))


def _in_proj(x2, g, w, layer, tm=1024, tn=1792):
    m = x2.shape[0]
    return pl.pallas_call(
        _in_proj_kernel,
        out_shape=jax.ShapeDtypeStruct((m, N_PROJ), F32),
        grid=(m // tm, N_PROJ // tn),
        in_specs=[
            pl.BlockSpec((tm, D_MODEL), lambda i, j: (i, 0), pipeline_mode=pl.Buffered(1)),
            pl.BlockSpec((1, D_MODEL), lambda i, j: (0, 0)),
            pl.BlockSpec((None, D_MODEL, tn), lambda i, j: (layer, 0, j)),
        ],
        out_specs=pl.BlockSpec((tm, tn), lambda i, j: (i, j)),
        scratch_shapes=[pltpu.VMEM((tm, D_MODEL), BF16)],
        compiler_params=_params(("parallel", "arbitrary")),
        name="in_proj",
    )(x2, g, w)


def _sb_attn_kernel(q_ref, k_ref, v_ref, g_ref, o_ref, kb_ref, vt_ref, *, t, heads):
    i = pl.program_id(2)
    nblk = k_ref.shape[0] // t
    scale = SB_HEAD_DIM ** -0.5
    key = lax.broadcasted_iota(jnp.int32, (t, t), 0)
    qry = lax.broadcasted_iota(jnp.int32, (t, t), 1)
    later = (qry > key).astype(BF16)
    causal = key < qry
    head = lambda h: slice(h * LANES, (h + 1) * LANES)
    qs = [q_ref[:, head(h)].astype(BF16) for h in range(heads)]

    @pl.when(i == 0)
    def _():
        for h in range(heads):
            for blk in range(nblk):
                rows = slice(blk * t, (blk + 1) * t)
                kb_ref[h, blk] = k_ref[rows, head(h)].astype(BF16)
                vt_ref[h, blk] = v_ref[rows, head(h)].T.astype(BF16)

    def step(j, carry, diagonal):
        zs = [_dot_nt(kb_ref[h, j], qs[h]) for h in range(heads)]
        stage = []
        for h in range(heads):
            z = zs[h] * scale
            sp = _softplus(z)
            log_fail = jnp.where(causal, -sp, 0.0) if diagonal else -sp
            hi, lo = _split_bf16(log_fail)
            stage.append((z - sp, log_fail, hi, lo))
        sums = [_dot(later, hi) + _dot(later, lo) for _, _, hi, lo in stage]
        ws = []
        for h in range(heads):
            w = jnp.exp(stage[h][0] + sums[h] + carry[h][0])
            if diagonal:
                w = jnp.where(causal, w, 0.0)
            ws.append(w.astype(BF16))
        out = []
        for h in range(heads):
            run, acc = carry[h]
            out.append((run + jnp.sum(stage[h][1], axis=0, keepdims=True), acc + _dot(vt_ref[h, j], ws[h])))
        return tuple(out)

    init = tuple((jnp.zeros((1, t), F32), jnp.zeros((SB_HEAD_DIM, t), F32)) for _ in range(heads))
    carry = step(i, init, True)

    def alive(c):
        return functools.reduce(jnp.maximum, [jnp.max(c[h][0]) for h in range(heads)]) >= SB_DEAD_LOG_WEIGHT

    def body(state):
        n, _, c = state
        c = step(i - 1 - n, c, False)
        return n + 1, alive(c), c

    _, _, carry = lax.while_loop(lambda s: jnp.logical_and(s[0] < i, s[1]), body, (0, alive(carry), carry))
    for h in range(heads):
        o_ref[:, head(h)] = (carry[h][1].T * _silu(g_ref[:, head(h)])).astype(o_ref.dtype)


def _sb_attn(p, batch, seq, t=ATTN_TILE, heads=SB_GROUP):
    nq = seq // t
    w = heads * LANES
    sec = SB_WIDTH // w
    tile = lambda s: pl.BlockSpec((t, w), lambda b, h, i: (b * nq + i, s * sec + h))
    full = lambda s: pl.BlockSpec((seq, w), lambda b, h, i: (b, s * sec + h))
    return pl.pallas_call(
        functools.partial(_sb_attn_kernel, t=t, heads=heads),
        out_shape=jax.ShapeDtypeStruct((batch * seq, SB_WIDTH), BF16),
        grid=(batch, SB_HEADS // heads, nq),
        in_specs=[tile(0), full(1), full(2), tile(3)],
        out_specs=pl.BlockSpec((t, w), lambda b, h, i: (b * nq + i, h)),
        scratch_shapes=[pltpu.VMEM((heads, nq, t, SB_HEAD_DIM), BF16),
                        pltpu.VMEM((heads, nq, SB_HEAD_DIM, t), BF16)],
        compiler_params=_params(("parallel", "parallel", "arbitrary")),
        name="sb_attn",
    )(p, p, p, p)


def _mla_prep_kernel(cq_ref, ckv_ref, sm_ref, cos_ref, sin_ref, gq_ref, gkv_ref, wuq_ref, wukv_ref,
                     gqn_ref, gqr_ref, gkn_ref, gkr_ref, q_ref, k_ref, v_ref):
    lane = lax.broadcasted_iota(jnp.int32, (1, LANES), 1)
    first = (lane % MLA_ROPE) < HALF_ROPE
    cos = cos_ref[...]
    sin = sin_ref[...]

    def latent_norm(c_ref, g_ref):
        c = c_ref[...]
        rs = lax.rsqrt(jnp.mean(c * c, axis=-1, keepdims=True) + EPS)
        return (c * rs * g_ref[...]).astype(BF16)

    def rotary(y):
        return y * cos + pltpu.roll(y, MLA_ROPE, axis=1) * sin

    def head_sums(sq):
        s_first = jnp.sum(jnp.where(first, sq, 0.0), axis=-1, keepdims=True)
        s_second = jnp.sum(jnp.where(first, 0.0, sq), axis=-1, keepdims=True)
        return s_first, s_second

    qfull = _dot(latent_norm(cq_ref, gq_ref), wuq_ref[...])
    kvfull = _dot(latent_norm(ckv_ref, gkv_ref), wukv_ref[...])
    v_ref[...] = kvfull[:, MLA_HEADS * MLA_NOPE:].astype(BF16)

    kr = sm_ref[...][:, :LANES]
    kr_ss, _ = head_sums(kr * kr)
    kr_rot = rotary(kr * gkr_ref[...])

    for pair in range(MLA_HEADS // 2):
        qr = qfull[:, MLA_HEADS * MLA_NOPE + pair * LANES:MLA_HEADS * MLA_NOPE + (pair + 1) * LANES]
        qr_ss = head_sums(qr * qr)
        q_rs, k_rs = [], []
        for e in range(2):
            h = 2 * pair + e
            qn = qfull[:, h * MLA_NOPE:(h + 1) * MLA_NOPE]
            kn = kvfull[:, h * MLA_NOPE:(h + 1) * MLA_NOPE]
            qs = lax.rsqrt((jnp.sum(qn * qn, axis=-1, keepdims=True) + qr_ss[e]) / MLA_QK + EPS)
            ks = lax.rsqrt((jnp.sum(kn * kn, axis=-1, keepdims=True) + kr_ss) / MLA_QK + EPS)
            q_rs.append(qs)
            k_rs.append(ks)
            q_ref[:, 2 * h * LANES:(2 * h + 1) * LANES] = (qn * qs * gqn_ref[...]).astype(BF16)
            k_ref[:, 2 * h * LANES:(2 * h + 1) * LANES] = (kn * ks * gkn_ref[...]).astype(BF16)
        q_rot = rotary(qr * jnp.where(first, q_rs[0], q_rs[1]) * gqr_ref[...])
        k_rot = (kr_rot * jnp.where(first, k_rs[0], k_rs[1])).astype(BF16)
        for e in range(2):
            h = 2 * pair + e
            own = first if e == 0 else jnp.logical_not(first)
            q_ref[:, (2 * h + 1) * LANES:(2 * h + 2) * LANES] = jnp.where(own, q_rot, 0.0).astype(BF16)
            k_ref[:, (2 * h + 1) * LANES:(2 * h + 2) * LANES] = k_rot


def _mla_prep(p, cos, sin, gq, gkv, wuq, wukv, gqn, gqr, gkn, gkr, seq, tm=512):
    m = p.shape[0]
    sblk = seq // tm
    row = lambda width, cb: pl.BlockSpec((tm, width), lambda i: (i, cb))
    const = lambda shape: pl.BlockSpec(shape, lambda i: (0, 0))
    pos = pl.BlockSpec((tm, LANES), lambda i: (i % sblk, 0))
    qk_width = MLA_HEADS * 2 * LANES
    return pl.pallas_call(
        _mla_prep_kernel,
        out_shape=(jax.ShapeDtypeStruct((m, qk_width), BF16),
                   jax.ShapeDtypeStruct((m, qk_width), BF16),
                   jax.ShapeDtypeStruct((m, MLA_WIDTH), BF16)),
        grid=(m // tm,),
        in_specs=[
            row(Q_LORA, SEC_MLA // Q_LORA),
            row(KV_LORA, SEC_MLA // KV_LORA + 1),
            row(256, SEC_SMALL // 256),
            pos, pos,
            const((1, Q_LORA)), const((1, KV_LORA)),
            const(wuq.shape), const(wukv.shape),
            const((1, LANES)), const((1, LANES)), const((1, LANES)), const((1, LANES)),
        ],
        out_specs=(row(qk_width, 0), row(qk_width, 0), row(MLA_WIDTH, 0)),
        compiler_params=_params(("parallel",)),
        name="mla_prep",
    )(p, p, p, cos, sin, gq, gkv, wuq, wukv, gqn, gqr, gkn, gkr)


def _mla_attn_kernel(q_ref, k_ref, v_ref, g_ref, o_ref, vt_ref, sa_ref, sb_ref, sd_ref, m_ref, acc_ref,
                     *, t, heads):
    i = pl.program_id(2)
    nblk = v_ref.shape[0] // t
    qw = 2 * LANES
    key_chunk = lax.broadcasted_iota(jnp.int32, (t, t), 0) // CHUNK
    qry_chunk = lax.broadcasted_iota(jnp.int32, (t, t), 1) // CHUNK
    visible = key_chunk <= qry_chunk
    qs = [q_ref[:, h * qw:(h + 1) * qw] for h in range(heads)]

    @pl.when(i == 0)
    def _():
        for h in range(heads):
            for blk in range(nblk):
                vb = v_ref[blk * t:(blk + 1) * t, h * LANES:(h + 1) * LANES]
                vt_ref[h, blk, :MLA_V] = vb.astype(F32).T.astype(BF16)
                vt_ref[h, blk, MLA_V:] = jnp.ones((MLA_SUM_ROWS, t), BF16)

    m_ref[...] = jnp.full(m_ref.shape, NEG_BIG, F32)
    acc_ref[...] = jnp.zeros(acc_ref.shape, F32)

    def issue_scores(j, buf):
        start = pl.multiple_of(j * t, t)
        for h in range(heads):
            buf[h] = _dot_nt(k_ref[pl.ds(start, t), h * qw:(h + 1) * qw], qs[h])

    def absorb(buf, j, diagonal):
        soft = []
        for h in range(heads):
            s = jnp.where(visible, buf[h], NEG_BIG) if diagonal else buf[h]
            m = m_ref[h]
            m_new = jnp.maximum(m, jnp.max(s, axis=0, keepdims=True))
            m_ref[h] = m_new
            soft.append((jnp.exp2(m - m_new), jnp.exp2(s - m_new).astype(BF16)))
        for h in range(heads):
            alpha, pr = soft[h]
            acc_ref[h] = alpha * acc_ref[h] + _dot(vt_ref[h, j], pr)

    issue_scores(i, sd_ref)
    issue_scores(0, sa_ref)
    last = jnp.maximum(i - 1, 0)

    def pair(n, _):
        issue_scores(2 * n + 1, sb_ref)
        absorb(sa_ref, 2 * n, False)
        issue_scores(jnp.minimum(2 * n + 2, last), sa_ref)
        absorb(sb_ref, 2 * n + 1, False)
        return 0

    lax.fori_loop(0, i // 2, pair, 0)

    @pl.when(i % 2 == 1)
    def _():
        absorb(sa_ref, i - 1, False)

    absorb(sd_ref, i, True)
    for h in range(heads):
        hs = slice(h * LANES, (h + 1) * LANES)
        acc = acc_ref[h]
        y = (acc[:MLA_V] / acc[MLA_V:MLA_V + 1]).T
        o_ref[:, hs] = (y * _silu(g_ref[:, hs])).astype(o_ref.dtype)


def _mla_attn(q, k, v, p, batch, seq, t=MLA_TILE, heads=MLA_GROUP):
    nq = seq // t
    qw = heads * 2 * LANES
    vw = heads * LANES
    gate0 = (SEC_MLA + Q_LORA + KV_LORA) // vw
    return pl.pallas_call(
        functools.partial(_mla_attn_kernel, t=t, heads=heads),
        out_shape=jax.ShapeDtypeStruct((batch * seq, MLA_WIDTH), BF16),
        grid=(batch, MLA_HEADS // heads, nq),
        in_specs=[
            pl.BlockSpec((t, qw), lambda b, h, i: (b * nq + i, h)),
            pl.BlockSpec((seq, qw), lambda b, h, i: (b, h)),
            pl.BlockSpec((seq, vw), lambda b, h, i: (b, h)),
            pl.BlockSpec((t, vw), lambda b, h, i: (b * nq + i, gate0 + h)),
        ],
        out_specs=pl.BlockSpec((t, vw), lambda b, h, i: (b * nq + i, h)),
        scratch_shapes=[pltpu.VMEM((heads, nq, MLA_V + MLA_SUM_ROWS, t), BF16)]
        + [pltpu.VMEM((heads, t, t), F32)] * 3
        + [pltpu.VMEM((heads, 1, t), F32)]
        + [pltpu.VMEM((heads, MLA_V + MLA_SUM_ROWS, t), F32)],
        compiler_params=_params(("parallel", "parallel", "arbitrary")),
        name="mla_attn",
    )(q, k, v, p)


def _rw_prep_kernel(cur_ref, prev_ref, smc_ref, smp_ref, mu_ref, mus_ref, wlora_ref, w0_ref, a0_ref,
                    kk_ref, ka_ref, ones_ref, r_ref, lw_ref, k_ref, v_ref, na_ref, b_ref,
                    *, tm, seq):
    i = pl.program_id(0)
    at_start = (i * tm) % seq == 0
    row = lax.broadcasted_iota(jnp.int32, (tm, 1), 0)

    def shifted(c_ref, p_ref, mu):
        cur = c_ref[...]
        last = jnp.where(at_start, 0.0, p_ref[...][7:8, :])
        prev = jnp.where(row == 0, last, pltpu.roll(cur, 1, axis=0))
        return cur + mu * (prev - cur)

    main = shifted(cur_ref, prev_ref, mu_ref[...])
    small = shifted(smc_ref, smp_ref, mus_ref[...])[:, LANES:]
    r = main[:, :RW_WIDTH]
    k = main[:, RW_WIDTH:2 * RW_WIDTH]
    v = main[:, 2 * RW_WIDTH:]

    lane = lax.broadcasted_iota(jnp.int32, (1, LANES), 1)
    lora_in = jnp.where(lane < DECAY_LORA, jnp.tanh(small), small).astype(BF16)
    lora = _dot(lora_in, wlora_ref[...])
    w_log = -_softplus(-(w0_ref[...] + lora[:, :RW_WIDTH])) - 0.5
    a = _sigmoid(a0_ref[...] + lora[:, RW_WIDTH:])

    kk = k * kk_ref[...]
    hi, lo = _split_bf16(kk * kk)
    ss = _dot(hi, ones_ref[...]) + _dot(lo, ones_ref[...])
    kk = kk / jnp.maximum(jnp.sqrt(ss), 1e-12)

    r_ref[...] = r
    lw_ref[...] = -jnp.exp(w_log)
    k_ref[...] = k * (1.0 + (a - 1.0) * ka_ref[...])
    v_ref[...] = v
    na_ref[...] = -kk
    b_ref[...] = kk * a


def _rw_prep(p, mu, mus, wlora, w0, a0, k_k, k_a, ones_bd, seq, tm=512):
    m = p.shape[0]
    vec = lambda width: pl.BlockSpec((1, width), lambda i: (0, 0))
    out = jax.ShapeDtypeStruct((m, RW_WIDTH), F32)
    main_w = 3 * RW_WIDTH
    prev_blk = lambda i: (jnp.maximum(i * (tm // 8) - 1, 0))
    return pl.pallas_call(
        functools.partial(_rw_prep_kernel, tm=tm, seq=seq),
        out_shape=(out,) * 6,
        grid=(m // tm,),
        in_specs=[
            pl.BlockSpec((tm, main_w), lambda i: (i, SEC_RW // main_w)),
            pl.BlockSpec((8, main_w), lambda i: (prev_blk(i), SEC_RW // main_w)),
            pl.BlockSpec((tm, 256), lambda i: (i, SEC_SMALL // 256)),
            pl.BlockSpec((8, 256), lambda i: (prev_blk(i), SEC_SMALL // 256)),
            vec(main_w), vec(256),
            pl.BlockSpec(wlora.shape, lambda i: (0, 0)),
            vec(RW_WIDTH), vec(RW_WIDTH), vec(RW_WIDTH), vec(RW_WIDTH),
            pl.BlockSpec((RW_WIDTH, RW_WIDTH), lambda i: (0, 0)),
        ],
        out_specs=(pl.BlockSpec((tm, RW_WIDTH), lambda i: (i, 0)),) * 6,
        compiler_params=_params(("parallel",)),
        name="rw_prep",
    )(p, p, p, p, mu, mus, wlora, w0, a0, k_k, k_a, ones_bd)


_HI = lax.Precision.HIGHEST

RW_MODE_SCORE = "b1"
RW_MODE_INV = "b1"
RW_MODE_MID = "b1"
RW_MODE_STATE = "b1"


def _mm(a, b, mode, form="nn"):
    f = {"nn": _dot, "nt": _dot_nt, "tn": _dot_tn}[form]
    if mode == "hi":
        return f(a, b, _HI)
    if mode == "b1":
        return f(a.astype(BF16), b.astype(BF16))
    a_hi, a_lo = _split_bf16(a)
    b_hi, b_lo = _split_bf16(b)
    return f(a_hi, b_hi) + (f(a_hi, b_lo) + f(a_lo, b_hi))


def _rw_chunks(probs, states):
    c = RW_CHUNK
    n = 2 * c
    lane = lax.broadcasted_iota(jnp.int32, (1, LANES), 1)
    first = lane < RW_HEAD_DIM
    ti = lax.broadcasted_iota(jnp.int32, (c, c), 0)
    si = lax.broadcasted_iota(jnp.int32, (c, c), 1)
    tri = (si <= ti).astype(BF16)
    ri = lax.broadcasted_iota(jnp.int32, (n, n), 0)
    ci = lax.broadcasted_iota(jnp.int32, (n, n), 1)
    strict = (ci % c) < (ri % c)
    incl = (ci % c) <= (ri % c)
    eye = ri == ci

    def stack(x):
        return jnp.concatenate([jnp.where(first, x, 0.0), jnp.where(first, 0.0, x)], axis=0)

    splits = [_split_bf16(lw) for _, lw, _, _, _, _ in probs]
    cums = [_dot(tri, h) + _dot(tri, l) for h, l in splits]

    feats = []
    for (r, lw, k, v, na, b), cum in zip(probs, cums):
        total = cum[c - 1:c, :]
        p_inv = jnp.exp(-cum)
        p_end = jnp.exp(total - cum)
        at = stack(na * jnp.exp(cum - lw))
        rt = stack(r * jnp.exp(cum))
        lhs = jnp.concatenate([at, rt], axis=0)
        rhs = jnp.concatenate([stack(b * p_inv), stack(k * p_inv)], axis=0)
        feats.append(dict(at=at, rt=rt, lhs=lhs, rhs=rhs, bh=stack(b * p_end), kh=stack(k * p_end),
                          vs=stack(v), decay=jnp.exp(total)))

    gs = [_mm(f["lhs"], f["rhs"], RW_MODE_SCORE, "nt") for f in feats]
    l_ab = [jnp.where(strict, g[:n, :n], 0.0) for g in gs]
    l_ak = [jnp.where(strict, g[:n, n:], 0.0) for g in gs]
    m_rb = [jnp.where(incl, g[n:, :n], 0.0) for g in gs]
    m_rk = [jnp.where(incl, g[n:, n:], 0.0) for g in gs]

    xs = l_ab
    tinv = [jnp.where(eye, 1.0, 0.0) + x for x in xs]
    for _ in range(int(math.log2(c)) - 1):
        xs = [_mm(x, x, RW_MODE_INV) for x in xs]
        tinv = [t + _mm(t, x, RW_MODE_INV) for t, x in zip(tinv, xs)]

    lv = [_mm(l, f["vs"], RW_MODE_MID) for l, f in zip(l_ak, feats)]
    wu = [_mm(t, jnp.concatenate([f["at"], y], axis=1), RW_MODE_MID)
          for t, f, y in zip(tinv, feats, lv)]
    ro = [_mm(m, w, RW_MODE_MID) for m, w in zip(m_rb, wu)]
    rv = [_mm(m, f["vs"], RW_MODE_MID) for m, f in zip(m_rk, feats)]
    ab = [_mm(w, f["bh"], RW_MODE_STATE, "tn") for w, f in zip(wu, feats)]
    vk = [_mm(f["vs"], f["kh"], RW_MODE_STATE, "tn") for f in feats]

    outs, new_states = [], []
    for i, f in enumerate(feats):
        rh = f["rt"] + ro[i][:, :LANES]
        o0 = ro[i][:, LANES:] + rv[i]
        a_mat = jnp.where(eye, f["decay"], 0.0) + ab[i][:LANES]
        b_mat = ab[i][LANES:] + vk[i]
        outs.append(_mm(rh[:c] + rh[c:], states[i], RW_MODE_STATE, "nt") + o0[:c] + o0[c:])
        new_states.append(_mm(states[i], a_mat, RW_MODE_STATE) + b_mat)
    return outs, new_states


def _rw_scan_kernel(r_ref, lw_ref, k_ref, v_ref, na_ref, b_ref, g_ref, rk_ref, gng_ref, gnb_ref,
                    o_ref, s_ref):
    @pl.when(pl.program_id(0) == 0)
    def _():
        s_ref[...] = jnp.zeros_like(s_ref)

    lane = lax.broadcasted_iota(jnp.int32, (1, LANES), 1)
    first = lane < RW_HEAD_DIM
    batch = r_ref.shape[0]
    pairs = RW_WIDTH // LANES
    where = [(bi, slice(pr * LANES, (pr + 1) * LANES)) for bi in range(batch) for pr in range(pairs)]

    def head_sum(y):
        s_first = jnp.sum(jnp.where(first, y, 0.0), axis=-1, keepdims=True)
        s_second = jnp.sum(jnp.where(first, 0.0, y), axis=-1, keepdims=True)
        return jnp.where(first, s_first, s_second)

    probs = [(r_ref[bi, :, sl], lw_ref[bi, :, sl], k_ref[bi, :, sl], v_ref[bi, :, sl], na_ref[bi, :, sl],
              b_ref[bi, :, sl]) for bi, sl in where]
    outs, new_states = _rw_chunks(probs, [s_ref[i] for i in range(len(where))])
    for i, (bi, sl) in enumerate(where):
        s_ref[i] = new_states[i]
        r, _, k, v, _, _ = probs[i]
        o = outs[i]
        mu = head_sum(o) / RW_HEAD_DIM
        d = o - mu
        var = head_sum(d * d) / RW_HEAD_DIM
        normed = d * lax.rsqrt(var + RW_GN_EPS) * gng_ref[:, sl] + gnb_ref[:, sl]
        bonus = head_sum(r * k * rk_ref[:, sl]) * v
        o_ref[bi, :, sl] = ((normed + bonus) * _silu(g_ref[bi, :, sl])).astype(o_ref.dtype)


def _rw_scan(r, lw, k, v, na, b, p, r_k, gn_g, gn_b, batch, seq):
    c = RW_CHUNK
    as3d = lambda t: t.reshape(batch, seq, t.shape[-1])
    tok = pl.BlockSpec((batch, c, RW_WIDTH), lambda ci: (0, ci, 0))
    vec = pl.BlockSpec((1, RW_WIDTH), lambda ci: (0, 0))
    gate = pl.BlockSpec((batch, c, RW_WIDTH), lambda ci: (0, ci, SEC_RG // RW_WIDTH))
    out = pl.pallas_call(
        _rw_scan_kernel,
        out_shape=jax.ShapeDtypeStruct((batch, seq, RW_WIDTH), BF16),
        grid=(seq // c,),
        in_specs=[tok] * 6 + [gate, vec, vec, vec],
        out_specs=tok,
        scratch_shapes=[pltpu.VMEM((batch * RW_WIDTH // LANES, LANES, LANES), F32)],
        compiler_params=_params(("arbitrary",)),
        name="rw_scan",
    )(*[as3d(t) for t in (r, lw, k, v, na, b, p)], r_k, gn_g, gn_b)
    return out.reshape(batch * seq, RW_WIDTH)


def _rw_branch(p, mu, w0, w_up, a0, a_up, k_k, k_a, r_k, gn_g, gn_b, batch, seq):
    row = lambda t: t[None, :]
    head_ones = jnp.kron(jnp.eye(RW_HEADS, dtype=F32), jnp.ones((RW_HEAD_DIM, RW_HEAD_DIM), F32)).astype(BF16)
    mu_small = jnp.concatenate([jnp.zeros((LANES,), F32), mu[3 * RW_WIDTH:]])
    zeros = jnp.zeros((DECAY_LORA, RW_WIDTH), F32)
    wlora = jnp.concatenate([jnp.concatenate([w_up, zeros], axis=1),
                             jnp.concatenate([zeros, a_up], axis=1)], axis=0).astype(BF16)
    rr, lw, rk, rv, na, rb = _rw_prep(p, row(mu[:3 * RW_WIDTH]), row(mu_small), wlora, row(w0), row(a0),
                                      row(k_k), row(k_a), head_ones, seq)
    return _rw_scan(rr, lw, rk, rv, na, rb, p, r_k.reshape(1, RW_WIDTH), row(gn_g), row(gn_b), batch, seq)


def _merge_out_kernel(x_ref, ysb_ref, ymla_ref, yrw_ref, wsb_ref, wmla_ref, wrw_ref, g1_ref, g2_ref, g3_ref,
                      wout_ref, o_ref):
    merged = _sigmoid(g1_ref[...]) * _dot(ysb_ref[...], wsb_ref[...])
    merged = merged + _sigmoid(g2_ref[...]) * _dot(ymla_ref[...], wmla_ref[...])
    merged = merged + _sigmoid(g3_ref[...]) * _dot(yrw_ref[...], wrw_ref[...])
    o_ref[...] = x_ref[...] + _dot(merged.astype(BF16), wout_ref[...])


def _merge_out(x2, y_sb, y_mla, y_rw, w_sb, w_mla, w_rw, p, w_out, tm=256):
    m = p.shape[0]
    rows = lambda width: pl.BlockSpec((tm, width), lambda i: (i, 0))
    resident = lambda depth: pl.BlockSpec((depth, D_MODEL), lambda i: (0, 0), pipeline_mode=pl.Buffered(1))
    gate = lambda br: pl.BlockSpec((tm, D_MODEL), lambda i: (i, SEC_GATE // D_MODEL + br))
    return pl.pallas_call(
        _merge_out_kernel,
        out_shape=jax.ShapeDtypeStruct((m, D_MODEL), F32),
        grid=(m // tm,),
        in_specs=[rows(D_MODEL), rows(SB_WIDTH), rows(MLA_WIDTH), rows(RW_WIDTH),
                  resident(SB_WIDTH), resident(MLA_WIDTH), resident(RW_WIDTH),
                  gate(0), gate(1), gate(2), resident(D_MODEL)],
        out_specs=rows(D_MODEL),
        compiler_params=_params(("parallel",)),
        name="merge_out",
    )(x2, y_sb, y_mla, y_rw, w_sb, w_mla, w_rw, p, p, p, w_out)


def _pair_rope_cols(t):
    lead = t.shape[:-2]
    t = t.reshape(lead + (MLA_HEADS // 2, 2, 2, HALF_ROPE))
    t = jnp.swapaxes(t, -3, -2)
    return t.reshape(lead + (MLA_HEADS // 2 * LANES,))


def _pair_rope_gain(g):
    g1, g2 = g[:HALF_ROPE], g[HALF_ROPE:]
    return jnp.concatenate([g1, g1, g2, g2])[None, :]


def _rope_tables(seq):
    freqs = ROPE_THETA ** (-jnp.arange(HALF_ROPE, dtype=F32) / HALF_ROPE)
    ang = jnp.arange(seq, dtype=F32)[:, None] * freqs[None, :]
    c, s = jnp.cos(ang), jnp.sin(ang)
    return jnp.concatenate([c, c, c, c], axis=1), jnp.concatenate([-s, -s, s, s], axis=1)


def _mla_branch(p, cos, sin, q_norm_g, kv_norm_g, w_uq, w_ukv, qn_g, kn_g, batch, seq):
    row = lambda t: t[None, :]
    uq = w_uq.reshape(Q_LORA, MLA_HEADS, MLA_QK)
    wuq = jnp.concatenate([uq[:, :, :MLA_NOPE].reshape(Q_LORA, -1), _pair_rope_cols(uq[:, :, MLA_NOPE:])],
                          axis=1).astype(BF16)
    ukv = w_ukv.reshape(KV_LORA, MLA_HEADS, MLA_NOPE + MLA_V)
    wukv = jnp.concatenate([ukv[:, :, :MLA_NOPE].reshape(KV_LORA, -1),
                            ukv[:, :, MLA_NOPE:].reshape(KV_LORA, -1)], axis=1).astype(BF16)
    q, k, v = _mla_prep(p, cos, sin, row(q_norm_g), row(kv_norm_g), wuq, wukv,
                        row(qn_g[:MLA_NOPE]) * MLA_EXP2_SCALE, _pair_rope_gain(qn_g[MLA_NOPE:]) * MLA_EXP2_SCALE,
                        row(kn_g[:MLA_NOPE]), _pair_rope_gain(kn_g[MLA_NOPE:]), seq)
    return _mla_attn(q, k, v, p, batch, seq)


def kernel(x, norm_g, w_in, mla_q_norm_g, mla_kv_norm_g, mla_w_uq, mla_w_ukv, mla_qn_g, mla_kn_g,
           rw_mu, rw_w0, rw_w_up, rw_a0, rw_a_up, rw_k_k, rw_k_a, rw_r_k, rw_gn_g, rw_gn_b,
           w_br_sb, w_br_mla, w_br_rw, w_out):
    batch, seq, _ = x.shape
    depth = w_in.shape[0]
    x2 = x.reshape(batch * seq, D_MODEL)
    cos, sin = _rope_tables(seq)
    w_in_bf16 = _w_in_relayout(w_in)

    for l in range(depth):
        p = _in_proj(x2, norm_g[l][None, :], w_in_bf16, l)
        y_sb = _sb_attn(p, batch, seq)
        y_mla = _mla_branch(p, cos, sin, mla_q_norm_g[l], mla_kv_norm_g[l], mla_w_uq[l], mla_w_ukv[l],
                            mla_qn_g[l], mla_kn_g[l], batch, seq)
        y_rw = _rw_branch(p, rw_mu[l], rw_w0[l], rw_w_up[l], rw_a0[l], rw_a_up[l], rw_k_k[l], rw_k_a[l],
                          rw_r_k[l], rw_gn_g[l], rw_gn_b[l], batch, seq)
        x2 = _merge_out(x2, y_sb, y_mla, y_rw, w_br_sb[l].astype(BF16), w_br_mla[l].astype(BF16),
                        w_br_rw[l].astype(BF16), p, w_out[l].astype(BF16))

    return x2.reshape(batch, seq, D_MODEL)
```

```python
import functools
import math

import jax
import jax.numpy as jnp
from jax import lax
from jax.experimental import pallas as pl
from jax.experimental.pallas import tpu as pltpu

F32 = jnp.float32
BF16 = jnp.bfloat16

D_MODEL = 2048
EPS = 1e-6
CHUNK = 64

SB_HEADS = 4
SB_HEAD_DIM = 128
SB_WIDTH = SB_HEADS * SB_HEAD_DIM

MLA_HEADS = 8
MLA_NOPE = 128
MLA_ROPE = 64
MLA_QK = MLA_NOPE + MLA_ROPE
MLA_V = 128
MLA_WIDTH = MLA_HEADS * MLA_V
Q_LORA = 512
KV_LORA = 512
ROPE_THETA = 10000.0
HALF_ROPE = MLA_ROPE // 2

RW_HEADS = 8
RW_HEAD_DIM = 64
RW_WIDTH = RW_HEADS * RW_HEAD_DIM
DECAY_LORA = 64
ICL_LORA = 64
RW_GN_EPS = 64e-5
RW_CHUNK = 64
RW_CHUNKS_PER_STEP = 2

LANES = 128
VMEM_LIMIT = 56 * 1024 * 1024

SEC_SB = 0
SEC_MLA = 2048
SEC_RG = 4096
SEC_RW = 4608
SEC_GATE = 6144
SEC_SMALL = 12288
N_PROJ = SEC_SMALL + 256

_ORIG_KR = 4 * SB_WIDTH + Q_LORA + KV_LORA
_ORIG_MG = _ORIG_KR + MLA_ROPE
_ORIG_RW = _ORIG_MG + MLA_WIDTH
_ORIG_LORA = _ORIG_RW + 3 * RW_WIDTH
_ORIG_RG = _ORIG_LORA + DECAY_LORA + ICL_LORA
_ORIG_GATE = _ORIG_RG + RW_WIDTH
W_IN_MOVES = (
    (0, 0, _ORIG_KR),
    (SEC_MLA + Q_LORA + KV_LORA, _ORIG_MG, MLA_WIDTH),
    (SEC_RG, _ORIG_RG, RW_WIDTH),
    (SEC_RW, _ORIG_RW, 3 * RW_WIDTH),
    (SEC_GATE, _ORIG_GATE, 3 * D_MODEL),
    (SEC_SMALL, _ORIG_KR, HALF_ROPE),
    (SEC_SMALL + HALF_ROPE, _ORIG_KR, HALF_ROPE),
    (SEC_SMALL + 2 * HALF_ROPE, _ORIG_KR + HALF_ROPE, HALF_ROPE),
    (SEC_SMALL + 3 * HALF_ROPE, _ORIG_KR + HALF_ROPE, HALF_ROPE),
    (SEC_SMALL + LANES, _ORIG_LORA, DECAY_LORA + ICL_LORA),
)

NEG_BIG = -1e30

SB_DEAD_LOG_WEIGHT = -104.0

ATTN_TILE = 256
MLA_TILE = 512
MLA_EXP2_SCALE = (MLA_QK ** -0.5) * math.log2(math.e)
MLA_SUM_ROWS = 16
SB_GROUP = 4
MLA_GROUP = 4


def _dot(a, b, precision=None):
    return jnp.dot(a, b, preferred_element_type=F32, precision=precision)


def _dot_nt(a, b, precision=None):
    return lax.dot_general(a, b, (((1,), (1,)), ((), ())), preferred_element_type=F32,
                           precision=precision)


def _dot_tn(a, b, precision=None):
    return lax.dot_general(a, b, (((0,), (0,)), ((), ())), preferred_element_type=F32,
                           precision=precision)


def _split_bf16(x):
    hi = x.astype(BF16)
    lo = (x - hi.astype(F32)).astype(BF16)
    return hi, lo


def _softplus(z):
    return jnp.maximum(z, 0.0) + jnp.log(1.0 + jnp.exp(-jnp.abs(z)))


def _sigmoid(z):
    return 1.0 / (1.0 + jnp.exp(-z))


def _silu(z):
    return z * _sigmoid(z)


def _params(semantics):
    return pltpu.CompilerParams(dimension_semantics=semantics, vmem_limit_bytes=VMEM_LIMIT)


def _in_proj_kernel(x_ref, g_ref, w_ref, o_ref, h_ref):
    @pl.when(pl.program_id(1) == 0)
    def _():
        x = x_ref[...]
        rs = lax.rsqrt(jnp.mean(x * x, axis=-1, keepdims=True) + EPS)
        h_ref[...] = (x * rs * g_ref[...]).astype(BF16)

    o_ref[...] = _dot(h_ref[...], w_ref[...])


def _w_in_relayout_kernel(wt_ref, o_ref):
    for dst, src, width in W_IN_MOVES:
        for c in range(0, width, LANES):
            cw = min(LANES, width - c)
            o_ref[:, dst + c:dst + c + cw] = wt_ref[src + c:src + c + cw, :].T.astype(BF16)


def _w_in_relayout(w_in, tk=LANES):
    depth, _, n_in = w_in.shape
    return pl.pallas_call(
        _w_in_relayout_kernel,
        out_shape=jax.ShapeDtypeStruct((depth, D_MODEL, N_PROJ), BF16),
        grid=(depth, D_MODEL // tk),
        in_specs=[pl.BlockSpec((None, n_in, tk), lambda l, i: (l, 0, i))],
        out_specs=pl.BlockSpec((None, tk, N_PROJ), lambda l, i: (l, i, 0)),
        compiler_params=_params(("parallel", "parallel")),
        name="w_in_relayout",
    )(jnp.swapaxes(w_in, 1, 2))


def _in_proj(x2, g, w, layer, tm=1024, tn=1792):
    m = x2.shape[0]
    return pl.pallas_call(
        _in_proj_kernel,
        out_shape=jax.ShapeDtypeStruct((m, N_PROJ), F32),
        grid=(m // tm, N_PROJ // tn),
        in_specs=[
            pl.BlockSpec((tm, D_MODEL), lambda i, j: (i, 0), pipeline_mode=pl.Buffered(1)),
            pl.BlockSpec((1, D_MODEL), lambda i, j: (0, 0)),
            pl.BlockSpec((None, D_MODEL, tn), lambda i, j: (layer, 0, j)),
        ],
        out_specs=pl.BlockSpec((tm, tn), lambda i, j: (i, j)),
        scratch_shapes=[pltpu.VMEM((tm, D_MODEL), BF16)],
        compiler_params=_params(("parallel", "arbitrary")),
        name="in_proj",
    )(x2, g, w)


def _sb_attn_kernel(q_ref, k_ref, v_ref, g_ref, o_ref, kb_ref, vt_ref, *, t, heads):
    i = pl.program_id(2)
    nblk = k_ref.shape[0] // t
    scale = SB_HEAD_DIM ** -0.5
    key = lax.broadcasted_iota(jnp.int32, (t, t), 0)
    qry = lax.broadcasted_iota(jnp.int32, (t, t), 1)
    later = (qry > key).astype(BF16)
    causal = key < qry
    head = lambda h: slice(h * LANES, (h + 1) * LANES)
    qs = [q_ref[:, head(h)].astype(BF16) for h in range(heads)]

    @pl.when(i == 0)
    def _():
        for h in range(heads):
            for blk in range(nblk):
                rows = slice(blk * t, (blk + 1) * t)
                kb_ref[h, blk] = k_ref[rows, head(h)].astype(BF16)
                vt_ref[h, blk] = v_ref[rows, head(h)].T.astype(BF16)

    def step(j, carry, diagonal):
        zs = [_dot_nt(kb_ref[h, j], qs[h]) for h in range(heads)]
        stage = []
        for h in range(heads):
            z = zs[h] * scale
            sp = _softplus(z)
            log_fail = jnp.where(causal, -sp, 0.0) if diagonal else -sp
            hi, lo = _split_bf16(log_fail)
            stage.append((z - sp, log_fail, hi, lo))
        sums = [_dot(later, hi) + _dot(later, lo) for _, _, hi, lo in stage]
        ws = []
        for h in range(heads):
            w = jnp.exp(stage[h][0] + sums[h] + carry[h][0])
            if diagonal:
                w = jnp.where(causal, w, 0.0)
            ws.append(w.astype(BF16))
        out = []
        for h in range(heads):
            run, acc = carry[h]
            out.append((run + jnp.sum(stage[h][1], axis=0, keepdims=True), acc + _dot(vt_ref[h, j], ws[h])))
        return tuple(out)

    init = tuple((jnp.zeros((1, t), F32), jnp.zeros((SB_HEAD_DIM, t), F32)) for _ in range(heads))
    carry = step(i, init, True)

    def alive(c):
        return functools.reduce(jnp.maximum, [jnp.max(c[h][0]) for h in range(heads)]) >= SB_DEAD_LOG_WEIGHT

    def body(state):
        n, _, c = state
        c = step(i - 1 - n, c, False)
        return n + 1, alive(c), c

    _, _, carry = lax.while_loop(lambda s: jnp.logical_and(s[0] < i, s[1]), body, (0, alive(carry), carry))
    for h in range(heads):
        o_ref[:, head(h)] = (carry[h][1].T * _silu(g_ref[:, head(h)])).astype(o_ref.dtype)


def _sb_attn(p, batch, seq, t=ATTN_TILE, heads=SB_GROUP):
    nq = seq // t
    w = heads * LANES
    sec = SB_WIDTH // w
    tile = lambda s: pl.BlockSpec((t, w), lambda b, h, i: (b * nq + i, s * sec + h))
    full = lambda s: pl.BlockSpec((seq, w), lambda b, h, i: (b, s * sec + h))
    return pl.pallas_call(
        functools.partial(_sb_attn_kernel, t=t, heads=heads),
        out_shape=jax.ShapeDtypeStruct((batch * seq, SB_WIDTH), BF16),
        grid=(batch, SB_HEADS // heads, nq),
        in_specs=[tile(0), full(1), full(2), tile(3)],
        out_specs=pl.BlockSpec((t, w), lambda b, h, i: (b * nq + i, h)),
        scratch_shapes=[pltpu.VMEM((heads, nq, t, SB_HEAD_DIM), BF16),
                        pltpu.VMEM((heads, nq, SB_HEAD_DIM, t), BF16)],
        compiler_params=_params(("parallel", "parallel", "arbitrary")),
        name="sb_attn",
    )(p, p, p, p)


def _mla_prep_kernel(cq_ref, ckv_ref, sm_ref, cos_ref, sin_ref, gq_ref, gkv_ref, wuq_ref, wukv_ref,
                     gqn_ref, gqr_ref, gkn_ref, gkr_ref, q_ref, k_ref, v_ref):
    lane = lax.broadcasted_iota(jnp.int32, (1, LANES), 1)
    first = (lane % MLA_ROPE) < HALF_ROPE
    cos = cos_ref[...]
    sin = sin_ref[...]

    def latent_norm(c_ref, g_ref):
        c = c_ref[...]
        rs = lax.rsqrt(jnp.mean(c * c, axis=-1, keepdims=True) + EPS)
        return (c * rs * g_ref[...]).astype(BF16)

    def rotary(y):
        return y * cos + pltpu.roll(y, MLA_ROPE, axis=1) * sin

    def head_sums(sq):
        s_first = jnp.sum(jnp.where(first, sq, 0.0), axis=-1, keepdims=True)
        s_second = jnp.sum(jnp.where(first, 0.0, sq), axis=-1, keepdims=True)
        return s_first, s_second

    qfull = _dot(latent_norm(cq_ref, gq_ref), wuq_ref[...])
    kvfull = _dot(latent_norm(ckv_ref, gkv_ref), wukv_ref[...])
    v_ref[...] = kvfull[:, MLA_HEADS * MLA_NOPE:].astype(BF16)

    kr = sm_ref[...][:, :LANES]
    kr_ss, _ = head_sums(kr * kr)
    kr_rot = rotary(kr * gkr_ref[...])

    for pair in range(MLA_HEADS // 2):
        qr = qfull[:, MLA_HEADS * MLA_NOPE + pair * LANES:MLA_HEADS * MLA_NOPE + (pair + 1) * LANES]
        qr_ss = head_sums(qr * qr)
        q_rs, k_rs = [], []
        for e in range(2):
            h = 2 * pair + e
            qn = qfull[:, h * MLA_NOPE:(h + 1) * MLA_NOPE]
            kn = kvfull[:, h * MLA_NOPE:(h + 1) * MLA_NOPE]
            qs = lax.rsqrt((jnp.sum(qn * qn, axis=-1, keepdims=True) + qr_ss[e]) / MLA_QK + EPS)
            ks = lax.rsqrt((jnp.sum(kn * kn, axis=-1, keepdims=True) + kr_ss) / MLA_QK + EPS)
            q_rs.append(qs)
            k_rs.append(ks)
            q_ref[:, 2 * h * LANES:(2 * h + 1) * LANES] = (qn * qs * gqn_ref[...]).astype(BF16)
            k_ref[:, 2 * h * LANES:(2 * h + 1) * LANES] = (kn * ks * gkn_ref[...]).astype(BF16)
        q_rot = rotary(qr * jnp.where(first, q_rs[0], q_rs[1]) * gqr_ref[...])
        k_rot = (kr_rot * jnp.where(first, k_rs[0], k_rs[1])).astype(BF16)
        for e in range(2):
            h = 2 * pair + e
            own = first if e == 0 else jnp.logical_not(first)
            q_ref[:, (2 * h + 1) * LANES:(2 * h + 2) * LANES] = jnp.where(own, q_rot, 0.0).astype(BF16)
            k_ref[:, (2 * h + 1) * LANES:(2 * h + 2) * LANES] = k_rot


def _mla_prep(p, cos, sin, gq, gkv, wuq, wukv, gqn, gqr, gkn, gkr, seq, tm=512):
    m = p.shape[0]
    sblk = seq // tm
    row = lambda width, cb: pl.BlockSpec((tm, width), lambda i: (i, cb))
    const = lambda shape: pl.BlockSpec(shape, lambda i: (0, 0))
    pos = pl.BlockSpec((tm, LANES), lambda i: (i % sblk, 0))
    qk_width = MLA_HEADS * 2 * LANES
    return pl.pallas_call(
        _mla_prep_kernel,
        out_shape=(jax.ShapeDtypeStruct((m, qk_width), BF16),
                   jax.ShapeDtypeStruct((m, qk_width), BF16),
                   jax.ShapeDtypeStruct((m, MLA_WIDTH), BF16)),
        grid=(m // tm,),
        in_specs=[
            row(Q_LORA, SEC_MLA // Q_LORA),
            row(KV_LORA, SEC_MLA // KV_LORA + 1),
            row(256, SEC_SMALL // 256),
            pos, pos,
            const((1, Q_LORA)), const((1, KV_LORA)),
            const(wuq.shape), const(wukv.shape),
            const((1, LANES)), const((1, LANES)), const((1, LANES)), const((1, LANES)),
        ],
        out_specs=(row(qk_width, 0), row(qk_width, 0), row(MLA_WIDTH, 0)),
        compiler_params=_params(("parallel",)),
        name="mla_prep",
    )(p, p, p, cos, sin, gq, gkv, wuq, wukv, gqn, gqr, gkn, gkr)


def _mla_attn_kernel(q_ref, k_ref, v_ref, g_ref, o_ref, vt_ref, sa_ref, sb_ref, sd_ref, m_ref, acc_ref,
                     *, t, heads):
    i = pl.program_id(2)
    nblk = v_ref.shape[0] // t
    qw = 2 * LANES
    key_chunk = lax.broadcasted_iota(jnp.int32, (t, t), 0) // CHUNK
    qry_chunk = lax.broadcasted_iota(jnp.int32, (t, t), 1) // CHUNK
    visible = key_chunk <= qry_chunk
    qs = [q_ref[:, h * qw:(h + 1) * qw] for h in range(heads)]

    @pl.when(i == 0)
    def _():
        for h in range(heads):
            for blk in range(nblk):
                vb = v_ref[blk * t:(blk + 1) * t, h * LANES:(h + 1) * LANES]
                vt_ref[h, blk, :MLA_V] = vb.astype(F32).T.astype(BF16)
                vt_ref[h, blk, MLA_V:] = jnp.ones((MLA_SUM_ROWS, t), BF16)

    m_ref[...] = jnp.full(m_ref.shape, NEG_BIG, F32)
    acc_ref[...] = jnp.zeros(acc_ref.shape, F32)

    def issue_scores(j, buf):
        start = pl.multiple_of(j * t, t)
        for h in range(heads):
            buf[h] = _dot_nt(k_ref[pl.ds(start, t), h * qw:(h + 1) * qw], qs[h])

    def absorb(buf, j, diagonal):
        soft = []
        for h in range(heads):
            s = jnp.where(visible, buf[h], NEG_BIG) if diagonal else buf[h]
            m = m_ref[h]
            m_new = jnp.maximum(m, jnp.max(s, axis=0, keepdims=True))
            m_ref[h] = m_new
            soft.append((jnp.exp2(m - m_new), jnp.exp2(s - m_new).astype(BF16)))
        for h in range(heads):
            alpha, pr = soft[h]
            acc_ref[h] = alpha * acc_ref[h] + _dot(vt_ref[h, j], pr)

    issue_scores(i, sd_ref)
    issue_scores(0, sa_ref)
    last = jnp.maximum(i - 1, 0)

    def pair(n, _):
        issue_scores(2 * n + 1, sb_ref)
        absorb(sa_ref, 2 * n, False)
        issue_scores(jnp.minimum(2 * n + 2, last), sa_ref)
        absorb(sb_ref, 2 * n + 1, False)
        return 0

    lax.fori_loop(0, i // 2, pair, 0)

    @pl.when(i % 2 == 1)
    def _():
        absorb(sa_ref, i - 1, False)

    absorb(sd_ref, i, True)
    for h in range(heads):
        hs = slice(h * LANES, (h + 1) * LANES)
        acc = acc_ref[h]
        y = (acc[:MLA_V] / acc[MLA_V:MLA_V + 1]).T
        o_ref[:, hs] = (y * _silu(g_ref[:, hs])).astype(o_ref.dtype)


def _mla_attn(q, k, v, p, batch, seq, t=MLA_TILE, heads=MLA_GROUP):
    nq = seq // t
    qw = heads * 2 * LANES
    vw = heads * LANES
    gate0 = (SEC_MLA + Q_LORA + KV_LORA) // vw
    return pl.pallas_call(
        functools.partial(_mla_attn_kernel, t=t, heads=heads),
        out_shape=jax.ShapeDtypeStruct((batch * seq, MLA_WIDTH), BF16),
        grid=(batch, MLA_HEADS // heads, nq),
        in_specs=[
            pl.BlockSpec((t, qw), lambda b, h, i: (b * nq + i, h)),
            pl.BlockSpec((seq, qw), lambda b, h, i: (b, h)),
            pl.BlockSpec((seq, vw), lambda b, h, i: (b, h)),
            pl.BlockSpec((t, vw), lambda b, h, i: (b * nq + i, gate0 + h)),
        ],
        out_specs=pl.BlockSpec((t, vw), lambda b, h, i: (b * nq + i, h)),
        scratch_shapes=[pltpu.VMEM((heads, nq, MLA_V + MLA_SUM_ROWS, t), BF16)]
        + [pltpu.VMEM((heads, t, t), F32)] * 3
        + [pltpu.VMEM((heads, 1, t), F32)]
        + [pltpu.VMEM((heads, MLA_V + MLA_SUM_ROWS, t), F32)],
        compiler_params=_params(("parallel", "parallel", "arbitrary")),
        name="mla_attn",
    )(q, k, v, p)


def _rw_prep_kernel(cur_ref, prev_ref, smc_ref, smp_ref, mu_ref, mus_ref, wlora_ref, w0_ref, a0_ref,
                    kk_ref, ka_ref, ones_ref, r_ref, lw_ref, k_ref, v_ref, na_ref, b_ref,
                    *, tm, seq):
    i = pl.program_id(0)
    at_start = (i * tm) % seq == 0
    row = lax.broadcasted_iota(jnp.int32, (tm, 1), 0)

    def shifted(c_ref, p_ref, mu):
        cur = c_ref[...]
        last = jnp.where(at_start, 0.0, p_ref[...][7:8, :])
        prev = jnp.where(row == 0, last, pltpu.roll(cur, 1, axis=0))
        return cur + mu * (prev - cur)

    main = shifted(cur_ref, prev_ref, mu_ref[...])
    small = shifted(smc_ref, smp_ref, mus_ref[...])[:, LANES:]
    r = main[:, :RW_WIDTH]
    k = main[:, RW_WIDTH:2 * RW_WIDTH]
    v = main[:, 2 * RW_WIDTH:]

    lane = lax.broadcasted_iota(jnp.int32, (1, LANES), 1)
    lora_in = jnp.where(lane < DECAY_LORA, jnp.tanh(small), small).astype(BF16)
    lora = _dot(lora_in, wlora_ref[...])
    w_log = -_softplus(-(w0_ref[...] + lora[:, :RW_WIDTH])) - 0.5
    a = _sigmoid(a0_ref[...] + lora[:, RW_WIDTH:])

    kk = k * kk_ref[...]
    hi, lo = _split_bf16(kk * kk)
    ss = _dot(hi, ones_ref[...]) + _dot(lo, ones_ref[...])
    kk = kk / jnp.maximum(jnp.sqrt(ss), 1e-12)

    r_ref[...] = r
    lw_ref[...] = -jnp.exp(w_log)
    k_ref[...] = k * (1.0 + (a - 1.0) * ka_ref[...])
    v_ref[...] = v
    na_ref[...] = -kk
    b_ref[...] = kk * a


def _rw_prep(p, mu, mus, wlora, w0, a0, k_k, k_a, ones_bd, seq, tm=512):
    m = p.shape[0]
    vec = lambda width: pl.BlockSpec((1, width), lambda i: (0, 0))
    out = jax.ShapeDtypeStruct((m, RW_WIDTH), F32)
    main_w = 3 * RW_WIDTH
    prev_blk = lambda i: (jnp.maximum(i * (tm // 8) - 1, 0))
    return pl.pallas_call(
        functools.partial(_rw_prep_kernel, tm=tm, seq=seq),
        out_shape=(out,) * 6,
        grid=(m // tm,),
        in_specs=[
            pl.BlockSpec((tm, main_w), lambda i: (i, SEC_RW // main_w)),
            pl.BlockSpec((8, main_w), lambda i: (prev_blk(i), SEC_RW // main_w)),
            pl.BlockSpec((tm, 256), lambda i: (i, SEC_SMALL // 256)),
            pl.BlockSpec((8, 256), lambda i: (prev_blk(i), SEC_SMALL // 256)),
            vec(main_w), vec(256),
            pl.BlockSpec(wlora.shape, lambda i: (0, 0)),
            vec(RW_WIDTH), vec(RW_WIDTH), vec(RW_WIDTH), vec(RW_WIDTH),
            pl.BlockSpec((RW_WIDTH, RW_WIDTH), lambda i: (0, 0)),
        ],
        out_specs=(pl.BlockSpec((tm, RW_WIDTH), lambda i: (i, 0)),) * 6,
        compiler_params=_params(("parallel",)),
        name="rw_prep",
    )(p, p, p, p, mu, mus, wlora, w0, a0, k_k, k_a, ones_bd)


_HI = lax.Precision.HIGHEST

RW_MODE_SCORE = "b1"
RW_MODE_INV = "b1"
RW_MODE_MID = "b1"
RW_MODE_STATE = "b1"


def _mm(a, b, mode, form="nn"):
    f = {"nn": _dot, "nt": _dot_nt, "tn": _dot_tn}[form]
    if mode == "hi":
        return f(a, b, _HI)
    if mode == "b1":
        return f(a.astype(BF16), b.astype(BF16))
    a_hi, a_lo = _split_bf16(a)
    b_hi, b_lo = _split_bf16(b)
    return f(a_hi, b_hi) + (f(a_hi, b_lo) + f(a_lo, b_hi))


def _rw_chunks(probs):
    c = RW_CHUNK
    n = 2 * c
    lane = lax.broadcasted_iota(jnp.int32, (1, LANES), 1)
    first = lane < RW_HEAD_DIM
    ti = lax.broadcasted_iota(jnp.int32, (c, c), 0)
    si = lax.broadcasted_iota(jnp.int32, (c, c), 1)
    tri = (si <= ti).astype(BF16)
    ri = lax.broadcasted_iota(jnp.int32, (n, n), 0)
    ci = lax.broadcasted_iota(jnp.int32, (n, n), 1)
    strict = (ci % c) < (ri % c)
    incl = (ci % c) <= (ri % c)
    eye = ri == ci

    def stack(x):
        return jnp.concatenate([jnp.where(first, x, 0.0), jnp.where(first, 0.0, x)], axis=0)

    splits = [_split_bf16(lw) for _, lw, _, _, _, _ in probs]
    cums = [_dot(tri, h) + _dot(tri, l) for h, l in splits]

    feats = []
    for (r, lw, k, v, na, b), cum in zip(probs, cums):
        total = cum[c - 1:c, :]
        p_inv = jnp.exp(-cum)
        p_end = jnp.exp(total - cum)
        at = stack(na * jnp.exp(cum - lw))
        rt = stack(r * jnp.exp(cum))
        lhs = jnp.concatenate([at, rt], axis=0)
        rhs = jnp.concatenate([stack(b * p_inv), stack(k * p_inv)], axis=0)
        feats.append(dict(at=at, rt=rt, lhs=lhs, rhs=rhs, bh=stack(b * p_end), kh=stack(k * p_end),
                          vs=stack(v), decay=jnp.exp(total)))

    gs = [_mm(f["lhs"], f["rhs"], RW_MODE_SCORE, "nt") for f in feats]
    l_ab = [jnp.where(strict, g[:n, :n], 0.0) for g in gs]
    l_ak = [jnp.where(strict, g[:n, n:], 0.0) for g in gs]
    m_rb = [jnp.where(incl, g[n:, :n], 0.0) for g in gs]
    m_rk = [jnp.where(incl, g[n:, n:], 0.0) for g in gs]

    xs = l_ab
    tinv = [jnp.where(eye, 1.0, 0.0) + x for x in xs]
    for _ in range(int(math.log2(c)) - 1):
        xs = [_mm(x, x, RW_MODE_INV) for x in xs]
        tinv = [t + _mm(t, x, RW_MODE_INV) for t, x in zip(tinv, xs)]

    lv = [_mm(l, f["vs"], RW_MODE_MID) for l, f in zip(l_ak, feats)]
    wu = [_mm(t, jnp.concatenate([f["at"], y], axis=1), RW_MODE_MID)
          for t, f, y in zip(tinv, feats, lv)]
    ro = [_mm(m, w, RW_MODE_MID) for m, w in zip(m_rb, wu)]
    rv = [_mm(m, f["vs"], RW_MODE_MID) for m, f in zip(m_rk, feats)]
    ab = [_mm(w, f["bh"], RW_MODE_STATE, "tn") for w, f in zip(wu, feats)]
    vk = [_mm(f["vs"], f["kh"], RW_MODE_STATE, "tn") for f in feats]

    terms = []
    for i, f in enumerate(feats):
        rh = f["rt"] + ro[i][:, :LANES]
        o0 = ro[i][:, LANES:] + rv[i]
        a_mat = jnp.where(eye, f["decay"], 0.0) + ab[i][:LANES]
        b_mat = ab[i][LANES:] + vk[i]
        terms.append((rh[:c] + rh[c:], o0[:c] + o0[c:], a_mat, b_mat))
    return terms


def _rw_scan_kernel(r_ref, lw_ref, k_ref, v_ref, na_ref, b_ref, g_ref, rk_ref, gng_ref, gnb_ref,
                    o_ref, s_ref):
    @pl.when(pl.program_id(0) == 0)
    def _():
        s_ref[...] = jnp.zeros_like(s_ref)

    lane = lax.broadcasted_iota(jnp.int32, (1, LANES), 1)
    first = lane < RW_HEAD_DIM
    batch = r_ref.shape[0]
    pairs = RW_WIDTH // LANES
    where = [(bi, slice(pr * LANES, (pr + 1) * LANES)) for bi in range(batch) for pr in range(pairs)]
    rows = [slice(q * RW_CHUNK, (q + 1) * RW_CHUNK) for q in range(RW_CHUNKS_PER_STEP)]

    def head_sum(y):
        s_first = jnp.sum(jnp.where(first, y, 0.0), axis=-1, keepdims=True)
        s_second = jnp.sum(jnp.where(first, 0.0, y), axis=-1, keepdims=True)
        return jnp.where(first, s_first, s_second)

    probs = [(r_ref[bi, rs, sl], lw_ref[bi, rs, sl], k_ref[bi, rs, sl], v_ref[bi, rs, sl], na_ref[bi, rs, sl],
              b_ref[bi, rs, sl]) for bi, sl in where for rs in rows]
    terms = _rw_chunks(probs)
    for i, (bi, sl) in enumerate(where):
        state = s_ref[i]
        for q, rs in enumerate(rows):
            n = i * RW_CHUNKS_PER_STEP + q
            rhat, o0, a_mat, b_mat = terms[n]
            o = _mm(rhat, state, RW_MODE_STATE, "nt") + o0
            state = _mm(state, a_mat, RW_MODE_STATE) + b_mat
            r, _, k, v, _, _ = probs[n]
            mu = head_sum(o) / RW_HEAD_DIM
            d = o - mu
            var = head_sum(d * d) / RW_HEAD_DIM
            normed = d * lax.rsqrt(var + RW_GN_EPS) * gng_ref[:, sl] + gnb_ref[:, sl]
            bonus = head_sum(r * k * rk_ref[:, sl]) * v
            o_ref[bi, rs, sl] = ((normed + bonus) * _silu(g_ref[bi, rs, sl])).astype(o_ref.dtype)
        s_ref[i] = state


def _rw_scan(r, lw, k, v, na, b, p, r_k, gn_g, gn_b, batch, seq):
    c = RW_CHUNK * RW_CHUNKS_PER_STEP
    as3d = lambda t: t.reshape(batch, seq, t.shape[-1])
    tok = pl.BlockSpec((batch, c, RW_WIDTH), lambda ci: (0, ci, 0))
    vec = pl.BlockSpec((1, RW_WIDTH), lambda ci: (0, 0))
    gate = pl.BlockSpec((batch, c, RW_WIDTH), lambda ci: (0, ci, SEC_RG // RW_WIDTH))
    out = pl.pallas_call(
        _rw_scan_kernel,
        out_shape=jax.ShapeDtypeStruct((batch, seq, RW_WIDTH), BF16),
        grid=(seq // c,),
        in_specs=[tok] * 6 + [gate, vec, vec, vec],
        out_specs=tok,
        scratch_shapes=[pltpu.VMEM((batch * RW_WIDTH // LANES, LANES, LANES), F32)],
        compiler_params=_params(("arbitrary",)),
        name="rw_scan",
    )(*[as3d(t) for t in (r, lw, k, v, na, b, p)], r_k, gn_g, gn_b)
    return out.reshape(batch * seq, RW_WIDTH)


def _rw_branch(p, mu, w0, w_up, a0, a_up, k_k, k_a, r_k, gn_g, gn_b, batch, seq):
    row = lambda t: t[None, :]
    head_ones = jnp.kron(jnp.eye(RW_HEADS, dtype=F32), jnp.ones((RW_HEAD_DIM, RW_HEAD_DIM), F32)).astype(BF16)
    mu_small = jnp.concatenate([jnp.zeros((LANES,), F32), mu[3 * RW_WIDTH:]])
    zeros = jnp.zeros((DECAY_LORA, RW_WIDTH), F32)
    wlora = jnp.concatenate([jnp.concatenate([w_up, zeros], axis=1),
                             jnp.concatenate([zeros, a_up], axis=1)], axis=0).astype(BF16)
    rr, lw, rk, rv, na, rb = _rw_prep(p, row(mu[:3 * RW_WIDTH]), row(mu_small), wlora, row(w0), row(a0),
                                      row(k_k), row(k_a), head_ones, seq)
    return _rw_scan(rr, lw, rk, rv, na, rb, p, r_k.reshape(1, RW_WIDTH), row(gn_g), row(gn_b), batch, seq)


def _merge_out_kernel(x_ref, ysb_ref, ymla_ref, yrw_ref, wsb_ref, wmla_ref, wrw_ref, g1_ref, g2_ref, g3_ref,
                      wout_ref, o_ref):
    merged = _sigmoid(g1_ref[...]) * _dot(ysb_ref[...], wsb_ref[...])
    merged = merged + _sigmoid(g2_ref[...]) * _dot(ymla_ref[...], wmla_ref[...])
    merged = merged + _sigmoid(g3_ref[...]) * _dot(yrw_ref[...], wrw_ref[...])
    o_ref[...] = x_ref[...] + _dot(merged.astype(BF16), wout_ref[...])


def _merge_out(x2, y_sb, y_mla, y_rw, w_sb, w_mla, w_rw, p, w_out, tm=256):
    m = p.shape[0]
    rows = lambda width: pl.BlockSpec((tm, width), lambda i: (i, 0))
    resident = lambda depth: pl.BlockSpec((depth, D_MODEL), lambda i: (0, 0), pipeline_mode=pl.Buffered(1))
    gate = lambda br: pl.BlockSpec((tm, D_MODEL), lambda i: (i, SEC_GATE // D_MODEL + br))
    return pl.pallas_call(
        _merge_out_kernel,
        out_shape=jax.ShapeDtypeStruct((m, D_MODEL), F32),
        grid=(m // tm,),
        in_specs=[rows(D_MODEL), rows(SB_WIDTH), rows(MLA_WIDTH), rows(RW_WIDTH),
                  resident(SB_WIDTH), resident(MLA_WIDTH), resident(RW_WIDTH),
                  gate(0), gate(1), gate(2), resident(D_MODEL)],
        out_specs=rows(D_MODEL),
        compiler_params=_params(("parallel",)),
        name="merge_out",
    )(x2, y_sb, y_mla, y_rw, w_sb, w_mla, w_rw, p, p, p, w_out)


def _pair_rope_cols(t):
    lead = t.shape[:-2]
    t = t.reshape(lead + (MLA_HEADS // 2, 2, 2, HALF_ROPE))
    t = jnp.swapaxes(t, -3, -2)
    return t.reshape(lead + (MLA_HEADS // 2 * LANES,))


def _pair_rope_gain(g):
    g1, g2 = g[:HALF_ROPE], g[HALF_ROPE:]
    return jnp.concatenate([g1, g1, g2, g2])[None, :]


def _rope_tables(seq):
    freqs = ROPE_THETA ** (-jnp.arange(HALF_ROPE, dtype=F32) / HALF_ROPE)
    ang = jnp.arange(seq, dtype=F32)[:, None] * freqs[None, :]
    c, s = jnp.cos(ang), jnp.sin(ang)
    return jnp.concatenate([c, c, c, c], axis=1), jnp.concatenate([-s, -s, s, s], axis=1)


def _mla_branch(p, cos, sin, q_norm_g, kv_norm_g, w_uq, w_ukv, qn_g, kn_g, batch, seq):
    row = lambda t: t[None, :]
    uq = w_uq.reshape(Q_LORA, MLA_HEADS, MLA_QK)
    wuq = jnp.concatenate([uq[:, :, :MLA_NOPE].reshape(Q_LORA, -1), _pair_rope_cols(uq[:, :, MLA_NOPE:])],
                          axis=1).astype(BF16)
    ukv = w_ukv.reshape(KV_LORA, MLA_HEADS, MLA_NOPE + MLA_V)
    wukv = jnp.concatenate([ukv[:, :, :MLA_NOPE].reshape(KV_LORA, -1),
                            ukv[:, :, MLA_NOPE:].reshape(KV_LORA, -1)], axis=1).astype(BF16)
    q, k, v = _mla_prep(p, cos, sin, row(q_norm_g), row(kv_norm_g), wuq, wukv,
                        row(qn_g[:MLA_NOPE]) * MLA_EXP2_SCALE, _pair_rope_gain(qn_g[MLA_NOPE:]) * MLA_EXP2_SCALE,
                        row(kn_g[:MLA_NOPE]), _pair_rope_gain(kn_g[MLA_NOPE:]), seq)
    return _mla_attn(q, k, v, p, batch, seq)


def kernel(x, norm_g, w_in, mla_q_norm_g, mla_kv_norm_g, mla_w_uq, mla_w_ukv, mla_qn_g, mla_kn_g,
           rw_mu, rw_w0, rw_w_up, rw_a0, rw_a_up, rw_k_k, rw_k_a, rw_r_k, rw_gn_g, rw_gn_b,
           w_br_sb, w_br_mla, w_br_rw, w_out):
    batch, seq, _ = x.shape
    depth = w_in.shape[0]
    x2 = x.reshape(batch * seq, D_MODEL)
    cos, sin = _rope_tables(seq)
    w_in_bf16 = _w_in_relayout(w_in)

    for l in range(depth):
        p = _in_proj(x2, norm_g[l][None, :], w_in_bf16, l)
        y_sb = _sb_attn(p, batch, seq)
        y_mla = _mla_branch(p, cos, sin, mla_q_norm_g[l], mla_kv_norm_g[l], mla_w_uq[l], mla_w_ukv[l],
                            mla_qn_g[l], mla_kn_g[l], batch, seq)
        y_rw = _rw_branch(p, rw_mu[l], rw_w0[l], rw_w_up[l], rw_a0[l], rw_a_up[l], rw_k_k[l], rw_k_a[l],
                          rw_r_k[l], rw_gn_g[l], rw_gn_b[l], batch, seq)
        x2 = _merge_out(x2, y_sb, y_mla, y_rw, w_br_sb[l].astype(BF16), w_br_mla[l].astype(BF16),
                        w_br_rw[l].astype(BF16), p, w_out[l].astype(BF16))

    return x2.reshape(batch, seq, D_MODEL)
```

```python
import functools
import math

import jax
import jax.numpy as jnp
from jax import lax
from jax.experimental import pallas as pl
from jax.experimental.pallas import tpu as pltpu

F32 = jnp.float32
BF16 = jnp.bfloat16

D_MODEL = 2048
EPS = 1e-6
CHUNK = 64

SB_HEADS = 4
SB_HEAD_DIM = 128
SB_WIDTH = SB_HEADS * SB_HEAD_DIM

MLA_HEADS = 8
MLA_NOPE = 128
MLA_ROPE = 64
MLA_QK = MLA_NOPE + MLA_ROPE
MLA_V = 128
MLA_WIDTH = MLA_HEADS * MLA_V
Q_LORA = 512
KV_LORA = 512
ROPE_THETA = 10000.0
HALF_ROPE = MLA_ROPE // 2

RW_HEADS = 8
RW_HEAD_DIM = 64
RW_WIDTH = RW_HEADS * RW_HEAD_DIM
DECAY_LORA = 64
ICL_LORA = 64
RW_GN_EPS = 64e-5
RW_CHUNK = 64
RW_CHUNKS_PER_STEP = 2

LANES = 128
VMEM_LIMIT = 56 * 1024 * 1024

SEC_SB = 0
SEC_MLA = 2048
SEC_RG = 4096
SEC_RW = 4608
SEC_GATE = 6144
SEC_SMALL = 12288
N_PROJ = SEC_SMALL + 256

_ORIG_KR = 4 * SB_WIDTH + Q_LORA + KV_LORA
_ORIG_MG = _ORIG_KR + MLA_ROPE
_ORIG_RW = _ORIG_MG + MLA_WIDTH
_ORIG_LORA = _ORIG_RW + 3 * RW_WIDTH
_ORIG_RG = _ORIG_LORA + DECAY_LORA + ICL_LORA
_ORIG_GATE = _ORIG_RG + RW_WIDTH
W_IN_MOVES = (
    (0, 0, _ORIG_KR),
    (SEC_MLA + Q_LORA + KV_LORA, _ORIG_MG, MLA_WIDTH),
    (SEC_RG, _ORIG_RG, RW_WIDTH),
    (SEC_RW, _ORIG_RW, 3 * RW_WIDTH),
    (SEC_GATE, _ORIG_GATE, 3 * D_MODEL),
    (SEC_SMALL, _ORIG_KR, HALF_ROPE),
    (SEC_SMALL + HALF_ROPE, _ORIG_KR, HALF_ROPE),
    (SEC_SMALL + 2 * HALF_ROPE, _ORIG_KR + HALF_ROPE, HALF_ROPE),
    (SEC_SMALL + 3 * HALF_ROPE, _ORIG_KR + HALF_ROPE, HALF_ROPE),
    (SEC_SMALL + LANES, _ORIG_LORA, DECAY_LORA + ICL_LORA),
)

NEG_BIG = -1e30

SB_DEAD_LOG_WEIGHT = -104.0

ATTN_TILE = 256
MLA_TILE = 512
MLA_EXP2_SCALE = (MLA_QK ** -0.5) * math.log2(math.e)
MLA_SUM_ROWS = 16
SB_GROUP = 4
MLA_GROUP = 4


def _dot(a, b, precision=None):
    return jnp.dot(a, b, preferred_element_type=F32, precision=precision)


def _dot_nt(a, b, precision=None):
    return lax.dot_general(a, b, (((1,), (1,)), ((), ())), preferred_element_type=F32,
                           precision=precision)


def _dot_tn(a, b, precision=None):
    return lax.dot_general(a, b, (((0,), (0,)), ((), ())), preferred_element_type=F32,
                           precision=precision)


def _split_bf16(x):
    hi = x.astype(BF16)
    lo = (x - hi.astype(F32)).astype(BF16)
    return hi, lo


def _softplus(z):
    return jnp.maximum(z, 0.0) + jnp.log(1.0 + jnp.exp(-jnp.abs(z)))


def _sigmoid(z):
    return 1.0 / (1.0 + jnp.exp(-z))


def _silu(z):
    return z * _sigmoid(z)


def _params(semantics):
    return pltpu.CompilerParams(dimension_semantics=semantics, vmem_limit_bytes=VMEM_LIMIT)


def _in_proj_kernel(x_ref, g_ref, w_ref, o_ref, h_ref):
    @pl.when(pl.program_id(1) == 0)
    def _():
        x = x_ref[...]
        rs = lax.rsqrt(jnp.mean(x * x, axis=-1, keepdims=True) + EPS)
        h_ref[...] = (x * rs * g_ref[...]).astype(BF16)

    o_ref[...] = _dot(h_ref[...], w_ref[...])


def _w_in_relayout_kernel(wt_ref, o_ref):
    for dst, src, width in W_IN_MOVES:
        for c in range(0, width, LANES):
            cw = min(LANES, width - c)
            o_ref[:, dst + c:dst + c + cw] = wt_ref[src + c:src + c + cw, :].T.astype(BF16)


def _w_in_relayout(w_in, tk=LANES):
    depth, _, n_in = w_in.shape
    return pl.pallas_call(
        _w_in_relayout_kernel,
        out_shape=jax.ShapeDtypeStruct((depth, D_MODEL, N_PROJ), BF16),
        grid=(depth, D_MODEL // tk),
        in_specs=[pl.BlockSpec((None, n_in, tk), lambda l, i: (l, 0, i))],
        out_specs=pl.BlockSpec((None, tk, N_PROJ), lambda l, i: (l, i, 0)),
        compiler_params=_params(("parallel", "parallel")),
        name="w_in_relayout",
    )(jnp.swapaxes(w_in, 1, 2))


def _in_proj(x2, g, w, layer, tm=1024, tn=1792):
    m = x2.shape[0]
    return pl.pallas_call(
        _in_proj_kernel,
        out_shape=jax.ShapeDtypeStruct((m, N_PROJ), F32),
        grid=(m // tm, N_PROJ // tn),
        in_specs=[
            pl.BlockSpec((tm, D_MODEL), lambda i, j: (i, 0), pipeline_mode=pl.Buffered(1)),
            pl.BlockSpec((1, D_MODEL), lambda i, j: (0, 0)),
            pl.BlockSpec((None, D_MODEL, tn), lambda i, j: (layer, 0, j)),
        ],
        out_specs=pl.BlockSpec((tm, tn), lambda i, j: (i, j)),
        scratch_shapes=[pltpu.VMEM((tm, D_MODEL), BF16)],
        compiler_params=_params(("parallel", "arbitrary")),
        name="in_proj",
    )(x2, g, w)


def _sb_attn_kernel(q_ref, k_ref, v_ref, g_ref, o_ref, kb_ref, vt_ref, *, t, heads):
    i = pl.program_id(2)
    nblk = k_ref.shape[0] // t
    scale = SB_HEAD_DIM ** -0.5
    key = lax.broadcasted_iota(jnp.int32, (t, t), 0)
    qry = lax.broadcasted_iota(jnp.int32, (t, t), 1)
    later = (qry > key).astype(BF16)
    causal = key < qry
    head = lambda h: slice(h * LANES, (h + 1) * LANES)
    qs = [q_ref[:, head(h)].astype(BF16) for h in range(heads)]

    @pl.when(i == 0)
    def _():
        for h in range(heads):
            for blk in range(nblk):
                rows = slice(blk * t, (blk + 1) * t)
                kb_ref[h, blk] = k_ref[rows, head(h)].astype(BF16)
                vt_ref[h, blk] = v_ref[rows, head(h)].T.astype(BF16)

    def sweep(blocks, carry):
        zs = [[_dot_nt(kb_ref[h, j], qs[h]) for h in range(heads)] for j, _ in blocks]
        stage = []
        for (j, diagonal), zb in zip(blocks, zs):
            per_head = []
            for h in range(heads):
                z = zb[h] * scale
                sp = _softplus(z)
                log_fail = jnp.where(causal, -sp, 0.0) if diagonal else -sp
                hi, lo = _split_bf16(log_fail)
                per_head.append((z - sp, log_fail, hi, lo))
            stage.append(per_head)
        sums = [[_dot(later, hi) + _dot(later, lo) for _, _, hi, lo in per_head] for per_head in stage]
        runs = [carry[h][0] for h in range(heads)]
        ws = []
        for (j, diagonal), per_head, sums_b in zip(blocks, stage, sums):
            wb = []
            for h in range(heads):
                w = jnp.exp(per_head[h][0] + sums_b[h] + runs[h])
                if diagonal:
                    w = jnp.where(causal, w, 0.0)
                wb.append(w.astype(BF16))
                runs[h] = runs[h] + jnp.sum(per_head[h][1], axis=0, keepdims=True)
            ws.append(wb)
        accs = [carry[h][1] for h in range(heads)]
        for (j, _), wb in zip(blocks, ws):
            for h in range(heads):
                accs[h] = accs[h] + _dot(vt_ref[h, j], wb[h])
        return tuple((runs[h], accs[h]) for h in range(heads))

    def step(j, carry, diagonal):
        return sweep([(j, diagonal)], carry)

    init = tuple((jnp.zeros((1, t), F32), jnp.zeros((SB_HEAD_DIM, t), F32)) for _ in range(heads))
    carry = lax.cond(i > 0, lambda: sweep([(i, True), (i - 1, False)], init), lambda: sweep([(i, True)], init))

    def alive(c):
        return functools.reduce(jnp.maximum, [jnp.max(c[h][0]) for h in range(heads)]) >= SB_DEAD_LOG_WEIGHT

    def body(state):
        n, _, c = state
        c = step(i - 1 - n, c, False)
        return n + 1, alive(c), c

    _, _, carry = lax.while_loop(lambda s: jnp.logical_and(s[0] < i, s[1]), body, (1, alive(carry), carry))
    for h in range(heads):
        o_ref[:, head(h)] = (carry[h][1].T * _silu(g_ref[:, head(h)])).astype(o_ref.dtype)


def _sb_attn(p, batch, seq, t=ATTN_TILE, heads=SB_GROUP):
    nq = seq // t
    w = heads * LANES
    sec = SB_WIDTH // w
    tile = lambda s: pl.BlockSpec((t, w), lambda b, h, i: (b * nq + i, s * sec + h))
    full = lambda s: pl.BlockSpec((seq, w), lambda b, h, i: (b, s * sec + h))
    return pl.pallas_call(
        functools.partial(_sb_attn_kernel, t=t, heads=heads),
        out_shape=jax.ShapeDtypeStruct((batch * seq, SB_WIDTH), BF16),
        grid=(batch, SB_HEADS // heads, nq),
        in_specs=[tile(0), full(1), full(2), tile(3)],
        out_specs=pl.BlockSpec((t, w), lambda b, h, i: (b * nq + i, h)),
        scratch_shapes=[pltpu.VMEM((heads, nq, t, SB_HEAD_DIM), BF16),
                        pltpu.VMEM((heads, nq, SB_HEAD_DIM, t), BF16)],
        compiler_params=_params(("parallel", "parallel", "arbitrary")),
        name="sb_attn",
    )(p, p, p, p)


def _mla_prep_kernel(cq_ref, ckv_ref, sm_ref, cos_ref, sin_ref, gq_ref, gkv_ref, wuq_ref, wukv_ref,
                     gqn_ref, gqr_ref, gkn_ref, gkr_ref, q_ref, k_ref, v_ref):
    lane = lax.broadcasted_iota(jnp.int32, (1, LANES), 1)
    first = (lane % MLA_ROPE) < HALF_ROPE
    cos = cos_ref[...]
    sin = sin_ref[...]

    def latent_norm(c_ref, g_ref):
        c = c_ref[...]
        rs = lax.rsqrt(jnp.mean(c * c, axis=-1, keepdims=True) + EPS)
        return (c * rs * g_ref[...]).astype(BF16)

    def rotary(y):
        return y * cos + pltpu.roll(y, MLA_ROPE, axis=1) * sin

    def head_sums(sq):
        s_first = jnp.sum(jnp.where(first, sq, 0.0), axis=-1, keepdims=True)
        s_second = jnp.sum(jnp.where(first, 0.0, sq), axis=-1, keepdims=True)
        return s_first, s_second

    qfull = _dot(latent_norm(cq_ref, gq_ref), wuq_ref[...])
    kvfull = _dot(latent_norm(ckv_ref, gkv_ref), wukv_ref[...])
    v_ref[...] = kvfull[:, MLA_HEADS * MLA_NOPE:].astype(BF16)

    kr = sm_ref[...][:, :LANES]
    kr_ss, _ = head_sums(kr * kr)
    kr_rot = rotary(kr * gkr_ref[...])

    for pair in range(MLA_HEADS // 2):
        qr = qfull[:, MLA_HEADS * MLA_NOPE + pair * LANES:MLA_HEADS * MLA_NOPE + (pair + 1) * LANES]
        qr_ss = head_sums(qr * qr)
        q_rs, k_rs = [], []
        for e in range(2):
            h = 2 * pair + e
            qn = qfull[:, h * MLA_NOPE:(h + 1) * MLA_NOPE]
            kn = kvfull[:, h * MLA_NOPE:(h + 1) * MLA_NOPE]
            qs = lax.rsqrt((jnp.sum(qn * qn, axis=-1, keepdims=True) + qr_ss[e]) / MLA_QK + EPS)
            ks = lax.rsqrt((jnp.sum(kn * kn, axis=-1, keepdims=True) + kr_ss) / MLA_QK + EPS)
            q_rs.append(qs)
            k_rs.append(ks)
            q_ref[:, 2 * h * LANES:(2 * h + 1) * LANES] = (qn * qs * gqn_ref[...]).astype(BF16)
            k_ref[:, 2 * h * LANES:(2 * h + 1) * LANES] = (kn * ks * gkn_ref[...]).astype(BF16)
        q_rot = rotary(qr * jnp.where(first, q_rs[0], q_rs[1]) * gqr_ref[...])
        k_rot = (kr_rot * jnp.where(first, k_rs[0], k_rs[1])).astype(BF16)
        for e in range(2):
            h = 2 * pair + e
            own = first if e == 0 else jnp.logical_not(first)
            q_ref[:, (2 * h + 1) * LANES:(2 * h + 2) * LANES] = jnp.where(own, q_rot, 0.0).astype(BF16)
            k_ref[:, (2 * h + 1) * LANES:(2 * h + 2) * LANES] = k_rot


def _mla_prep(p, cos, sin, gq, gkv, wuq, wukv, gqn, gqr, gkn, gkr, seq, tm=512):
    m = p.shape[0]
    sblk = seq // tm
    row = lambda width, cb: pl.BlockSpec((tm, width), lambda i: (i, cb))
    const = lambda shape: pl.BlockSpec(shape, lambda i: (0, 0))
    pos = pl.BlockSpec((tm, LANES), lambda i: (i % sblk, 0))
    qk_width = MLA_HEADS * 2 * LANES
    return pl.pallas_call(
        _mla_prep_kernel,
        out_shape=(jax.ShapeDtypeStruct((m, qk_width), BF16),
                   jax.ShapeDtypeStruct((m, qk_width), BF16),
                   jax.ShapeDtypeStruct((m, MLA_WIDTH), BF16)),
        grid=(m // tm,),
        in_specs=[
            row(Q_LORA, SEC_MLA // Q_LORA),
            row(KV_LORA, SEC_MLA // KV_LORA + 1),
            row(256, SEC_SMALL // 256),
            pos, pos,
            const((1, Q_LORA)), const((1, KV_LORA)),
            const(wuq.shape), const(wukv.shape),
            const((1, LANES)), const((1, LANES)), const((1, LANES)), const((1, LANES)),
        ],
        out_specs=(row(qk_width, 0), row(qk_width, 0), row(MLA_WIDTH, 0)),
        compiler_params=_params(("parallel",)),
        name="mla_prep",
    )(p, p, p, cos, sin, gq, gkv, wuq, wukv, gqn, gqr, gkn, gkr)


def _mla_attn_kernel(q_ref, k_ref, v_ref, g_ref, o_ref, vt_ref, sa_ref, sb_ref, sd_ref, m_ref, acc_ref,
                     *, t, heads):
    i = pl.program_id(2)
    nblk = v_ref.shape[0] // t
    qw = 2 * LANES
    key_chunk = lax.broadcasted_iota(jnp.int32, (t, t), 0) // CHUNK
    qry_chunk = lax.broadcasted_iota(jnp.int32, (t, t), 1) // CHUNK
    visible = key_chunk <= qry_chunk
    qs = [q_ref[:, h * qw:(h + 1) * qw] for h in range(heads)]

    @pl.when(i == 0)
    def _():
        for h in range(heads):
            for blk in range(nblk):
                vb = v_ref[blk * t:(blk + 1) * t, h * LANES:(h + 1) * LANES]
                vt_ref[h, blk, :MLA_V] = vb.astype(F32).T.astype(BF16)
                vt_ref[h, blk, MLA_V:] = jnp.ones((MLA_SUM_ROWS, t), BF16)

    m_ref[...] = jnp.full(m_ref.shape, NEG_BIG, F32)
    acc_ref[...] = jnp.zeros(acc_ref.shape, F32)

    def issue_scores(j, buf):
        start = pl.multiple_of(j * t, t)
        for h in range(heads):
            buf[h] = _dot_nt(k_ref[pl.ds(start, t), h * qw:(h + 1) * qw], qs[h])

    def absorb(buf, j, diagonal):
        soft = []
        for h in range(heads):
            s = jnp.where(visible, buf[h], NEG_BIG) if diagonal else buf[h]
            m = m_ref[h]
            m_new = jnp.maximum(m, jnp.max(s, axis=0, keepdims=True))
            m_ref[h] = m_new
            soft.append((jnp.exp2(m - m_new), jnp.exp2(s - m_new).astype(BF16)))
        for h in range(heads):
            alpha, pr = soft[h]
            acc_ref[h] = alpha * acc_ref[h] + _dot(vt_ref[h, j], pr)

    issue_scores(i, sd_ref)
    issue_scores(0, sa_ref)
    last = jnp.maximum(i - 1, 0)

    def pair(n, _):
        issue_scores(2 * n + 1, sb_ref)
        absorb(sa_ref, 2 * n, False)
        issue_scores(jnp.minimum(2 * n + 2, last), sa_ref)
        absorb(sb_ref, 2 * n + 1, False)
        return 0

    lax.fori_loop(0, i // 2, pair, 0)

    @pl.when(i % 2 == 1)
    def _():
        absorb(sa_ref, i - 1, False)

    absorb(sd_ref, i, True)
    for h in range(heads):
        hs = slice(h * LANES, (h + 1) * LANES)
        acc = acc_ref[h]
        y = (acc[:MLA_V] / acc[MLA_V:MLA_V + 1]).T
        o_ref[:, hs] = (y * _silu(g_ref[:, hs])).astype(o_ref.dtype)


def _mla_attn(q, k, v, p, batch, seq, t=MLA_TILE, heads=MLA_GROUP):
    nq = seq // t
    qw = heads * 2 * LANES
    vw = heads * LANES
    gate0 = (SEC_MLA + Q_LORA + KV_LORA) // vw
    return pl.pallas_call(
        functools.partial(_mla_attn_kernel, t=t, heads=heads),
        out_shape=jax.ShapeDtypeStruct((batch * seq, MLA_WIDTH), BF16),
        grid=(batch, MLA_HEADS // heads, nq),
        in_specs=[
            pl.BlockSpec((t, qw), lambda b, h, i: (b * nq + i, h)),
            pl.BlockSpec((seq, qw), lambda b, h, i: (b, h)),
            pl.BlockSpec((seq, vw), lambda b, h, i: (b, h)),
            pl.BlockSpec((t, vw), lambda b, h, i: (b * nq + i, gate0 + h)),
        ],
        out_specs=pl.BlockSpec((t, vw), lambda b, h, i: (b * nq + i, h)),
        scratch_shapes=[pltpu.VMEM((heads, nq, MLA_V + MLA_SUM_ROWS, t), BF16)]
        + [pltpu.VMEM((heads, t, t), F32)] * 3
        + [pltpu.VMEM((heads, 1, t), F32)]
        + [pltpu.VMEM((heads, MLA_V + MLA_SUM_ROWS, t), F32)],
        compiler_params=_params(("parallel", "parallel", "arbitrary")),
        name="mla_attn",
    )(q, k, v, p)


def _rw_prep_kernel(cur_ref, prev_ref, smc_ref, smp_ref, mu_ref, mus_ref, wlora_ref, w0_ref, a0_ref,
                    kk_ref, ka_ref, ones_ref, r_ref, lw_ref, k_ref, v_ref, na_ref, b_ref,
                    *, tm, seq):
    i = pl.program_id(0)
    at_start = (i * tm) % seq == 0
    row = lax.broadcasted_iota(jnp.int32, (tm, 1), 0)

    def shifted(c_ref, p_ref, mu):
        cur = c_ref[...]
        last = jnp.where(at_start, 0.0, p_ref[...][7:8, :])
        prev = jnp.where(row == 0, last, pltpu.roll(cur, 1, axis=0))
        return cur + mu * (prev - cur)

    main = shifted(cur_ref, prev_ref, mu_ref[...])
    small = shifted(smc_ref, smp_ref, mus_ref[...])[:, LANES:]
    r = main[:, :RW_WIDTH]
    k = main[:, RW_WIDTH:2 * RW_WIDTH]
    v = main[:, 2 * RW_WIDTH:]

    lane = lax.broadcasted_iota(jnp.int32, (1, LANES), 1)
    lora_in = jnp.where(lane < DECAY_LORA, jnp.tanh(small), small).astype(BF16)
    lora = _dot(lora_in, wlora_ref[...])
    w_log = -_softplus(-(w0_ref[...] + lora[:, :RW_WIDTH])) - 0.5
    a = _sigmoid(a0_ref[...] + lora[:, RW_WIDTH:])

    kk = k * kk_ref[...]
    hi, lo = _split_bf16(kk * kk)
    ss = _dot(hi, ones_ref[...]) + _dot(lo, ones_ref[...])
    kk = kk / jnp.maximum(jnp.sqrt(ss), 1e-12)

    r_ref[...] = r
    lw_ref[...] = -jnp.exp(w_log)
    k_ref[...] = k * (1.0 + (a - 1.0) * ka_ref[...])
    v_ref[...] = v
    na_ref[...] = -kk
    b_ref[...] = kk * a


def _rw_prep(p, mu, mus, wlora, w0, a0, k_k, k_a, ones_bd, seq, tm=512):
    m = p.shape[0]
    vec = lambda width: pl.BlockSpec((1, width), lambda i: (0, 0))
    out = jax.ShapeDtypeStruct((m, RW_WIDTH), F32)
    main_w = 3 * RW_WIDTH
    prev_blk = lambda i: (jnp.maximum(i * (tm // 8) - 1, 0))
    return pl.pallas_call(
        functools.partial(_rw_prep_kernel, tm=tm, seq=seq),
        out_shape=(out,) * 6,
        grid=(m // tm,),
        in_specs=[
            pl.BlockSpec((tm, main_w), lambda i: (i, SEC_RW // main_w)),
            pl.BlockSpec((8, main_w), lambda i: (prev_blk(i), SEC_RW // main_w)),
            pl.BlockSpec((tm, 256), lambda i: (i, SEC_SMALL // 256)),
            pl.BlockSpec((8, 256), lambda i: (prev_blk(i), SEC_SMALL // 256)),
            vec(main_w), vec(256),
            pl.BlockSpec(wlora.shape, lambda i: (0, 0)),
            vec(RW_WIDTH), vec(RW_WIDTH), vec(RW_WIDTH), vec(RW_WIDTH),
            pl.BlockSpec((RW_WIDTH, RW_WIDTH), lambda i: (0, 0)),
        ],
        out_specs=(pl.BlockSpec((tm, RW_WIDTH), lambda i: (i, 0)),) * 6,
        compiler_params=_params(("parallel",)),
        name="rw_prep",
    )(p, p, p, p, mu, mus, wlora, w0, a0, k_k, k_a, ones_bd)


_HI = lax.Precision.HIGHEST

RW_MODE_SCORE = "b1"
RW_MODE_INV = "b1"
RW_MODE_MID = "b1"
RW_MODE_STATE = "b1"


def _mm(a, b, mode, form="nn"):
    f = {"nn": _dot, "nt": _dot_nt, "tn": _dot_tn}[form]
    if mode == "hi":
        return f(a, b, _HI)
    if mode == "b1":
        return f(a.astype(BF16), b.astype(BF16))
    a_hi, a_lo = _split_bf16(a)
    b_hi, b_lo = _split_bf16(b)
    return f(a_hi, b_hi) + (f(a_hi, b_lo) + f(a_lo, b_hi))


def _rw_chunks(probs):
    c = RW_CHUNK
    n = 2 * c
    lane = lax.broadcasted_iota(jnp.int32, (1, LANES), 1)
    first = lane < RW_HEAD_DIM
    ti = lax.broadcasted_iota(jnp.int32, (c, c), 0)
    si = lax.broadcasted_iota(jnp.int32, (c, c), 1)
    tri = (si <= ti).astype(BF16)
    ri = lax.broadcasted_iota(jnp.int32, (n, n), 0)
    ci = lax.broadcasted_iota(jnp.int32, (n, n), 1)
    strict = (ci % c) < (ri % c)
    incl = (ci % c) <= (ri % c)
    eye = ri == ci

    def stack(x):
        return jnp.concatenate([jnp.where(first, x, 0.0), jnp.where(first, 0.0, x)], axis=0)

    splits = [_split_bf16(lw) for _, lw, _, _, _, _ in probs]
    cums = [_dot(tri, h) + _dot(tri, l) for h, l in splits]

    feats = []
    for (r, lw, k, v, na, b), cum in zip(probs, cums):
        total = cum[c - 1:c, :]
        p_inv = jnp.exp(-cum)
        p_end = jnp.exp(total - cum)
        at = stack(na * jnp.exp(cum - lw))
        rt = stack(r * jnp.exp(cum))
        lhs = jnp.concatenate([at, rt], axis=0)
        rhs = jnp.concatenate([stack(b * p_inv), stack(k * p_inv)], axis=0)
        feats.append(dict(at=at, rt=rt, lhs=lhs, rhs=rhs, bh=stack(b * p_end), kh=stack(k * p_end),
                          vs=stack(v), decay=jnp.exp(total)))

    gs = [_mm(f["lhs"], f["rhs"], RW_MODE_SCORE, "nt") for f in feats]
    l_ab = [jnp.where(strict, g[:n, :n], 0.0) for g in gs]
    l_ak = [jnp.where(strict, g[:n, n:], 0.0) for g in gs]
    m_rb = [jnp.where(incl, g[n:, :n], 0.0) for g in gs]
    m_rk = [jnp.where(incl, g[n:, n:], 0.0) for g in gs]

    xs = l_ab
    tinv = [jnp.where(eye, 1.0, 0.0) + x for x in xs]
    for _ in range(int(math.log2(c)) - 1):
        xs = [_mm(x, x, RW_MODE_INV) for x in xs]
        tinv = [t + _mm(t, x, RW_MODE_INV) for t, x in zip(tinv, xs)]

    lv = [_mm(l, f["vs"], RW_MODE_MID) for l, f in zip(l_ak, feats)]
    wu = [_mm(t, jnp.concatenate([f["at"], y], axis=1), RW_MODE_MID)
          for t, f, y in zip(tinv, feats, lv)]
    ro = [_mm(m, w, RW_MODE_MID) for m, w in zip(m_rb, wu)]
    rv = [_mm(m, f["vs"], RW_MODE_MID) for m, f in zip(m_rk, feats)]
    ab = [_mm(w, f["bh"], RW_MODE_STATE, "tn") for w, f in zip(wu, feats)]
    vk = [_mm(f["vs"], f["kh"], RW_MODE_STATE, "tn") for f in feats]

    terms = []
    for i, f in enumerate(feats):
        rh = f["rt"] + ro[i][:, :LANES]
        o0 = ro[i][:, LANES:] + rv[i]
        a_mat = jnp.where(eye, f["decay"], 0.0) + ab[i][:LANES]
        b_mat = ab[i][LANES:] + vk[i]
        terms.append((rh[:c] + rh[c:], o0[:c] + o0[c:], a_mat, b_mat))
    return terms


def _rw_scan_kernel(r_ref, lw_ref, k_ref, v_ref, na_ref, b_ref, g_ref, rk_ref, gng_ref, gnb_ref,
                    o_ref, s_ref):
    @pl.when(pl.program_id(0) == 0)
    def _():
        s_ref[...] = jnp.zeros_like(s_ref)

    lane = lax.broadcasted_iota(jnp.int32, (1, LANES), 1)
    first = lane < RW_HEAD_DIM
    batch = r_ref.shape[0]
    pairs = RW_WIDTH // LANES
    where = [(bi, slice(pr * LANES, (pr + 1) * LANES)) for bi in range(batch) for pr in range(pairs)]
    rows = [slice(q * RW_CHUNK, (q + 1) * RW_CHUNK) for q in range(RW_CHUNKS_PER_STEP)]

    def head_sum(y):
        s_first = jnp.sum(jnp.where(first, y, 0.0), axis=-1, keepdims=True)
        s_second = jnp.sum(jnp.where(first, 0.0, y), axis=-1, keepdims=True)
        return jnp.where(first, s_first, s_second)

    probs = [(r_ref[bi, rs, sl], lw_ref[bi, rs, sl], k_ref[bi, rs, sl], v_ref[bi, rs, sl], na_ref[bi, rs, sl],
              b_ref[bi, rs, sl]) for bi, sl in where for rs in rows]
    terms = _rw_chunks(probs)
    for i, (bi, sl) in enumerate(where):
        state = s_ref[i]
        for q, rs in enumerate(rows):
            n = i * RW_CHUNKS_PER_STEP + q
            rhat, o0, a_mat, b_mat = terms[n]
            o = _mm(rhat, state, RW_MODE_STATE, "nt") + o0
            state = _mm(state, a_mat, RW_MODE_STATE) + b_mat
            r, _, k, v, _, _ = probs[n]
            mu = head_sum(o) / RW_HEAD_DIM
            d = o - mu
            var = head_sum(d * d) / RW_HEAD_DIM
            normed = d * lax.rsqrt(var + RW_GN_EPS) * gng_ref[:, sl] + gnb_ref[:, sl]
            bonus = head_sum(r * k * rk_ref[:, sl]) * v
            o_ref[bi, rs, sl] = ((normed + bonus) * _silu(g_ref[bi, rs, sl])).astype(o_ref.dtype)
        s_ref[i] = state


def _rw_scan(r, lw, k, v, na, b, p, r_k, gn_g, gn_b, batch, seq):
    c = RW_CHUNK * RW_CHUNKS_PER_STEP
    as3d = lambda t: t.reshape(batch, seq, t.shape[-1])
    tok = pl.BlockSpec((batch, c, RW_WIDTH), lambda ci: (0, ci, 0))
    vec = pl.BlockSpec((1, RW_WIDTH), lambda ci: (0, 0))
    gate = pl.BlockSpec((batch, c, RW_WIDTH), lambda ci: (0, ci, SEC_RG // RW_WIDTH))
    out = pl.pallas_call(
        _rw_scan_kernel,
        out_shape=jax.ShapeDtypeStruct((batch, seq, RW_WIDTH), BF16),
        grid=(seq // c,),
        in_specs=[tok] * 6 + [gate, vec, vec, vec],
        out_specs=tok,
        scratch_shapes=[pltpu.VMEM((batch * RW_WIDTH // LANES, LANES, LANES), F32)],
        compiler_params=_params(("arbitrary",)),
        name="rw_scan",
    )(*[as3d(t) for t in (r, lw, k, v, na, b, p)], r_k, gn_g, gn_b)
    return out.reshape(batch * seq, RW_WIDTH)


def _rw_branch(p, mu, w0, w_up, a0, a_up, k_k, k_a, r_k, gn_g, gn_b, batch, seq):
    row = lambda t: t[None, :]
    head_ones = jnp.kron(jnp.eye(RW_HEADS, dtype=F32), jnp.ones((RW_HEAD_DIM, RW_HEAD_DIM), F32)).astype(BF16)
    mu_small = jnp.concatenate([jnp.zeros((LANES,), F32), mu[3 * RW_WIDTH:]])
    zeros = jnp.zeros((DECAY_LORA, RW_WIDTH), F32)
    wlora = jnp.concatenate([jnp.concatenate([w_up, zeros], axis=1),
                             jnp.concatenate([zeros, a_up], axis=1)], axis=0).astype(BF16)
    rr, lw, rk, rv, na, rb = _rw_prep(p, row(mu[:3 * RW_WIDTH]), row(mu_small), wlora, row(w0), row(a0),
                                      row(k_k), row(k_a), head_ones, seq)
    return _rw_scan(rr, lw, rk, rv, na, rb, p, r_k.reshape(1, RW_WIDTH), row(gn_g), row(gn_b), batch, seq)


def _merge_out_kernel(x_ref, ysb_ref, ymla_ref, yrw_ref, wsb_ref, wmla_ref, wrw_ref, g1_ref, g2_ref, g3_ref,
                      wout_ref, o_ref):
    merged = _sigmoid(g1_ref[...]) * _dot(ysb_ref[...], wsb_ref[...])
    merged = merged + _sigmoid(g2_ref[...]) * _dot(ymla_ref[...], wmla_ref[...])
    merged = merged + _sigmoid(g3_ref[...]) * _dot(yrw_ref[...], wrw_ref[...])
    o_ref[...] = x_ref[...] + _dot(merged.astype(BF16), wout_ref[...])


def _merge_out(x2, y_sb, y_mla, y_rw, w_sb, w_mla, w_rw, p, w_out, tm=256):
    m = p.shape[0]
    rows = lambda width: pl.BlockSpec((tm, width), lambda i: (i, 0))
    resident = lambda depth: pl.BlockSpec((depth, D_MODEL), lambda i: (0, 0), pipeline_mode=pl.Buffered(1))
    gate = lambda br: pl.BlockSpec((tm, D_MODEL), lambda i: (i, SEC_GATE // D_MODEL + br))
    return pl.pallas_call(
        _merge_out_kernel,
        out_shape=jax.ShapeDtypeStruct((m, D_MODEL), F32),
        grid=(m // tm,),
        in_specs=[rows(D_MODEL), rows(SB_WIDTH), rows(MLA_WIDTH), rows(RW_WIDTH),
                  resident(SB_WIDTH), resident(MLA_WIDTH), resident(RW_WIDTH),
                  gate(0), gate(1), gate(2), resident(D_MODEL)],
        out_specs=rows(D_MODEL),
        compiler_params=_params(("parallel",)),
        name="merge_out",
    )(x2, y_sb, y_mla, y_rw, w_sb, w_mla, w_rw, p, p, p, w_out)


def _pair_rope_cols(t):
    lead = t.shape[:-2]
    t = t.reshape(lead + (MLA_HEADS // 2, 2, 2, HALF_ROPE))
    t = jnp.swapaxes(t, -3, -2)
    return t.reshape(lead + (MLA_HEADS // 2 * LANES,))


def _pair_rope_gain(g):
    g1, g2 = g[:HALF_ROPE], g[HALF_ROPE:]
    return jnp.concatenate([g1, g1, g2, g2])[None, :]


def _rope_tables(seq):
    freqs = ROPE_THETA ** (-jnp.arange(HALF_ROPE, dtype=F32) / HALF_ROPE)
    ang = jnp.arange(seq, dtype=F32)[:, None] * freqs[None, :]
    c, s = jnp.cos(ang), jnp.sin(ang)
    return jnp.concatenate([c, c, c, c], axis=1), jnp.concatenate([-s, -s, s, s], axis=1)


def _mla_branch(p, cos, sin, q_norm_g, kv_norm_g, w_uq, w_ukv, qn_g, kn_g, batch, seq):
    row = lambda t: t[None, :]
    uq = w_uq.reshape(Q_LORA, MLA_HEADS, MLA_QK)
    wuq = jnp.concatenate([uq[:, :, :MLA_NOPE].reshape(Q_LORA, -1), _pair_rope_cols(uq[:, :, MLA_NOPE:])],
                          axis=1).astype(BF16)
    ukv = w_ukv.reshape(KV_LORA, MLA_HEADS, MLA_NOPE + MLA_V)
    wukv = jnp.concatenate([ukv[:, :, :MLA_NOPE].reshape(KV_LORA, -1),
                            ukv[:, :, MLA_NOPE:].reshape(KV_LORA, -1)], axis=1).astype(BF16)
    q, k, v = _mla_prep(p, cos, sin, row(q_norm_g), row(kv_norm_g), wuq, wukv,
                        row(qn_g[:MLA_NOPE]) * MLA_EXP2_SCALE, _pair_rope_gain(qn_g[MLA_NOPE:]) * MLA_EXP2_SCALE,
                        row(kn_g[:MLA_NOPE]), _pair_rope_gain(kn_g[MLA_NOPE:]), seq)
    return _mla_attn(q, k, v, p, batch, seq)


def kernel(x, norm_g, w_in, mla_q_norm_g, mla_kv_norm_g, mla_w_uq, mla_w_ukv, mla_qn_g, mla_kn_g,
           rw_mu, rw_w0, rw_w_up, rw_a0, rw_a_up, rw_k_k, rw_k_a, rw_r_k, rw_gn_g, rw_gn_b,
           w_br_sb, w_br_mla, w_br_rw, w_out):
    batch, seq, _ = x.shape
    depth = w_in.shape[0]
    x2 = x.reshape(batch * seq, D_MODEL)
    cos, sin = _rope_tables(seq)
    w_in_bf16 = _w_in_relayout(w_in)

    for l in range(depth):
        p = _in_proj(x2, norm_g[l][None, :], w_in_bf16, l)
        y_sb = _sb_attn(p, batch, seq)
        y_mla = _mla_branch(p, cos, sin, mla_q_norm_g[l], mla_kv_norm_g[l], mla_w_uq[l], mla_w_ukv[l],
                            mla_qn_g[l], mla_kn_g[l], batch, seq)
        y_rw = _rw_branch(p, rw_mu[l], rw_w0[l], rw_w_up[l], rw_a0[l], rw_a_up[l], rw_k_k[l], rw_k_a[l],
                          rw_r_k[l], rw_gn_g[l], rw_gn_b[l], batch, seq)
        x2 = _merge_out(x2, y_sb, y_mla, y_rw, w_br_sb[l].astype(BF16), w_br_mla[l].astype(BF16),
                        w_br_rw[l].astype(BF16), p, w_out[l].astype(BF16))

    return x2.reshape(batch, seq, D_MODEL)
```

```python
import functools
import math

import jax
import jax.numpy as jnp
from jax import lax
from jax.experimental import pallas as pl
from jax.experimental.pallas import tpu as pltpu

F32 = jnp.float32
BF16 = jnp.bfloat16

D_MODEL = 2048
EPS = 1e-6
CHUNK = 64

SB_HEADS = 4
SB_HEAD_DIM = 128
SB_WIDTH = SB_HEADS * SB_HEAD_DIM

MLA_HEADS = 8
MLA_NOPE = 128
MLA_ROPE = 64
MLA_QK = MLA_NOPE + MLA_ROPE
MLA_V = 128
MLA_WIDTH = MLA_HEADS * MLA_V
Q_LORA = 512
KV_LORA = 512
ROPE_THETA = 10000.0
HALF_ROPE = MLA_ROPE // 2

RW_HEADS = 8
RW_HEAD_DIM = 64
RW_WIDTH = RW_HEADS * RW_HEAD_DIM
DECAY_LORA = 64
ICL_LORA = 64
RW_GN_EPS = 64e-5
RW_CHUNK = 64
RW_CHUNKS_PER_STEP = 2

LANES = 128
VMEM_LIMIT = 56 * 1024 * 1024

SEC_SB = 0
SEC_MLA = 2048
SEC_RG = 4096
SEC_RW = 4608
SEC_GATE = 6144
SEC_SMALL = 12288
N_PROJ = SEC_SMALL + 256

_ORIG_KR = 4 * SB_WIDTH + Q_LORA + KV_LORA
_ORIG_MG = _ORIG_KR + MLA_ROPE
_ORIG_RW = _ORIG_MG + MLA_WIDTH
_ORIG_LORA = _ORIG_RW + 3 * RW_WIDTH
_ORIG_RG = _ORIG_LORA + DECAY_LORA + ICL_LORA
_ORIG_GATE = _ORIG_RG + RW_WIDTH
W_IN_MOVES = (
    (0, 0, _ORIG_KR),
    (SEC_MLA + Q_LORA + KV_LORA, _ORIG_MG, MLA_WIDTH),
    (SEC_RG, _ORIG_RG, RW_WIDTH),
    (SEC_RW, _ORIG_RW, 3 * RW_WIDTH),
    (SEC_GATE, _ORIG_GATE, 3 * D_MODEL),
    (SEC_SMALL, _ORIG_KR, HALF_ROPE),
    (SEC_SMALL + HALF_ROPE, _ORIG_KR, HALF_ROPE),
    (SEC_SMALL + 2 * HALF_ROPE, _ORIG_KR + HALF_ROPE, HALF_ROPE),
    (SEC_SMALL + 3 * HALF_ROPE, _ORIG_KR + HALF_ROPE, HALF_ROPE),
    (SEC_SMALL + LANES, _ORIG_LORA, DECAY_LORA + ICL_LORA),
)

NEG_BIG = -1e30

SB_DEAD_LOG_WEIGHT = -104.0

ATTN_TILE = 256
MLA_TILE = 512
MLA_EXP2_SCALE = (MLA_QK ** -0.5) * math.log2(math.e)
MLA_SUM_ROWS = 16
SB_GROUP = 4
MLA_GROUP = 4


def _dot(a, b, precision=None):
    return jnp.dot(a, b, preferred_element_type=F32, precision=precision)


def _dot_nt(a, b, precision=None):
    return lax.dot_general(a, b, (((1,), (1,)), ((), ())), preferred_element_type=F32,
                           precision=precision)


def _dot_tn(a, b, precision=None):
    return lax.dot_general(a, b, (((0,), (0,)), ((), ())), preferred_element_type=F32,
                           precision=precision)


def _split_bf16(x):
    hi = x.astype(BF16)
    lo = (x - hi.astype(F32)).astype(BF16)
    return hi, lo


def _softplus(z):
    return jnp.maximum(z, 0.0) + jnp.log(1.0 + jnp.exp(-jnp.abs(z)))


def _sigmoid(z):
    return 1.0 / (1.0 + jnp.exp(-z))


def _silu(z):
    return z * _sigmoid(z)


def _params(semantics):
    return pltpu.CompilerParams(dimension_semantics=semantics, vmem_limit_bytes=VMEM_LIMIT)


def _in_proj_kernel(x_ref, g_ref, w_ref, o_ref, h_ref):
    @pl.when(pl.program_id(1) == 0)
    def _():
        x = x_ref[...]
        rs = lax.rsqrt(jnp.mean(x * x, axis=-1, keepdims=True) + EPS)
        h_ref[...] = (x * rs * g_ref[...]).astype(BF16)

    o_ref[...] = _dot(h_ref[...], w_ref[...])


def _w_in_relayout_kernel(wt_ref, o_ref):
    for dst, src, width in W_IN_MOVES:
        for c in range(0, width, LANES):
            cw = min(LANES, width - c)
            o_ref[:, dst + c:dst + c + cw] = wt_ref[src + c:src + c + cw, :].T.astype(BF16)


def _w_in_relayout(w_in, tk=LANES):
    depth, _, n_in = w_in.shape
    return pl.pallas_call(
        _w_in_relayout_kernel,
        out_shape=jax.ShapeDtypeStruct((depth, D_MODEL, N_PROJ), BF16),
        grid=(depth, D_MODEL // tk),
        in_specs=[pl.BlockSpec((None, n_in, tk), lambda l, i: (l, 0, i))],
        out_specs=pl.BlockSpec((None, tk, N_PROJ), lambda l, i: (l, i, 0)),
        compiler_params=_params(("parallel", "parallel")),
        name="w_in_relayout",
    )(jnp.swapaxes(w_in, 1, 2))


def _in_proj(x2, g, w, layer, tm=1024, tn=1792):
    m = x2.shape[0]
    return pl.pallas_call(
        _in_proj_kernel,
        out_shape=jax.ShapeDtypeStruct((m, N_PROJ), F32),
        grid=(m // tm, N_PROJ // tn),
        in_specs=[
            pl.BlockSpec((tm, D_MODEL), lambda i, j: (i, 0), pipeline_mode=pl.Buffered(1)),
            pl.BlockSpec((1, D_MODEL), lambda i, j: (0, 0)),
            pl.BlockSpec((None, D_MODEL, tn), lambda i, j: (layer, 0, j)),
        ],
        out_specs=pl.BlockSpec((tm, tn), lambda i, j: (i, j)),
        scratch_shapes=[pltpu.VMEM((tm, D_MODEL), BF16)],
        compiler_params=_params(("parallel", "arbitrary")),
        name="in_proj",
    )(x2, g, w)


def _sb_attn_kernel(q_ref, k_ref, v_ref, g_ref, o_ref, kb_ref, vt_ref, *, t, heads):
    i = pl.program_id(2)
    nblk = k_ref.shape[0] // t
    scale = SB_HEAD_DIM ** -0.5
    key = lax.broadcasted_iota(jnp.int32, (t, t), 0)
    qry = lax.broadcasted_iota(jnp.int32, (t, t), 1)
    later = (qry > key).astype(BF16)
    causal = key < qry
    head = lambda h: slice(h * LANES, (h + 1) * LANES)
    qs = [q_ref[:, head(h)].astype(BF16) for h in range(heads)]

    @pl.when(i == 0)
    def _():
        for h in range(heads):
            for blk in range(nblk):
                rows = slice(blk * t, (blk + 1) * t)
                kb_ref[h, blk] = k_ref[rows, head(h)].astype(BF16)
                vt_ref[h, blk] = v_ref[rows, head(h)].T.astype(BF16)

    def sweep(blocks, carry):
        zs = [[_dot_nt(kb_ref[h, j], qs[h]) for h in range(heads)] for j, _ in blocks]
        stage = []
        for (j, diagonal), zb in zip(blocks, zs):
            per_head = []
            for h in range(heads):
                z = zb[h] * scale
                sp = _softplus(z)
                log_fail = jnp.where(causal, -sp, 0.0) if diagonal else -sp
                hi, lo = _split_bf16(log_fail)
                per_head.append((z - sp, log_fail, hi, lo))
            stage.append(per_head)
        sums = [[_dot(later, hi) + _dot(later, lo) for _, _, hi, lo in per_head] for per_head in stage]
        runs = [carry[h][0] for h in range(heads)]
        ws = []
        for (j, diagonal), per_head, sums_b in zip(blocks, stage, sums):
            wb = []
            for h in range(heads):
                w = jnp.exp(per_head[h][0] + sums_b[h] + runs[h])
                if diagonal:
                    w = jnp.where(causal, w, 0.0)
                wb.append(w.astype(BF16))
                runs[h] = runs[h] + jnp.sum(per_head[h][1], axis=0, keepdims=True)
            ws.append(wb)
        accs = [carry[h][1] for h in range(heads)]
        for (j, _), wb in zip(blocks, ws):
            for h in range(heads):
                accs[h] = accs[h] + _dot(vt_ref[h, j], wb[h])
        return tuple((runs[h], accs[h]) for h in range(heads))

    def step(j, carry, diagonal):
        return sweep([(j, diagonal)], carry)

    init = tuple((jnp.zeros((1, t), F32), jnp.zeros((SB_HEAD_DIM, t), F32)) for _ in range(heads))
    carry = lax.cond(i > 0, lambda: sweep([(i, True), (i - 1, False)], init), lambda: sweep([(i, True)], init))

    def alive(c):
        return functools.reduce(jnp.maximum, [jnp.max(c[h][0]) for h in range(heads)]) >= SB_DEAD_LOG_WEIGHT

    def body(state):
        n, _, c = state
        c = step(i - 1 - n, c, False)
        return n + 1, alive(c), c

    _, _, carry = lax.while_loop(lambda s: jnp.logical_and(s[0] < i, s[1]), body, (1, alive(carry), carry))
    for h in range(heads):
        o_ref[:, head(h)] = (carry[h][1].T * _silu(g_ref[:, head(h)])).astype(o_ref.dtype)


def _sb_attn(p, batch, seq, t=ATTN_TILE, heads=SB_GROUP):
    nq = seq // t
    w = heads * LANES
    sec = SB_WIDTH // w
    tile = lambda s: pl.BlockSpec((t, w), lambda b, h, i: (b * nq + i, s * sec + h))
    full = lambda s: pl.BlockSpec((seq, w), lambda b, h, i: (b, s * sec + h))
    return pl.pallas_call(
        functools.partial(_sb_attn_kernel, t=t, heads=heads),
        out_shape=jax.ShapeDtypeStruct((batch * seq, SB_WIDTH), BF16),
        grid=(batch, SB_HEADS // heads, nq),
        in_specs=[tile(0), full(1), full(2), tile(3)],
        out_specs=pl.BlockSpec((t, w), lambda b, h, i: (b * nq + i, h)),
        scratch_shapes=[pltpu.VMEM((heads, nq, t, SB_HEAD_DIM), BF16),
                        pltpu.VMEM((heads, nq, SB_HEAD_DIM, t), BF16)],
        compiler_params=_params(("parallel", "parallel", "arbitrary")),
        name="sb_attn",
    )(p, p, p, p)


def _mla_prep_kernel(cq_ref, ckv_ref, sm_ref, cos_ref, sin_ref, gq_ref, gkv_ref, wuq_ref, wukv_ref,
                     gqn_ref, gqr_ref, gkn_ref, gkr_ref, q_ref, k_ref, v_ref):
    lane = lax.broadcasted_iota(jnp.int32, (1, LANES), 1)
    first = (lane % MLA_ROPE) < HALF_ROPE
    cos = cos_ref[...]
    sin = sin_ref[...]

    def latent_norm(c_ref, g_ref):
        c = c_ref[...]
        rs = lax.rsqrt(jnp.mean(c * c, axis=-1, keepdims=True) + EPS)
        return (c * rs * g_ref[...]).astype(BF16)

    def rotary(y):
        return y * cos + pltpu.roll(y, MLA_ROPE, axis=1) * sin

    def head_sums(sq):
        s_first = jnp.sum(jnp.where(first, sq, 0.0), axis=-1, keepdims=True)
        s_second = jnp.sum(jnp.where(first, 0.0, sq), axis=-1, keepdims=True)
        return s_first, s_second

    qfull = _dot(latent_norm(cq_ref, gq_ref), wuq_ref[...])
    kvfull = _dot(latent_norm(ckv_ref, gkv_ref), wukv_ref[...])
    v_ref[...] = kvfull[:, MLA_HEADS * MLA_NOPE:].astype(BF16)

    kr = sm_ref[...][:, :LANES]
    kr_ss, _ = head_sums(kr * kr)
    kr_rot = rotary(kr * gkr_ref[...])

    for pair in range(MLA_HEADS // 2):
        qr = qfull[:, MLA_HEADS * MLA_NOPE + pair * LANES:MLA_HEADS * MLA_NOPE + (pair + 1) * LANES]
        qr_ss = head_sums(qr * qr)
        q_rs, k_rs = [], []
        for e in range(2):
            h = 2 * pair + e
            qn = qfull[:, h * MLA_NOPE:(h + 1) * MLA_NOPE]
            kn = kvfull[:, h * MLA_NOPE:(h + 1) * MLA_NOPE]
            qs = lax.rsqrt((jnp.sum(qn * qn, axis=-1, keepdims=True) + qr_ss[e]) / MLA_QK + EPS)
            ks = lax.rsqrt((jnp.sum(kn * kn, axis=-1, keepdims=True) + kr_ss) / MLA_QK + EPS)
            q_rs.append(qs)
            k_rs.append(ks)
            q_ref[:, 2 * h * LANES:(2 * h + 1) * LANES] = (qn * qs * gqn_ref[...]).astype(BF16)
            k_ref[:, 2 * h * LANES:(2 * h + 1) * LANES] = (kn * ks * gkn_ref[...]).astype(BF16)
        q_rot = rotary(qr * jnp.where(first, q_rs[0], q_rs[1]) * gqr_ref[...])
        k_rot = (kr_rot * jnp.where(first, k_rs[0], k_rs[1])).astype(BF16)
        for e in range(2):
            h = 2 * pair + e
            own = first if e == 0 else jnp.logical_not(first)
            q_ref[:, (2 * h + 1) * LANES:(2 * h + 2) * LANES] = jnp.where(own, q_rot, 0.0).astype(BF16)
            k_ref[:, (2 * h + 1) * LANES:(2 * h + 2) * LANES] = k_rot


def _mla_prep(p, cos, sin, gq, gkv, wuq, wukv, gqn, gqr, gkn, gkr, seq, tm=512):
    m = p.shape[0]
    sblk = seq // tm
    row = lambda width, cb: pl.BlockSpec((tm, width), lambda i: (i, cb))
    const = lambda shape: pl.BlockSpec(shape, lambda i: (0, 0))
    pos = pl.BlockSpec((tm, LANES), lambda i: (i % sblk, 0))
    qk_width = MLA_HEADS * 2 * LANES
    return pl.pallas_call(
        _mla_prep_kernel,
        out_shape=(jax.ShapeDtypeStruct((m, qk_width), BF16),
                   jax.ShapeDtypeStruct((m, qk_width), BF16),
                   jax.ShapeDtypeStruct((m, MLA_WIDTH), BF16)),
        grid=(m // tm,),
        in_specs=[
            row(Q_LORA, SEC_MLA // Q_LORA),
            row(KV_LORA, SEC_MLA // KV_LORA + 1),
            row(256, SEC_SMALL // 256),
            pos, pos,
            const((1, Q_LORA)), const((1, KV_LORA)),
            const(wuq.shape), const(wukv.shape),
            const((1, LANES)), const((1, LANES)), const((1, LANES)), const((1, LANES)),
        ],
        out_specs=(row(qk_width, 0), row(qk_width, 0), row(MLA_WIDTH, 0)),
        compiler_params=_params(("parallel",)),
        name="mla_prep",
    )(p, p, p, cos, sin, gq, gkv, wuq, wukv, gqn, gqr, gkn, gkr)


def _mla_attn_kernel(q_ref, k_ref, v_ref, g_ref, o_ref, vt_ref, sa_ref, sb_ref, sd_ref, m_ref, acc_ref,
                     *, t, heads):
    i = pl.program_id(2)
    nblk = v_ref.shape[0] // t
    qw = 2 * LANES
    key_chunk = lax.broadcasted_iota(jnp.int32, (t, t), 0) // CHUNK
    qry_chunk = lax.broadcasted_iota(jnp.int32, (t, t), 1) // CHUNK
    visible = key_chunk <= qry_chunk
    qs = [q_ref[:, h * qw:(h + 1) * qw] for h in range(heads)]

    @pl.when(i == 0)
    def _():
        for h in range(heads):
            for blk in range(nblk):
                vb = v_ref[blk * t:(blk + 1) * t, h * LANES:(h + 1) * LANES]
                vt_ref[h, blk, :MLA_V] = vb.astype(F32).T.astype(BF16)
                vt_ref[h, blk, MLA_V:] = jnp.ones((MLA_SUM_ROWS, t), BF16)

    def issue_scores(j, buf):
        start = pl.multiple_of(j * t, t)
        for h in range(heads):
            buf[h] = _dot_nt(k_ref[pl.ds(start, t), h * qw:(h + 1) * qw], qs[h])

    def absorb_diagonal(buf, j):
        weights = []
        for h in range(heads):
            s = jnp.where(visible, buf[h], NEG_BIG)
            m_new = jnp.max(s, axis=0, keepdims=True)
            m_ref[h] = m_new
            weights.append(jnp.exp2(s - m_new).astype(BF16))
        for h in range(heads):
            acc_ref[h] = _dot(vt_ref[h, j], weights[h])

    def absorb(buf, j):
        soft = []
        for h in range(heads):
            s = buf[h]
            m = m_ref[h]
            m_new = jnp.maximum(m, jnp.max(s, axis=0, keepdims=True))
            m_ref[h] = m_new
            soft.append((jnp.exp2(m - m_new), jnp.exp2(s - m_new).astype(BF16)))
        for h in range(heads):
            alpha, pr = soft[h]
            acc_ref[h] = alpha * acc_ref[h] + _dot(vt_ref[h, j], pr)

    issue_scores(i, sd_ref)
    issue_scores(0, sa_ref)
    absorb_diagonal(sd_ref, i)
    last = jnp.maximum(i - 1, 0)

    def pair(n, _):
        issue_scores(2 * n + 1, sb_ref)
        absorb(sa_ref, 2 * n)
        issue_scores(jnp.minimum(2 * n + 2, last), sa_ref)
        absorb(sb_ref, 2 * n + 1)
        return 0

    lax.fori_loop(0, i // 2, pair, 0)

    @pl.when(i % 2 == 1)
    def _():
        absorb(sa_ref, i - 1)

    for h in range(heads):
        hs = slice(h * LANES, (h + 1) * LANES)
        acc = acc_ref[h]
        y = (acc[:MLA_V] / acc[MLA_V:MLA_V + 1]).T
        o_ref[:, hs] = (y * _silu(g_ref[:, hs])).astype(o_ref.dtype)


def _mla_attn(q, k, v, p, batch, seq, t=MLA_TILE, heads=MLA_GROUP):
    nq = seq // t
    qw = heads * 2 * LANES
    vw = heads * LANES
    gate0 = (SEC_MLA + Q_LORA + KV_LORA) // vw
    return pl.pallas_call(
        functools.partial(_mla_attn_kernel, t=t, heads=heads),
        out_shape=jax.ShapeDtypeStruct((batch * seq, MLA_WIDTH), BF16),
        grid=(batch, MLA_HEADS // heads, nq),
        in_specs=[
            pl.BlockSpec((t, qw), lambda b, h, i: (b * nq + i, h)),
            pl.BlockSpec((seq, qw), lambda b, h, i: (b, h)),
            pl.BlockSpec((seq, vw), lambda b, h, i: (b, h)),
            pl.BlockSpec((t, vw), lambda b, h, i: (b * nq + i, gate0 + h)),
        ],
        out_specs=pl.BlockSpec((t, vw), lambda b, h, i: (b * nq + i, h)),
        scratch_shapes=[pltpu.VMEM((heads, nq, MLA_V + MLA_SUM_ROWS, t), BF16)]
        + [pltpu.VMEM((heads, t, t), F32)] * 3
        + [pltpu.VMEM((heads, 1, t), F32)]
        + [pltpu.VMEM((heads, MLA_V + MLA_SUM_ROWS, t), F32)],
        compiler_params=_params(("parallel", "parallel", "arbitrary")),
        name="mla_attn",
    )(q, k, v, p)


def _rw_prep_kernel(cur_ref, prev_ref, smc_ref, smp_ref, mu_ref, mus_ref, wlora_ref, w0_ref, a0_ref,
                    kk_ref, ka_ref, ones_ref, r_ref, lw_ref, k_ref, v_ref, na_ref, b_ref,
                    *, tm, seq):
    i = pl.program_id(0)
    at_start = (i * tm) % seq == 0
    row = lax.broadcasted_iota(jnp.int32, (tm, 1), 0)

    def shifted(c_ref, p_ref, mu):
        cur = c_ref[...]
        last = jnp.where(at_start, 0.0, p_ref[...][7:8, :])
        prev = jnp.where(row == 0, last, pltpu.roll(cur, 1, axis=0))
        return cur + mu * (prev - cur)

    main = shifted(cur_ref, prev_ref, mu_ref[...])
    small = shifted(smc_ref, smp_ref, mus_ref[...])[:, LANES:]
    r = main[:, :RW_WIDTH]
    k = main[:, RW_WIDTH:2 * RW_WIDTH]
    v = main[:, 2 * RW_WIDTH:]

    lane = lax.broadcasted_iota(jnp.int32, (1, LANES), 1)
    lora_in = jnp.where(lane < DECAY_LORA, jnp.tanh(small), small).astype(BF16)
    lora = _dot(lora_in, wlora_ref[...])
    w_log = -_softplus(-(w0_ref[...] + lora[:, :RW_WIDTH])) - 0.5
    a = _sigmoid(a0_ref[...] + lora[:, RW_WIDTH:])

    kk = k * kk_ref[...]
    hi, lo = _split_bf16(kk * kk)
    ss = _dot(hi, ones_ref[...]) + _dot(lo, ones_ref[...])
    kk = kk / jnp.maximum(jnp.sqrt(ss), 1e-12)

    r_ref[...] = r
    lw_ref[...] = -jnp.exp(w_log)
    k_ref[...] = k * (1.0 + (a - 1.0) * ka_ref[...])
    v_ref[...] = v
    na_ref[...] = -kk
    b_ref[...] = kk * a


def _rw_prep(p, mu, mus, wlora, w0, a0, k_k, k_a, ones_bd, seq, tm=512):
    m = p.shape[0]
    vec = lambda width: pl.BlockSpec((1, width), lambda i: (0, 0))
    out = jax.ShapeDtypeStruct((m, RW_WIDTH), F32)
    main_w = 3 * RW_WIDTH
    prev_blk = lambda i: (jnp.maximum(i * (tm // 8) - 1, 0))
    return pl.pallas_call(
        functools.partial(_rw_prep_kernel, tm=tm, seq=seq),
        out_shape=(out,) * 6,
        grid=(m // tm,),
        in_specs=[
            pl.BlockSpec((tm, main_w), lambda i: (i, SEC_RW // main_w)),
            pl.BlockSpec((8, main_w), lambda i: (prev_blk(i), SEC_RW // main_w)),
            pl.BlockSpec((tm, 256), lambda i: (i, SEC_SMALL // 256)),
            pl.BlockSpec((8, 256), lambda i: (prev_blk(i), SEC_SMALL // 256)),
            vec(main_w), vec(256),
            pl.BlockSpec(wlora.shape, lambda i: (0, 0)),
            vec(RW_WIDTH), vec(RW_WIDTH), vec(RW_WIDTH), vec(RW_WIDTH),
            pl.BlockSpec((RW_WIDTH, RW_WIDTH), lambda i: (0, 0)),
        ],
        out_specs=(pl.BlockSpec((tm, RW_WIDTH), lambda i: (i, 0)),) * 6,
        compiler_params=_params(("parallel",)),
        name="rw_prep",
    )(p, p, p, p, mu, mus, wlora, w0, a0, k_k, k_a, ones_bd)


_HI = lax.Precision.HIGHEST

RW_MODE_SCORE = "b1"
RW_MODE_INV = "b1"
RW_MODE_MID = "b1"
RW_MODE_STATE = "b1"


def _mm(a, b, mode, form="nn"):
    f = {"nn": _dot, "nt": _dot_nt, "tn": _dot_tn}[form]
    if mode == "hi":
        return f(a, b, _HI)
    if mode == "b1":
        return f(a.astype(BF16), b.astype(BF16))
    a_hi, a_lo = _split_bf16(a)
    b_hi, b_lo = _split_bf16(b)
    return f(a_hi, b_hi) + (f(a_hi, b_lo) + f(a_lo, b_hi))


def _rw_chunks(probs):
    c = RW_CHUNK
    n = 2 * c
    lane = lax.broadcasted_iota(jnp.int32, (1, LANES), 1)
    first = lane < RW_HEAD_DIM
    ti = lax.broadcasted_iota(jnp.int32, (c, c), 0)
    si = lax.broadcasted_iota(jnp.int32, (c, c), 1)
    tri = (si <= ti).astype(BF16)
    ri = lax.broadcasted_iota(jnp.int32, (n, n), 0)
    ci = lax.broadcasted_iota(jnp.int32, (n, n), 1)
    strict = (ci % c) < (ri % c)
    incl = (ci % c) <= (ri % c)
    eye = ri == ci

    def stack(x):
        return jnp.concatenate([jnp.where(first, x, 0.0), jnp.where(first, 0.0, x)], axis=0)

    splits = [_split_bf16(lw) for _, lw, _, _, _, _ in probs]
    cums = [_dot(tri, h) + _dot(tri, l) for h, l in splits]

    feats = []
    for (r, lw, k, v, na, b), cum in zip(probs, cums):
        total = cum[c - 1:c, :]
        p_inv = jnp.exp(-cum)
        p_end = jnp.exp(total - cum)
        at = stack(na * jnp.exp(cum - lw))
        rt = stack(r * jnp.exp(cum))
        lhs = jnp.concatenate([at, rt], axis=0)
        rhs = jnp.concatenate([stack(b * p_inv), stack(k * p_inv)], axis=0)
        feats.append(dict(at=at, rt=rt, lhs=lhs, rhs=rhs, bh=stack(b * p_end), kh=stack(k * p_end),
                          vs=stack(v), decay=jnp.exp(total)))

    gs = [_mm(f["lhs"], f["rhs"], RW_MODE_SCORE, "nt") for f in feats]
    l_ab = [jnp.where(strict, g[:n, :n], 0.0) for g in gs]
    l_ak = [jnp.where(strict, g[:n, n:], 0.0) for g in gs]
    m_rb = [jnp.where(incl, g[n:, :n], 0.0) for g in gs]
    m_rk = [jnp.where(incl, g[n:, n:], 0.0) for g in gs]

    xs = l_ab
    tinv = [jnp.where(eye, 1.0, 0.0) + x for x in xs]
    for _ in range(int(math.log2(c)) - 1):
        xs = [_mm(x, x, RW_MODE_INV) for x in xs]
        tinv = [t + _mm(t, x, RW_MODE_INV) for t, x in zip(tinv, xs)]

    lv = [_mm(l, f["vs"], RW_MODE_MID) for l, f in zip(l_ak, feats)]
    wu = [_mm(t, jnp.concatenate([f["at"], y], axis=1), RW_MODE_MID)
          for t, f, y in zip(tinv, feats, lv)]
    ro = [_mm(m, w, RW_MODE_MID) for m, w in zip(m_rb, wu)]
    rv = [_mm(m, f["vs"], RW_MODE_MID) for m, f in zip(m_rk, feats)]
    ab = [_mm(w, f["bh"], RW_MODE_STATE, "tn") for w, f in zip(wu, feats)]
    vk = [_mm(f["vs"], f["kh"], RW_MODE_STATE, "tn") for f in feats]

    terms = []
    for i, f in enumerate(feats):
        rh = f["rt"] + ro[i][:, :LANES]
        o0 = ro[i][:, LANES:] + rv[i]
        a_mat = jnp.where(eye, f["decay"], 0.0) + ab[i][:LANES]
        b_mat = ab[i][LANES:] + vk[i]
        terms.append((rh[:c] + rh[c:], o0[:c] + o0[c:], a_mat, b_mat))
    return terms


def _rw_scan_kernel(r_ref, lw_ref, k_ref, v_ref, na_ref, b_ref, g_ref, rk_ref, gng_ref, gnb_ref,
                    o_ref, s_ref):
    @pl.when(pl.program_id(0) == 0)
    def _():
        s_ref[...] = jnp.zeros_like(s_ref)

    lane = lax.broadcasted_iota(jnp.int32, (1, LANES), 1)
    first = lane < RW_HEAD_DIM
    batch = r_ref.shape[0]
    pairs = RW_WIDTH // LANES
    where = [(bi, slice(pr * LANES, (pr + 1) * LANES)) for bi in range(batch) for pr in range(pairs)]
    rows = [slice(q * RW_CHUNK, (q + 1) * RW_CHUNK) for q in range(RW_CHUNKS_PER_STEP)]

    def head_sum(y):
        s_first = jnp.sum(jnp.where(first, y, 0.0), axis=-1, keepdims=True)
        s_second = jnp.sum(jnp.where(first, 0.0, y), axis=-1, keepdims=True)
        return jnp.where(first, s_first, s_second)

    probs = [(r_ref[bi, rs, sl], lw_ref[bi, rs, sl], k_ref[bi, rs, sl], v_ref[bi, rs, sl], na_ref[bi, rs, sl],
              b_ref[bi, rs, sl]) for bi, sl in where for rs in rows]
    terms = _rw_chunks(probs)
    for i, (bi, sl) in enumerate(where):
        state = s_ref[i]
        for q, rs in enumerate(rows):
            n = i * RW_CHUNKS_PER_STEP + q
            rhat, o0, a_mat, b_mat = terms[n]
            o = _mm(rhat, state, RW_MODE_STATE, "nt") + o0
            state = _mm(state, a_mat, RW_MODE_STATE) + b_mat
            r, _, k, v, _, _ = probs[n]
            mu = head_sum(o) / RW_HEAD_DIM
            d = o - mu
            var = head_sum(d * d) / RW_HEAD_DIM
            normed = d * lax.rsqrt(var + RW_GN_EPS) * gng_ref[:, sl] + gnb_ref[:, sl]
            bonus = head_sum(r * k * rk_ref[:, sl]) * v
            o_ref[bi, rs, sl] = ((normed + bonus) * _silu(g_ref[bi, rs, sl])).astype(o_ref.dtype)
        s_ref[i] = state


def _rw_scan(r, lw, k, v, na, b, p, r_k, gn_g, gn_b, batch, seq):
    c = RW_CHUNK * RW_CHUNKS_PER_STEP
    as3d = lambda t: t.reshape(batch, seq, t.shape[-1])
    tok = pl.BlockSpec((batch, c, RW_WIDTH), lambda ci: (0, ci, 0))
    vec = pl.BlockSpec((1, RW_WIDTH), lambda ci: (0, 0))
    gate = pl.BlockSpec((batch, c, RW_WIDTH), lambda ci: (0, ci, SEC_RG // RW_WIDTH))
    out = pl.pallas_call(
        _rw_scan_kernel,
        out_shape=jax.ShapeDtypeStruct((batch, seq, RW_WIDTH), BF16),
        grid=(seq // c,),
        in_specs=[tok] * 6 + [gate, vec, vec, vec],
        out_specs=tok,
        scratch_shapes=[pltpu.VMEM((batch * RW_WIDTH // LANES, LANES, LANES), F32)],
        compiler_params=_params(("arbitrary",)),
        name="rw_scan",
    )(*[as3d(t) for t in (r, lw, k, v, na, b, p)], r_k, gn_g, gn_b)
    return out.reshape(batch * seq, RW_WIDTH)


def _rw_branch(p, mu, w0, w_up, a0, a_up, k_k, k_a, r_k, gn_g, gn_b, batch, seq):
    row = lambda t: t[None, :]
    head_ones = jnp.kron(jnp.eye(RW_HEADS, dtype=F32), jnp.ones((RW_HEAD_DIM, RW_HEAD_DIM), F32)).astype(BF16)
    mu_small = jnp.concatenate([jnp.zeros((LANES,), F32), mu[3 * RW_WIDTH:]])
    zeros = jnp.zeros((DECAY_LORA, RW_WIDTH), F32)
    wlora = jnp.concatenate([jnp.concatenate([w_up, zeros], axis=1),
                             jnp.concatenate([zeros, a_up], axis=1)], axis=0).astype(BF16)
    rr, lw, rk, rv, na, rb = _rw_prep(p, row(mu[:3 * RW_WIDTH]), row(mu_small), wlora, row(w0), row(a0),
                                      row(k_k), row(k_a), head_ones, seq)
    return _rw_scan(rr, lw, rk, rv, na, rb, p, r_k.reshape(1, RW_WIDTH), row(gn_g), row(gn_b), batch, seq)


def _merge_out_kernel(x_ref, ysb_ref, ymla_ref, yrw_ref, wsb_ref, wmla_ref, wrw_ref, g1_ref, g2_ref, g3_ref,
                      wout_ref, o_ref):
    merged = _sigmoid(g1_ref[...]) * _dot(ysb_ref[...], wsb_ref[...])
    merged = merged + _sigmoid(g2_ref[...]) * _dot(ymla_ref[...], wmla_ref[...])
    merged = merged + _sigmoid(g3_ref[...]) * _dot(yrw_ref[...], wrw_ref[...])
    o_ref[...] = x_ref[...] + _dot(merged.astype(BF16), wout_ref[...])


def _merge_out(x2, y_sb, y_mla, y_rw, w_sb, w_mla, w_rw, p, w_out, tm=256):
    m = p.shape[0]
    rows = lambda width: pl.BlockSpec((tm, width), lambda i: (i, 0))
    resident = lambda depth: pl.BlockSpec((depth, D_MODEL), lambda i: (0, 0), pipeline_mode=pl.Buffered(1))
    gate = lambda br: pl.BlockSpec((tm, D_MODEL), lambda i: (i, SEC_GATE // D_MODEL + br))
    return pl.pallas_call(
        _merge_out_kernel,
        out_shape=jax.ShapeDtypeStruct((m, D_MODEL), F32),
        grid=(m // tm,),
        in_specs=[rows(D_MODEL), rows(SB_WIDTH), rows(MLA_WIDTH), rows(RW_WIDTH),
                  resident(SB_WIDTH), resident(MLA_WIDTH), resident(RW_WIDTH),
                  gate(0), gate(1), gate(2), resident(D_MODEL)],
        out_specs=rows(D_MODEL),
        compiler_params=_params(("parallel",)),
        name="merge_out",
    )(x2, y_sb, y_mla, y_rw, w_sb, w_mla, w_rw, p, p, p, w_out)


def _pair_rope_cols(t):
    lead = t.shape[:-2]
    t = t.reshape(lead + (MLA_HEADS // 2, 2, 2, HALF_ROPE))
    t = jnp.swapaxes(t, -3, -2)
    return t.reshape(lead + (MLA_HEADS // 2 * LANES,))


def _pair_rope_gain(g):
    g1, g2 = g[:HALF_ROPE], g[HALF_ROPE:]
    return jnp.concatenate([g1, g1, g2, g2])[None, :]


def _rope_tables(seq):
    freqs = ROPE_THETA ** (-jnp.arange(HALF_ROPE, dtype=F32) / HALF_ROPE)
    ang = jnp.arange(seq, dtype=F32)[:, None] * freqs[None, :]
    c, s = jnp.cos(ang), jnp.sin(ang)
    return jnp.concatenate([c, c, c, c], axis=1), jnp.concatenate([-s, -s, s, s], axis=1)


def _mla_branch(p, cos, sin, q_norm_g, kv_norm_g, w_uq, w_ukv, qn_g, kn_g, batch, seq):
    row = lambda t: t[None, :]
    uq = w_uq.reshape(Q_LORA, MLA_HEADS, MLA_QK)
    wuq = jnp.concatenate([uq[:, :, :MLA_NOPE].reshape(Q_LORA, -1), _pair_rope_cols(uq[:, :, MLA_NOPE:])],
                          axis=1).astype(BF16)
    ukv = w_ukv.reshape(KV_LORA, MLA_HEADS, MLA_NOPE + MLA_V)
    wukv = jnp.concatenate([ukv[:, :, :MLA_NOPE].reshape(KV_LORA, -1),
                            ukv[:, :, MLA_NOPE:].reshape(KV_LORA, -1)], axis=1).astype(BF16)
    q, k, v = _mla_prep(p, cos, sin, row(q_norm_g), row(kv_norm_g), wuq, wukv,
                        row(qn_g[:MLA_NOPE]) * MLA_EXP2_SCALE, _pair_rope_gain(qn_g[MLA_NOPE:]) * MLA_EXP2_SCALE,
                        row(kn_g[:MLA_NOPE]), _pair_rope_gain(kn_g[MLA_NOPE:]), seq)
    return _mla_attn(q, k, v, p, batch, seq)


def kernel(x, norm_g, w_in, mla_q_norm_g, mla_kv_norm_g, mla_w_uq, mla_w_ukv, mla_qn_g, mla_kn_g,
           rw_mu, rw_w0, rw_w_up, rw_a0, rw_a_up, rw_k_k, rw_k_a, rw_r_k, rw_gn_g, rw_gn_b,
           w_br_sb, w_br_mla, w_br_rw, w_out):
    batch, seq, _ = x.shape
    depth = w_in.shape[0]
    x2 = x.reshape(batch * seq, D_MODEL)
    cos, sin = _rope_tables(seq)
    w_in_bf16 = _w_in_relayout(w_in)

    for l in range(depth):
        p = _in_proj(x2, norm_g[l][None, :], w_in_bf16, l)
        y_sb = _sb_attn(p, batch, seq)
        y_mla = _mla_branch(p, cos, sin, mla_q_norm_g[l], mla_kv_norm_g[l], mla_w_uq[l], mla_w_ukv[l],
                            mla_qn_g[l], mla_kn_g[l], batch, seq)
        y_rw = _rw_branch(p, rw_mu[l], rw_w0[l], rw_w_up[l], rw_a0[l], rw_a_up[l], rw_k_k[l], rw_k_a[l],
                          rw_r_k[l], rw_gn_g[l], rw_gn_b[l], batch, seq)
        x2 = _merge_out(x2, y_sb, y_mla, y_rw, w_br_sb[l].astype(BF16), w_br_mla[l].astype(BF16),
                        w_br_rw[l].astype(BF16), p, w_out[l].astype(BF16))

    return x2.reshape(batch, seq, D_MODEL)
```

```python
import functools
import math

import jax
import jax.numpy as jnp
from jax import lax
from jax.experimental import pallas as pl
from jax.experimental.pallas import tpu as pltpu

F32 = jnp.float32
BF16 = jnp.bfloat16

D_MODEL = 2048
EPS = 1e-6
CHUNK = 64

SB_HEADS = 4
SB_HEAD_DIM = 128
SB_WIDTH = SB_HEADS * SB_HEAD_DIM

MLA_HEADS = 8
MLA_NOPE = 128
MLA_ROPE = 64
MLA_QK = MLA_NOPE + MLA_ROPE
MLA_V = 128
MLA_WIDTH = MLA_HEADS * MLA_V
Q_LORA = 512
KV_LORA = 512
ROPE_THETA = 10000.0
HALF_ROPE = MLA_ROPE // 2

RW_HEADS = 8
RW_HEAD_DIM = 64
RW_WIDTH = RW_HEADS * RW_HEAD_DIM
DECAY_LORA = 64
ICL_LORA = 64
RW_GN_EPS = 64e-5
RW_CHUNK = 64
RW_CHUNKS_PER_STEP = 2

LANES = 128
VMEM_LIMIT = 56 * 1024 * 1024

SEC_SB = 0
SEC_MLA = 2048
SEC_RG = 4096
SEC_RW = 4608
SEC_GATE = 6144
SEC_SMALL = 12288
N_PROJ = SEC_SMALL + 256

_ORIG_KR = 4 * SB_WIDTH + Q_LORA + KV_LORA
_ORIG_MG = _ORIG_KR + MLA_ROPE
_ORIG_RW = _ORIG_MG + MLA_WIDTH
_ORIG_LORA = _ORIG_RW + 3 * RW_WIDTH
_ORIG_RG = _ORIG_LORA + DECAY_LORA + ICL_LORA
_ORIG_GATE = _ORIG_RG + RW_WIDTH
W_IN_MOVES = (
    (0, 0, _ORIG_KR),
    (SEC_MLA + Q_LORA + KV_LORA, _ORIG_MG, MLA_WIDTH),
    (SEC_RG, _ORIG_RG, RW_WIDTH),
    (SEC_RW, _ORIG_RW, 3 * RW_WIDTH),
    (SEC_GATE, _ORIG_GATE, 3 * D_MODEL),
    (SEC_SMALL, _ORIG_KR, HALF_ROPE),
    (SEC_SMALL + HALF_ROPE, _ORIG_KR, HALF_ROPE),
    (SEC_SMALL + 2 * HALF_ROPE, _ORIG_KR + HALF_ROPE, HALF_ROPE),
    (SEC_SMALL + 3 * HALF_ROPE, _ORIG_KR + HALF_ROPE, HALF_ROPE),
    (SEC_SMALL + LANES, _ORIG_LORA, DECAY_LORA + ICL_LORA),
)

NEG_BIG = -1e30

SB_DEAD_LOG_WEIGHT = -104.0

IN_PROJ_ROWS = 1024
IN_PROJ_COLS = 1792
PREP_ROWS = 512
MERGE_ROWS = 256
ATTN_TILE = 256
MLA_TILE = 512
MLA_EXP2_SCALE = (MLA_QK ** -0.5) * math.log2(math.e)
MLA_SUM_ROWS = 16
SB_GROUP = 4
MLA_GROUP = 4


def _dot(a, b, precision=None):
    return jnp.dot(a, b, preferred_element_type=F32, precision=precision)


def _dot_nt(a, b, precision=None):
    return lax.dot_general(a, b, (((1,), (1,)), ((), ())), preferred_element_type=F32,
                           precision=precision)


def _dot_tn(a, b, precision=None):
    return lax.dot_general(a, b, (((0,), (0,)), ((), ())), preferred_element_type=F32,
                           precision=precision)


def _split_bf16(x):
    hi = x.astype(BF16)
    lo = (x - hi.astype(F32)).astype(BF16)
    return hi, lo


def _softplus(z):
    return jnp.maximum(z, 0.0) + jnp.log(1.0 + jnp.exp(-jnp.abs(z)))


def _sigmoid(z):
    return 1.0 / (1.0 + jnp.exp(-z))


def _silu(z):
    return z * _sigmoid(z)


def _params(semantics):
    return pltpu.CompilerParams(dimension_semantics=semantics, vmem_limit_bytes=VMEM_LIMIT)


def _in_proj_kernel(x_ref, g_ref, w_ref, o_ref, h_ref):
    @pl.when(pl.program_id(1) == 0)
    def _():
        x = x_ref[...]
        rs = lax.rsqrt(jnp.mean(x * x, axis=-1, keepdims=True) + EPS)
        h_ref[...] = (x * rs * g_ref[...]).astype(BF16)

    o_ref[...] = _dot(h_ref[...], w_ref[...])


def _w_in_relayout_kernel(wt_ref, o_ref):
    for dst, src, width in W_IN_MOVES:
        for c in range(0, width, LANES):
            cw = min(LANES, width - c)
            o_ref[:, dst + c:dst + c + cw] = wt_ref[src + c:src + c + cw, :].T.astype(BF16)


def _w_in_relayout(w_in, tk=LANES):
    depth, _, n_in = w_in.shape
    return pl.pallas_call(
        _w_in_relayout_kernel,
        out_shape=jax.ShapeDtypeStruct((depth, D_MODEL, N_PROJ), BF16),
        grid=(depth, D_MODEL // tk),
        in_specs=[pl.BlockSpec((None, n_in, tk), lambda l, i: (l, 0, i))],
        out_specs=pl.BlockSpec((None, tk, N_PROJ), lambda l, i: (l, i, 0)),
        compiler_params=_params(("parallel", "parallel")),
        name="w_in_relayout",
    )(jnp.swapaxes(w_in, 1, 2))


def _in_proj(x2, g, w, layer, tm=IN_PROJ_ROWS, tn=IN_PROJ_COLS):
    m = x2.shape[0]
    return pl.pallas_call(
        _in_proj_kernel,
        out_shape=jax.ShapeDtypeStruct((m, N_PROJ), F32),
        grid=(m // tm, N_PROJ // tn),
        in_specs=[
            pl.BlockSpec((tm, D_MODEL), lambda i, j: (i, 0)),
            pl.BlockSpec((1, D_MODEL), lambda i, j: (0, 0)),
            pl.BlockSpec((None, D_MODEL, tn), lambda i, j: (layer, 0, j)),
        ],
        out_specs=pl.BlockSpec((tm, tn), lambda i, j: (i, j)),
        scratch_shapes=[pltpu.VMEM((tm, D_MODEL), BF16)],
        compiler_params=_params(("parallel", "arbitrary")),
        name="in_proj",
    )(x2, g, w)


def _sb_attn_kernel(q_ref, k_ref, v_ref, g_ref, o_ref, kb_ref, vt_ref, *, t, heads):
    i = pl.program_id(2)
    nblk = k_ref.shape[0] // t
    scale = SB_HEAD_DIM ** -0.5
    key = lax.broadcasted_iota(jnp.int32, (t, t), 0)
    qry = lax.broadcasted_iota(jnp.int32, (t, t), 1)
    later = (qry > key).astype(BF16)
    causal = key < qry
    head = lambda h: slice(h * LANES, (h + 1) * LANES)
    qs = [q_ref[:, head(h)].astype(BF16) for h in range(heads)]

    @pl.when(i == 0)
    def _():
        for h in range(heads):
            for blk in range(nblk):
                rows = slice(blk * t, (blk + 1) * t)
                kb_ref[h, blk] = k_ref[rows, head(h)].astype(BF16)
                vt_ref[h, blk] = v_ref[rows, head(h)].T.astype(BF16)

    def sweep(blocks, carry):
        zs = [[_dot_nt(kb_ref[h, j], qs[h]) for h in range(heads)] for j, _ in blocks]
        stage = []
        for (j, diagonal), zb in zip(blocks, zs):
            per_head = []
            for h in range(heads):
                z = zb[h] * scale
                sp = _softplus(z)
                log_fail = jnp.where(causal, -sp, 0.0) if diagonal else -sp
                hi, lo = _split_bf16(log_fail)
                per_head.append((z - sp, log_fail, hi, lo))
            stage.append(per_head)
        sums = [[_dot(later, hi) + _dot(later, lo) for _, _, hi, lo in per_head] for per_head in stage]
        runs = [carry[h][0] for h in range(heads)]
        ws = []
        for (j, diagonal), per_head, sums_b in zip(blocks, stage, sums):
            wb = []
            for h in range(heads):
                w = jnp.exp(per_head[h][0] + sums_b[h] + runs[h])
                if diagonal:
                    w = jnp.where(causal, w, 0.0)
                wb.append(w.astype(BF16))
                runs[h] = runs[h] + jnp.sum(per_head[h][1], axis=0, keepdims=True)
            ws.append(wb)
        accs = [carry[h][1] for h in range(heads)]
        for (j, _), wb in zip(blocks, ws):
            for h in range(heads):
                accs[h] = accs[h] + _dot(vt_ref[h, j], wb[h])
        return tuple((runs[h], accs[h]) for h in range(heads))

    def step(j, carry, diagonal):
        return sweep([(j, diagonal)], carry)

    init = tuple((jnp.zeros((1, t), F32), jnp.zeros((SB_HEAD_DIM, t), F32)) for _ in range(heads))
    carry = lax.cond(i > 0, lambda: sweep([(i, True), (i - 1, False)], init), lambda: sweep([(i, True)], init))

    def alive(c):
        return functools.reduce(jnp.maximum, [jnp.max(c[h][0]) for h in range(heads)]) >= SB_DEAD_LOG_WEIGHT

    def body(state):
        n, _, c = state
        c = step(i - 1 - n, c, False)
        return n + 1, alive(c), c

    _, _, carry = lax.while_loop(lambda s: jnp.logical_and(s[0] < i, s[1]), body, (1, alive(carry), carry))
    for h in range(heads):
        o_ref[:, head(h)] = (carry[h][1].T * _silu(g_ref[:, head(h)])).astype(o_ref.dtype)


def _sb_attn(p, batch, seq, t=ATTN_TILE, heads=SB_GROUP):
    nq = seq // t
    w = heads * LANES
    sec = SB_WIDTH // w
    tile = lambda s: pl.BlockSpec((t, w), lambda b, h, i: (b * nq + i, s * sec + h))
    full = lambda s: pl.BlockSpec((seq, w), lambda b, h, i: (b, s * sec + h))
    return pl.pallas_call(
        functools.partial(_sb_attn_kernel, t=t, heads=heads),
        out_shape=jax.ShapeDtypeStruct((batch * seq, SB_WIDTH), BF16),
        grid=(batch, SB_HEADS // heads, nq),
        in_specs=[tile(0), full(1), full(2), tile(3)],
        out_specs=pl.BlockSpec((t, w), lambda b, h, i: (b * nq + i, h)),
        scratch_shapes=[pltpu.VMEM((heads, nq, t, SB_HEAD_DIM), BF16),
                        pltpu.VMEM((heads, nq, SB_HEAD_DIM, t), BF16)],
        compiler_params=_params(("parallel", "parallel", "arbitrary")),
        name="sb_attn",
    )(p, p, p, p)


def _mla_prep_kernel(cq_ref, ckv_ref, sm_ref, cos_ref, sin_ref, gq_ref, gkv_ref, wuq_ref, wukv_ref,
                     gqn_ref, gqr_ref, gkn_ref, gkr_ref, q_ref, k_ref, v_ref):
    lane = lax.broadcasted_iota(jnp.int32, (1, LANES), 1)
    first = (lane % MLA_ROPE) < HALF_ROPE
    cos = cos_ref[...]
    sin = sin_ref[...]

    def latent_norm(c_ref, g_ref):
        c = c_ref[...]
        rs = lax.rsqrt(jnp.mean(c * c, axis=-1, keepdims=True) + EPS)
        return (c * rs * g_ref[...]).astype(BF16)

    def rotary(y):
        return y * cos + pltpu.roll(y, MLA_ROPE, axis=1) * sin

    def head_sums(sq):
        s_first = jnp.sum(jnp.where(first, sq, 0.0), axis=-1, keepdims=True)
        s_second = jnp.sum(jnp.where(first, 0.0, sq), axis=-1, keepdims=True)
        return s_first, s_second

    qfull = _dot(latent_norm(cq_ref, gq_ref), wuq_ref[...])
    kvfull = _dot(latent_norm(ckv_ref, gkv_ref), wukv_ref[...])
    v_ref[...] = kvfull[:, MLA_HEADS * MLA_NOPE:].astype(BF16)

    kr = sm_ref[...][:, :LANES]
    kr_ss, _ = head_sums(kr * kr)
    kr_rot = rotary(kr * gkr_ref[...])

    for pair in range(MLA_HEADS // 2):
        qr = qfull[:, MLA_HEADS * MLA_NOPE + pair * LANES:MLA_HEADS * MLA_NOPE + (pair + 1) * LANES]
        qr_ss = head_sums(qr * qr)
        q_rs, k_rs = [], []
        for e in range(2):
            h = 2 * pair + e
            qn = qfull[:, h * MLA_NOPE:(h + 1) * MLA_NOPE]
            kn = kvfull[:, h * MLA_NOPE:(h + 1) * MLA_NOPE]
            qs = lax.rsqrt((jnp.sum(qn * qn, axis=-1, keepdims=True) + qr_ss[e]) / MLA_QK + EPS)
            ks = lax.rsqrt((jnp.sum(kn * kn, axis=-1, keepdims=True) + kr_ss) / MLA_QK + EPS)
            q_rs.append(qs)
            k_rs.append(ks)
            q_ref[:, 2 * h * LANES:(2 * h + 1) * LANES] = (qn * qs * gqn_ref[...]).astype(BF16)
            k_ref[:, 2 * h * LANES:(2 * h + 1) * LANES] = (kn * ks * gkn_ref[...]).astype(BF16)
        q_rot = rotary(qr * jnp.where(first, q_rs[0], q_rs[1]) * gqr_ref[...])
        k_rot = (kr_rot * jnp.where(first, k_rs[0], k_rs[1])).astype(BF16)
        for e in range(2):
            h = 2 * pair + e
            own = first if e == 0 else jnp.logical_not(first)
            q_ref[:, (2 * h + 1) * LANES:(2 * h + 2) * LANES] = jnp.where(own, q_rot, 0.0).astype(BF16)
            k_ref[:, (2 * h + 1) * LANES:(2 * h + 2) * LANES] = k_rot


def _mla_prep(p, cos, sin, gq, gkv, wuq, wukv, gqn, gqr, gkn, gkr, seq, tm=PREP_ROWS):
    m = p.shape[0]
    sblk = seq // tm
    row = lambda width, cb: pl.BlockSpec((tm, width), lambda i: (i, cb))
    const = lambda shape: pl.BlockSpec(shape, lambda i: (0, 0))
    pos = pl.BlockSpec((tm, LANES), lambda i: (i % sblk, 0))
    qk_width = MLA_HEADS * 2 * LANES
    return pl.pallas_call(
        _mla_prep_kernel,
        out_shape=(jax.ShapeDtypeStruct((m, qk_width), BF16),
                   jax.ShapeDtypeStruct((m, qk_width), BF16),
                   jax.ShapeDtypeStruct((m, MLA_WIDTH), BF16)),
        grid=(m // tm,),
        in_specs=[
            row(Q_LORA, SEC_MLA // Q_LORA),
            row(KV_LORA, SEC_MLA // KV_LORA + 1),
            row(256, SEC_SMALL // 256),
            pos, pos,
            const((1, Q_LORA)), const((1, KV_LORA)),
            const(wuq.shape), const(wukv.shape),
            const((1, LANES)), const((1, LANES)), const((1, LANES)), const((1, LANES)),
        ],
        out_specs=(row(qk_width, 0), row(qk_width, 0), row(MLA_WIDTH, 0)),
        compiler_params=_params(("parallel",)),
        name="mla_prep",
    )(p, p, p, cos, sin, gq, gkv, wuq, wukv, gqn, gqr, gkn, gkr)


def _mla_attn_kernel(q_ref, k_ref, v_ref, g_ref, o_ref, vt_ref, sa_ref, sb_ref, sd_ref, m_ref, acc_ref,
                     *, t, heads):
    i = pl.program_id(2)
    nblk = v_ref.shape[0] // t
    qw = 2 * LANES
    key_chunk = lax.broadcasted_iota(jnp.int32, (t, t), 0) // CHUNK
    qry_chunk = lax.broadcasted_iota(jnp.int32, (t, t), 1) // CHUNK
    visible = key_chunk <= qry_chunk
    qs = [q_ref[:, h * qw:(h + 1) * qw] for h in range(heads)]

    @pl.when(i == 0)
    def _():
        for h in range(heads):
            for blk in range(nblk):
                vb = v_ref[blk * t:(blk + 1) * t, h * LANES:(h + 1) * LANES]
                vt_ref[h, blk, :MLA_V] = vb.astype(F32).T.astype(BF16)
                vt_ref[h, blk, MLA_V:] = jnp.ones((MLA_SUM_ROWS, t), BF16)

    def issue_scores(j, buf):
        start = pl.multiple_of(j * t, t)
        for h in range(heads):
            buf[h] = _dot_nt(k_ref[pl.ds(start, t), h * qw:(h + 1) * qw], qs[h])

    def absorb_diagonal(buf, j):
        weights = []
        for h in range(heads):
            s = jnp.where(visible, buf[h], NEG_BIG)
            m_new = jnp.max(s, axis=0, keepdims=True)
            m_ref[h] = m_new
            weights.append(jnp.exp2(s - m_new).astype(BF16))
        for h in range(heads):
            acc_ref[h] = _dot(vt_ref[h, j], weights[h])

    def absorb(buf, j):
        soft = []
        for h in range(heads):
            s = buf[h]
            m = m_ref[h]
            m_new = jnp.maximum(m, jnp.max(s, axis=0, keepdims=True))
            m_ref[h] = m_new
            soft.append((jnp.exp2(m - m_new), jnp.exp2(s - m_new).astype(BF16)))
        for h in range(heads):
            alpha, pr = soft[h]
            acc_ref[h] = alpha * acc_ref[h] + _dot(vt_ref[h, j], pr)

    issue_scores(i, sd_ref)
    issue_scores(0, sa_ref)
    absorb_diagonal(sd_ref, i)
    last = jnp.maximum(i - 1, 0)

    def pair(n, _):
        issue_scores(2 * n + 1, sb_ref)
        absorb(sa_ref, 2 * n)
        issue_scores(jnp.minimum(2 * n + 2, last), sa_ref)
        absorb(sb_ref, 2 * n + 1)
        return 0

    lax.fori_loop(0, i // 2, pair, 0)

    @pl.when(i % 2 == 1)
    def _():
        absorb(sa_ref, i - 1)

    for h in range(heads):
        hs = slice(h * LANES, (h + 1) * LANES)
        acc = acc_ref[h]
        y = (acc[:MLA_V] / acc[MLA_V:MLA_V + 1]).T
        o_ref[:, hs] = (y * _silu(g_ref[:, hs])).astype(o_ref.dtype)


def _mla_attn(q, k, v, p, batch, seq, t=MLA_TILE, heads=MLA_GROUP):
    nq = seq // t
    qw = heads * 2 * LANES
    vw = heads * LANES
    gate0 = (SEC_MLA + Q_LORA + KV_LORA) // vw
    return pl.pallas_call(
        functools.partial(_mla_attn_kernel, t=t, heads=heads),
        out_shape=jax.ShapeDtypeStruct((batch * seq, MLA_WIDTH), BF16),
        grid=(batch, MLA_HEADS // heads, nq),
        in_specs=[
            pl.BlockSpec((t, qw), lambda b, h, i: (b * nq + i, h)),
            pl.BlockSpec((seq, qw), lambda b, h, i: (b, h)),
            pl.BlockSpec((seq, vw), lambda b, h, i: (b, h)),
            pl.BlockSpec((t, vw), lambda b, h, i: (b * nq + i, gate0 + h)),
        ],
        out_specs=pl.BlockSpec((t, vw), lambda b, h, i: (b * nq + i, h)),
        scratch_shapes=[pltpu.VMEM((heads, nq, MLA_V + MLA_SUM_ROWS, t), BF16)]
        + [pltpu.VMEM((heads, t, t), F32)] * 3
        + [pltpu.VMEM((heads, 1, t), F32)]
        + [pltpu.VMEM((heads, MLA_V + MLA_SUM_ROWS, t), F32)],
        compiler_params=_params(("parallel", "parallel", "arbitrary")),
        name="mla_attn",
    )(q, k, v, p)


def _rw_prep_kernel(cur_ref, prev_ref, smc_ref, smp_ref, mu_ref, mus_ref, wlora_ref, w0_ref, a0_ref,
                    kk_ref, ka_ref, ones_ref, r_ref, lw_ref, k_ref, v_ref, na_ref, b_ref,
                    *, tm, seq):
    i = pl.program_id(0)
    at_start = (i * tm) % seq == 0
    row = lax.broadcasted_iota(jnp.int32, (tm, 1), 0)

    def shifted(c_ref, p_ref, mu):
        cur = c_ref[...]
        last = jnp.where(at_start, 0.0, p_ref[...][7:8, :])
        prev = jnp.where(row == 0, last, pltpu.roll(cur, 1, axis=0))
        return cur + mu * (prev - cur)

    main = shifted(cur_ref, prev_ref, mu_ref[...])
    small = shifted(smc_ref, smp_ref, mus_ref[...])[:, LANES:]
    r = main[:, :RW_WIDTH]
    k = main[:, RW_WIDTH:2 * RW_WIDTH]
    v = main[:, 2 * RW_WIDTH:]

    lane = lax.broadcasted_iota(jnp.int32, (1, LANES), 1)
    lora_in = jnp.where(lane < DECAY_LORA, jnp.tanh(small), small).astype(BF16)
    lora = _dot(lora_in, wlora_ref[...])
    w_log = -_softplus(-(w0_ref[...] + lora[:, :RW_WIDTH])) - 0.5
    a = _sigmoid(a0_ref[...] + lora[:, RW_WIDTH:])

    kk = k * kk_ref[...]
    hi, lo = _split_bf16(kk * kk)
    ss = _dot(hi, ones_ref[...]) + _dot(lo, ones_ref[...])
    kk = kk * jnp.minimum(lax.rsqrt(ss), 1e12)

    r_ref[...] = r
    lw_ref[...] = -jnp.exp(w_log)
    k_ref[...] = k * (1.0 + (a - 1.0) * ka_ref[...])
    v_ref[...] = v
    na_ref[...] = -kk
    b_ref[...] = kk * a


def _rw_prep(p, mu, mus, wlora, w0, a0, k_k, k_a, ones_bd, seq, tm=PREP_ROWS):
    m = p.shape[0]
    vec = lambda width: pl.BlockSpec((1, width), lambda i: (0, 0))
    out = jax.ShapeDtypeStruct((m, RW_WIDTH), F32)
    main_w = 3 * RW_WIDTH
    prev_blk = lambda i: (jnp.maximum(i * (tm // 8) - 1, 0))
    return pl.pallas_call(
        functools.partial(_rw_prep_kernel, tm=tm, seq=seq),
        out_shape=(out,) * 6,
        grid=(m // tm,),
        in_specs=[
            pl.BlockSpec((tm, main_w), lambda i: (i, SEC_RW // main_w)),
            pl.BlockSpec((8, main_w), lambda i: (prev_blk(i), SEC_RW // main_w)),
            pl.BlockSpec((tm, 256), lambda i: (i, SEC_SMALL // 256)),
            pl.BlockSpec((8, 256), lambda i: (prev_blk(i), SEC_SMALL // 256)),
            vec(main_w), vec(256),
            pl.BlockSpec(wlora.shape, lambda i: (0, 0)),
            vec(RW_WIDTH), vec(RW_WIDTH), vec(RW_WIDTH), vec(RW_WIDTH),
            pl.BlockSpec((RW_WIDTH, RW_WIDTH), lambda i: (0, 0)),
        ],
        out_specs=(pl.BlockSpec((tm, RW_WIDTH), lambda i: (i, 0)),) * 6,
        compiler_params=_params(("parallel",)),
        name="rw_prep",
    )(p, p, p, p, mu, mus, wlora, w0, a0, k_k, k_a, ones_bd)


_HI = lax.Precision.HIGHEST

RW_MODE_SCORE = "b1"
RW_MODE_INV = "b1"
RW_MODE_MID = "b1"
RW_MODE_STATE = "b1"


def _mm(a, b, mode, form="nn"):
    f = {"nn": _dot, "nt": _dot_nt, "tn": _dot_tn}[form]
    if mode == "hi":
        return f(a, b, _HI)
    if mode == "b1":
        return f(a.astype(BF16), b.astype(BF16))
    a_hi, a_lo = _split_bf16(a)
    b_hi, b_lo = _split_bf16(b)
    return f(a_hi, b_hi) + (f(a_hi, b_lo) + f(a_lo, b_hi))


def _rw_chunks(probs):
    c = RW_CHUNK
    n = 2 * c
    lane = lax.broadcasted_iota(jnp.int32, (1, LANES), 1)
    first = lane < RW_HEAD_DIM
    ti = lax.broadcasted_iota(jnp.int32, (c, c), 0)
    si = lax.broadcasted_iota(jnp.int32, (c, c), 1)
    tri = (si <= ti).astype(BF16)
    ri = lax.broadcasted_iota(jnp.int32, (n, n), 0)
    ci = lax.broadcasted_iota(jnp.int32, (n, n), 1)
    strict = (ci % c) < (ri % c)
    incl = (ci % c) <= (ri % c)
    eye = ri == ci

    def stack(x):
        return jnp.concatenate([jnp.where(first, x, 0.0), jnp.where(first, 0.0, x)], axis=0)

    splits = [_split_bf16(lw) for _, lw, _, _, _, _ in probs]
    cums = [_dot(tri, h) + _dot(tri, l) for h, l in splits]

    feats = []
    for (r, lw, k, v, na, b), cum in zip(probs, cums):
        total = cum[c - 1:c, :]
        p_inv = jnp.exp(-cum)
        p_end = jnp.exp(total - cum)
        at = stack(na * jnp.exp(cum - lw))
        rt = stack(r * jnp.exp(cum))
        lhs = jnp.concatenate([at, rt], axis=0)
        rhs = jnp.concatenate([stack(b * p_inv), stack(k * p_inv)], axis=0)
        feats.append(dict(at=at, rt=rt, lhs=lhs, rhs=rhs, bh=stack(b * p_end), kh=stack(k * p_end),
                          vs=stack(v), decay=jnp.exp(total)))

    gs = [_mm(f["lhs"], f["rhs"], RW_MODE_SCORE, "nt") for f in feats]
    l_ab = [jnp.where(strict, g[:n, :n], 0.0) for g in gs]
    l_ak = [jnp.where(strict, g[:n, n:], 0.0) for g in gs]
    m_rb = [jnp.where(incl, g[n:, :n], 0.0) for g in gs]
    m_rk = [jnp.where(incl, g[n:, n:], 0.0) for g in gs]

    xs = l_ab
    tinv = [jnp.where(eye, 1.0, 0.0) + x for x in xs]
    for _ in range(int(math.log2(c)) - 1):
        xs = [_mm(x, x, RW_MODE_INV) for x in xs]
        tinv = [t + _mm(t, x, RW_MODE_INV) for t, x in zip(tinv, xs)]

    lv = [_mm(l, f["vs"], RW_MODE_MID) for l, f in zip(l_ak, feats)]
    wu = [_mm(t, jnp.concatenate([f["at"], y], axis=1), RW_MODE_MID)
          for t, f, y in zip(tinv, feats, lv)]
    ro = [_mm(m, w, RW_MODE_MID) for m, w in zip(m_rb, wu)]
    rv = [_mm(m, f["vs"], RW_MODE_MID) for m, f in zip(m_rk, feats)]
    ab = [_mm(w, f["bh"], RW_MODE_STATE, "tn") for w, f in zip(wu, feats)]
    vk = [_mm(f["vs"], f["kh"], RW_MODE_STATE, "tn") for f in feats]

    terms = []
    for i, f in enumerate(feats):
        rh = f["rt"] + ro[i][:, :LANES]
        o0 = ro[i][:, LANES:] + rv[i]
        a_mat = jnp.where(eye, f["decay"], 0.0) + ab[i][:LANES]
        b_mat = ab[i][LANES:] + vk[i]
        terms.append((rh[:c] + rh[c:], o0[:c] + o0[c:], a_mat, b_mat))
    return terms


def _rw_scan_kernel(r_ref, lw_ref, k_ref, v_ref, na_ref, b_ref, g_ref, rk_ref, gng_ref, gnb_ref,
                    o_ref, s_ref):
    @pl.when(pl.program_id(0) == 0)
    def _():
        s_ref[...] = jnp.zeros_like(s_ref)

    lane = lax.broadcasted_iota(jnp.int32, (1, LANES), 1)
    first = lane < RW_HEAD_DIM
    batch = r_ref.shape[0]
    pairs = RW_WIDTH // LANES
    where = [(bi, slice(pr * LANES, (pr + 1) * LANES)) for bi in range(batch) for pr in range(pairs)]
    rows = [slice(q * RW_CHUNK, (q + 1) * RW_CHUNK) for q in range(RW_CHUNKS_PER_STEP)]

    def head_sum(y):
        s_first = jnp.sum(jnp.where(first, y, 0.0), axis=-1, keepdims=True)
        s_second = jnp.sum(jnp.where(first, 0.0, y), axis=-1, keepdims=True)
        return jnp.where(first, s_first, s_second)

    probs = [(r_ref[bi, rs, sl], lw_ref[bi, rs, sl], k_ref[bi, rs, sl], v_ref[bi, rs, sl], na_ref[bi, rs, sl],
              b_ref[bi, rs, sl]) for bi, sl in where for rs in rows]
    terms = _rw_chunks(probs)
    for i, (bi, sl) in enumerate(where):
        state = s_ref[i]
        for q, rs in enumerate(rows):
            n = i * RW_CHUNKS_PER_STEP + q
            rhat, o0, a_mat, b_mat = terms[n]
            o = _mm(rhat, state, RW_MODE_STATE, "nt") + o0
            state = _mm(state, a_mat, RW_MODE_STATE) + b_mat
            r, _, k, v, _, _ = probs[n]
            mu = head_sum(o) / RW_HEAD_DIM
            d = o - mu
            var = head_sum(d * d) / RW_HEAD_DIM
            normed = d * lax.rsqrt(var + RW_GN_EPS) * gng_ref[:, sl] + gnb_ref[:, sl]
            bonus = head_sum(r * k * rk_ref[:, sl]) * v
            o_ref[bi, rs, sl] = ((normed + bonus) * _silu(g_ref[bi, rs, sl])).astype(o_ref.dtype)
        s_ref[i] = state


def _rw_scan(r, lw, k, v, na, b, p, r_k, gn_g, gn_b, batch, seq):
    c = RW_CHUNK * RW_CHUNKS_PER_STEP
    as3d = lambda t: t.reshape(batch, seq, t.shape[-1])
    tok = pl.BlockSpec((batch, c, RW_WIDTH), lambda ci: (0, ci, 0))
    vec = pl.BlockSpec((1, RW_WIDTH), lambda ci: (0, 0))
    gate = pl.BlockSpec((batch, c, RW_WIDTH), lambda ci: (0, ci, SEC_RG // RW_WIDTH))
    out = pl.pallas_call(
        _rw_scan_kernel,
        out_shape=jax.ShapeDtypeStruct((batch, seq, RW_WIDTH), BF16),
        grid=(seq // c,),
        in_specs=[tok] * 6 + [gate, vec, vec, vec],
        out_specs=tok,
        scratch_shapes=[pltpu.VMEM((batch * RW_WIDTH // LANES, LANES, LANES), F32)],
        compiler_params=_params(("arbitrary",)),
        name="rw_scan",
    )(*[as3d(t) for t in (r, lw, k, v, na, b, p)], r_k, gn_g, gn_b)
    return out.reshape(batch * seq, RW_WIDTH)


def _rw_branch(p, mu, w0, w_up, a0, a_up, k_k, k_a, r_k, gn_g, gn_b, batch, seq):
    row = lambda t: t[None, :]
    head_ones = jnp.kron(jnp.eye(RW_HEADS, dtype=F32), jnp.ones((RW_HEAD_DIM, RW_HEAD_DIM), F32)).astype(BF16)
    mu_small = jnp.concatenate([jnp.zeros((LANES,), F32), mu[3 * RW_WIDTH:]])
    zeros = jnp.zeros((DECAY_LORA, RW_WIDTH), F32)
    wlora = jnp.concatenate([jnp.concatenate([w_up, zeros], axis=1),
                             jnp.concatenate([zeros, a_up], axis=1)], axis=0).astype(BF16)
    rr, lw, rk, rv, na, rb = _rw_prep(p, row(mu[:3 * RW_WIDTH]), row(mu_small), wlora, row(w0), row(a0),
                                      row(k_k), row(k_a), head_ones, seq)
    return _rw_scan(rr, lw, rk, rv, na, rb, p, r_k.reshape(1, RW_WIDTH), row(gn_g), row(gn_b), batch, seq)


def _merge_out_kernel(x_ref, ysb_ref, ymla_ref, yrw_ref, wsb_ref, wmla_ref, wrw_ref, g1_ref, g2_ref, g3_ref,
                      wout_ref, o_ref):
    merged = _sigmoid(g1_ref[...]) * _dot(ysb_ref[...], wsb_ref[...])
    merged = merged + _sigmoid(g2_ref[...]) * _dot(ymla_ref[...], wmla_ref[...])
    merged = merged + _sigmoid(g3_ref[...]) * _dot(yrw_ref[...], wrw_ref[...])
    o_ref[...] = x_ref[...] + _dot(merged.astype(BF16), wout_ref[...])


def _merge_out(x2, y_sb, y_mla, y_rw, w_sb, w_mla, w_rw, p, w_out, tm=MERGE_ROWS):
    m = p.shape[0]
    rows = lambda width: pl.BlockSpec((tm, width), lambda i: (i, 0))
    resident = lambda depth: pl.BlockSpec((depth, D_MODEL), lambda i: (0, 0), pipeline_mode=pl.Buffered(1))
    gate = lambda br: pl.BlockSpec((tm, D_MODEL), lambda i: (i, SEC_GATE // D_MODEL + br))
    return pl.pallas_call(
        _merge_out_kernel,
        out_shape=jax.ShapeDtypeStruct((m, D_MODEL), F32),
        grid=(m // tm,),
        in_specs=[rows(D_MODEL), rows(SB_WIDTH), rows(MLA_WIDTH), rows(RW_WIDTH),
                  resident(SB_WIDTH), resident(MLA_WIDTH), resident(RW_WIDTH),
                  gate(0), gate(1), gate(2), resident(D_MODEL)],
        out_specs=rows(D_MODEL),
        compiler_params=_params(("parallel",)),
        name="merge_out",
    )(x2, y_sb, y_mla, y_rw, w_sb, w_mla, w_rw, p, p, p, w_out)


def _pair_rope_cols(t):
    lead = t.shape[:-2]
    t = t.reshape(lead + (MLA_HEADS // 2, 2, 2, HALF_ROPE))
    t = jnp.swapaxes(t, -3, -2)
    return t.reshape(lead + (MLA_HEADS // 2 * LANES,))


def _pair_rope_gain(g):
    g1, g2 = g[:HALF_ROPE], g[HALF_ROPE:]
    return jnp.concatenate([g1, g1, g2, g2])[None, :]


def _rope_tables(seq):
    freqs = ROPE_THETA ** (-jnp.arange(HALF_ROPE, dtype=F32) / HALF_ROPE)
    ang = jnp.arange(seq, dtype=F32)[:, None] * freqs[None, :]
    c, s = jnp.cos(ang), jnp.sin(ang)
    return jnp.concatenate([c, c, c, c], axis=1), jnp.concatenate([-s, -s, s, s], axis=1)


def _mla_branch(p, cos, sin, q_norm_g, kv_norm_g, w_uq, w_ukv, qn_g, kn_g, batch, seq):
    row = lambda t: t[None, :]
    uq = w_uq.reshape(Q_LORA, MLA_HEADS, MLA_QK)
    wuq = jnp.concatenate([uq[:, :, :MLA_NOPE].reshape(Q_LORA, -1), _pair_rope_cols(uq[:, :, MLA_NOPE:])],
                          axis=1).astype(BF16)
    ukv = w_ukv.reshape(KV_LORA, MLA_HEADS, MLA_NOPE + MLA_V)
    wukv = jnp.concatenate([ukv[:, :, :MLA_NOPE].reshape(KV_LORA, -1),
                            ukv[:, :, MLA_NOPE:].reshape(KV_LORA, -1)], axis=1).astype(BF16)
    q, k, v = _mla_prep(p, cos, sin, row(q_norm_g), row(kv_norm_g), wuq, wukv,
                        row(qn_g[:MLA_NOPE]) * MLA_EXP2_SCALE, _pair_rope_gain(qn_g[MLA_NOPE:]) * MLA_EXP2_SCALE,
                        row(kn_g[:MLA_NOPE]), _pair_rope_gain(kn_g[MLA_NOPE:]), seq)
    return _mla_attn(q, k, v, p, batch, seq)


def kernel(x, norm_g, w_in, mla_q_norm_g, mla_kv_norm_g, mla_w_uq, mla_w_ukv, mla_qn_g, mla_kn_g,
           rw_mu, rw_w0, rw_w_up, rw_a0, rw_a_up, rw_k_k, rw_k_a, rw_r_k, rw_gn_g, rw_gn_b,
           w_br_sb, w_br_mla, w_br_rw, w_out):
    batch, seq, _ = x.shape
    depth = w_in.shape[0]
    x2 = x.reshape(batch * seq, D_MODEL)
    cos, sin = _rope_tables(seq)
    w_in_bf16 = _w_in_relayout(w_in)

    for l in range(depth):
        p = _in_proj(x2, norm_g[l][None, :], w_in_bf16, l)
        y_sb = _sb_attn(p, batch, seq)
        y_mla = _mla_branch(p, cos, sin, mla_q_norm_g[l], mla_kv_norm_g[l], mla_w_uq[l], mla_w_ukv[l],
                            mla_qn_g[l], mla_kn_g[l], batch, seq)
        y_rw = _rw_branch(p, rw_mu[l], rw_w0[l], rw_w_up[l], rw_a0[l], rw_a_up[l], rw_k_k[l], rw_k_a[l],
                          rw_r_k[l], rw_gn_g[l], rw_gn_b[l], batch, seq)
        x2 = _merge_out(x2, y_sb, y_mla, y_rw, w_br_sb[l].astype(BF16), w_br_mla[l].astype(BF16),
                        w_br_rw[l].astype(BF16), p, w_out[l].astype(BF16))

    return x2.reshape(batch, seq, D_MODEL)
```

```python
import functools
import math

import jax
import jax.numpy as jnp
from jax import lax
from jax.experimental import pallas as pl
from jax.experimental.pallas import tpu as pltpu

F32 = jnp.float32
BF16 = jnp.bfloat16

D_MODEL = 2048
EPS = 1e-6
CHUNK = 64

SB_HEADS = 4
SB_HEAD_DIM = 128
SB_WIDTH = SB_HEADS * SB_HEAD_DIM

MLA_HEADS = 8
MLA_NOPE = 128
MLA_ROPE = 64
MLA_QK = MLA_NOPE + MLA_ROPE
MLA_V = 128
MLA_WIDTH = MLA_HEADS * MLA_V
Q_LORA = 512
KV_LORA = 512
ROPE_THETA = 10000.0
HALF_ROPE = MLA_ROPE // 2

RW_HEADS = 8
RW_HEAD_DIM = 64
RW_WIDTH = RW_HEADS * RW_HEAD_DIM
DECAY_LORA = 64
ICL_LORA = 64
RW_GN_EPS = 64e-5
RW_CHUNK = 64
RW_CHUNKS_PER_STEP = 2

LANES = 128
VMEM_LIMIT = 56 * 1024 * 1024

SEC_SB = 0
SEC_MLA = 2048
SEC_RG = 4096
SEC_RW = 4608
SEC_GATE = 6144
SEC_SMALL = 12288
N_PROJ = SEC_SMALL + 256

_ORIG_KR = 4 * SB_WIDTH + Q_LORA + KV_LORA
_ORIG_MG = _ORIG_KR + MLA_ROPE
_ORIG_RW = _ORIG_MG + MLA_WIDTH
_ORIG_LORA = _ORIG_RW + 3 * RW_WIDTH
_ORIG_RG = _ORIG_LORA + DECAY_LORA + ICL_LORA
_ORIG_GATE = _ORIG_RG + RW_WIDTH
W_IN_MOVES = (
    (0, 0, _ORIG_KR),
    (SEC_MLA + Q_LORA + KV_LORA, _ORIG_MG, MLA_WIDTH),
    (SEC_RG, _ORIG_RG, RW_WIDTH),
    (SEC_RW, _ORIG_RW, 3 * RW_WIDTH),
    (SEC_GATE, _ORIG_GATE, 3 * D_MODEL),
    (SEC_SMALL, _ORIG_KR, HALF_ROPE),
    (SEC_SMALL + HALF_ROPE, _ORIG_KR, HALF_ROPE),
    (SEC_SMALL + 2 * HALF_ROPE, _ORIG_KR + HALF_ROPE, HALF_ROPE),
    (SEC_SMALL + 3 * HALF_ROPE, _ORIG_KR + HALF_ROPE, HALF_ROPE),
    (SEC_SMALL + LANES, _ORIG_LORA, DECAY_LORA + ICL_LORA),
)

NEG_BIG = -1e30

SB_DEAD_LOG_WEIGHT = -104.0

IN_PROJ_ROWS = 1024
IN_PROJ_COLS = 1792
PREP_ROWS = 512
MERGE_ROWS = 256
ATTN_TILE = 256
MLA_TILE = 512
MLA_EXP2_SCALE = (MLA_QK ** -0.5) * math.log2(math.e)
MLA_SUM_ROWS = 16
SB_GROUP = 4
MLA_GROUP = 4


def _dot(a, b, precision=None):
    return jnp.dot(a, b, preferred_element_type=F32, precision=precision)


def _dot_nt(a, b, precision=None):
    return lax.dot_general(a, b, (((1,), (1,)), ((), ())), preferred_element_type=F32,
                           precision=precision)


def _dot_tn(a, b, precision=None):
    return lax.dot_general(a, b, (((0,), (0,)), ((), ())), preferred_element_type=F32,
                           precision=precision)


def _split_bf16(x):
    hi = x.astype(BF16)
    lo = (x - hi.astype(F32)).astype(BF16)
    return hi, lo


def _softplus(z):
    return jnp.maximum(z, 0.0) + jnp.log(1.0 + jnp.exp(-jnp.abs(z)))


def _sigmoid(z):
    return 1.0 / (1.0 + jnp.exp(-z))


def _silu(z):
    return z * _sigmoid(z)


def _params(semantics):
    return pltpu.CompilerParams(dimension_semantics=semantics, vmem_limit_bytes=VMEM_LIMIT)


def _in_proj_kernel(x_ref, g_ref, w_ref, o_ref, h_ref):
    @pl.when(pl.program_id(1) == 0)
    def _():
        x = x_ref[...]
        rs = lax.rsqrt(jnp.mean(x * x, axis=-1, keepdims=True) + EPS)
        h_ref[...] = (x * rs * g_ref[...]).astype(BF16)

    o_ref[...] = _dot(h_ref[...], w_ref[...])


def _w_in_relayout_kernel(wt_ref, o_ref):
    for dst, src, width in W_IN_MOVES:
        for c in range(0, width, LANES):
            cw = min(LANES, width - c)
            o_ref[:, dst + c:dst + c + cw] = wt_ref[src + c:src + c + cw, :].T.astype(BF16)


def _w_in_relayout(w_in, tk=LANES):
    depth, _, n_in = w_in.shape
    return pl.pallas_call(
        _w_in_relayout_kernel,
        out_shape=jax.ShapeDtypeStruct((depth, D_MODEL, N_PROJ), BF16),
        grid=(depth, D_MODEL // tk),
        in_specs=[pl.BlockSpec((None, n_in, tk), lambda l, i: (l, 0, i))],
        out_specs=pl.BlockSpec((None, tk, N_PROJ), lambda l, i: (l, i, 0)),
        compiler_params=_params(("parallel", "parallel")),
        name="w_in_relayout",
    )(jnp.swapaxes(w_in, 1, 2))


def _in_proj(x2, g, w, layer, tm=IN_PROJ_ROWS, tn=IN_PROJ_COLS):
    m = x2.shape[0]
    return pl.pallas_call(
        _in_proj_kernel,
        out_shape=jax.ShapeDtypeStruct((m, N_PROJ), F32),
        grid=(m // tm, N_PROJ // tn),
        in_specs=[
            pl.BlockSpec((tm, D_MODEL), lambda i, j: (i, 0)),
            pl.BlockSpec((1, D_MODEL), lambda i, j: (0, 0)),
            pl.BlockSpec((None, D_MODEL, tn), lambda i, j: (layer, 0, j)),
        ],
        out_specs=pl.BlockSpec((tm, tn), lambda i, j: (i, j)),
        scratch_shapes=[pltpu.VMEM((tm, D_MODEL), BF16)],
        compiler_params=_params(("parallel", "arbitrary")),
        name="in_proj",
    )(x2, g, w)


def _sb_attn_kernel(q_ref, k_ref, v_ref, g_ref, o_ref, kb_ref, vt_ref, *, t, heads):
    i = pl.program_id(2)
    nblk = k_ref.shape[0] // t
    scale = SB_HEAD_DIM ** -0.5
    key = lax.broadcasted_iota(jnp.int32, (t, t), 0)
    qry = lax.broadcasted_iota(jnp.int32, (t, t), 1)
    later = (qry > key).astype(BF16)
    causal = key < qry
    head = lambda h: slice(h * LANES, (h + 1) * LANES)
    qs = [q_ref[:, head(h)].astype(BF16) for h in range(heads)]

    @pl.when(i == 0)
    def _():
        for h in range(heads):
            for blk in range(nblk):
                rows = slice(blk * t, (blk + 1) * t)
                kb_ref[h, blk] = k_ref[rows, head(h)].astype(BF16)
                vt_ref[h, blk] = v_ref[rows, head(h)].T.astype(BF16)

    def sweep(blocks, carry):
        zs = [[_dot_nt(kb_ref[h, j], qs[h]) for h in range(heads)] for j, _ in blocks]
        stage = []
        for (j, diagonal), zb in zip(blocks, zs):
            per_head = []
            for h in range(heads):
                z = zb[h] * scale
                sp = _softplus(z)
                log_fail = jnp.where(causal, -sp, 0.0) if diagonal else -sp
                hi, lo = _split_bf16(log_fail)
                per_head.append((z - sp, log_fail, hi, lo))
            stage.append(per_head)
        sums = [[_dot(later, hi) + _dot(later, lo) for _, _, hi, lo in per_head] for per_head in stage]
        runs = [carry[h][0] for h in range(heads)]
        ws = []
        for (j, diagonal), per_head, sums_b in zip(blocks, stage, sums):
            wb = []
            for h in range(heads):
                w = jnp.exp(per_head[h][0] + sums_b[h] + runs[h])
                if diagonal:
                    w = jnp.where(causal, w, 0.0)
                wb.append(w.astype(BF16))
                runs[h] = runs[h] + jnp.sum(per_head[h][1], axis=0, keepdims=True)
            ws.append(wb)
        accs = [carry[h][1] for h in range(heads)]
        for (j, _), wb in zip(blocks, ws):
            for h in range(heads):
                accs[h] = accs[h] + _dot(vt_ref[h, j], wb[h])
        return tuple((runs[h], accs[h]) for h in range(heads))

    def step(j, carry, diagonal):
        return sweep([(j, diagonal)], carry)

    init = tuple((jnp.zeros((1, t), F32), jnp.zeros((SB_HEAD_DIM, t), F32)) for _ in range(heads))
    carry = lax.cond(i > 0, lambda: sweep([(i, True), (i - 1, False)], init), lambda: sweep([(i, True)], init))

    def alive(c):
        return functools.reduce(jnp.maximum, [jnp.max(c[h][0]) for h in range(heads)]) >= SB_DEAD_LOG_WEIGHT

    def body(state):
        n, _, c = state
        c = step(i - 1 - n, c, False)
        return n + 1, alive(c), c

    _, _, carry = lax.while_loop(lambda s: jnp.logical_and(s[0] < i, s[1]), body, (1, alive(carry), carry))
    for h in range(heads):
        o_ref[:, head(h)] = (carry[h][1].T * _silu(g_ref[:, head(h)])).astype(o_ref.dtype)


def _sb_attn(p, batch, seq, t=ATTN_TILE, heads=SB_GROUP):
    nq = seq // t
    w = heads * LANES
    sec = SB_WIDTH // w
    tile = lambda s: pl.BlockSpec((t, w), lambda b, h, i: (b * nq + i, s * sec + h))
    full = lambda s: pl.BlockSpec((seq, w), lambda b, h, i: (b, s * sec + h))
    return pl.pallas_call(
        functools.partial(_sb_attn_kernel, t=t, heads=heads),
        out_shape=jax.ShapeDtypeStruct((batch * seq, SB_WIDTH), BF16),
        grid=(batch, SB_HEADS // heads, nq),
        in_specs=[tile(0), full(1), full(2), tile(3)],
        out_specs=pl.BlockSpec((t, w), lambda b, h, i: (b * nq + i, h)),
        scratch_shapes=[pltpu.VMEM((heads, nq, t, SB_HEAD_DIM), BF16),
                        pltpu.VMEM((heads, nq, SB_HEAD_DIM, t), BF16)],
        compiler_params=_params(("parallel", "parallel", "arbitrary")),
        name="sb_attn",
    )(p, p, p, p)


def _mla_prep_kernel(cq_ref, ckv_ref, sm_ref, cos_ref, sin_ref, gq_ref, gkv_ref, wuq_ref, wukv_ref,
                     gqn_ref, gqr_ref, gkn_ref, gkr_ref, q_ref, k_ref, v_ref):
    lane = lax.broadcasted_iota(jnp.int32, (1, LANES), 1)
    first = (lane % MLA_ROPE) < HALF_ROPE
    cos = cos_ref[...]
    sin = sin_ref[...]

    def latent_norm(c_ref, g_ref):
        c = c_ref[...]
        rs = lax.rsqrt(jnp.mean(c * c, axis=-1, keepdims=True) + EPS)
        return (c * rs * g_ref[...]).astype(BF16)

    def rotary(y):
        return y * cos + pltpu.roll(y, MLA_ROPE, axis=1) * sin

    def head_sums(sq):
        s_first = jnp.sum(jnp.where(first, sq, 0.0), axis=-1, keepdims=True)
        s_second = jnp.sum(jnp.where(first, 0.0, sq), axis=-1, keepdims=True)
        return s_first, s_second

    qfull = _dot(latent_norm(cq_ref, gq_ref), wuq_ref[...])
    kvfull = _dot(latent_norm(ckv_ref, gkv_ref), wukv_ref[...])
    v_ref[...] = kvfull[:, MLA_HEADS * MLA_NOPE:].astype(BF16)

    kr = sm_ref[...][:, :LANES]
    kr_ss, _ = head_sums(kr * kr)
    kr_rot = rotary(kr * gkr_ref[...])

    for pair in range(MLA_HEADS // 2):
        qr = qfull[:, MLA_HEADS * MLA_NOPE + pair * LANES:MLA_HEADS * MLA_NOPE + (pair + 1) * LANES]
        qr_ss = head_sums(qr * qr)
        q_rs, k_rs = [], []
        for e in range(2):
            h = 2 * pair + e
            qn = qfull[:, h * MLA_NOPE:(h + 1) * MLA_NOPE]
            kn = kvfull[:, h * MLA_NOPE:(h + 1) * MLA_NOPE]
            qs = lax.rsqrt((jnp.sum(qn * qn, axis=-1, keepdims=True) + qr_ss[e]) / MLA_QK + EPS)
            ks = lax.rsqrt((jnp.sum(kn * kn, axis=-1, keepdims=True) + kr_ss) / MLA_QK + EPS)
            q_rs.append(qs)
            k_rs.append(ks)
            q_ref[:, 2 * h * LANES:(2 * h + 1) * LANES] = (qn * qs * gqn_ref[...]).astype(BF16)
            k_ref[:, 2 * h * LANES:(2 * h + 1) * LANES] = (kn * ks * gkn_ref[...]).astype(BF16)
        q_rot = rotary(qr * jnp.where(first, q_rs[0], q_rs[1]) * gqr_ref[...])
        k_rot = (kr_rot * jnp.where(first, k_rs[0], k_rs[1])).astype(BF16)
        for e in range(2):
            h = 2 * pair + e
            own = first if e == 0 else jnp.logical_not(first)
            q_ref[:, (2 * h + 1) * LANES:(2 * h + 2) * LANES] = jnp.where(own, q_rot, 0.0).astype(BF16)
            k_ref[:, (2 * h + 1) * LANES:(2 * h + 2) * LANES] = k_rot


def _mla_prep(p, cos, sin, gq, gkv, wuq, wukv, gqn, gqr, gkn, gkr, seq, tm=PREP_ROWS):
    m = p.shape[0]
    sblk = seq // tm
    row = lambda width, cb: pl.BlockSpec((tm, width), lambda i: (i, cb))
    const = lambda shape: pl.BlockSpec(shape, lambda i: (0, 0))
    pos = pl.BlockSpec((tm, LANES), lambda i: (i % sblk, 0))
    qk_width = MLA_HEADS * 2 * LANES
    return pl.pallas_call(
        _mla_prep_kernel,
        out_shape=(jax.ShapeDtypeStruct((m, qk_width), BF16),
                   jax.ShapeDtypeStruct((m, qk_width), BF16),
                   jax.ShapeDtypeStruct((m, MLA_WIDTH), BF16)),
        grid=(m // tm,),
        in_specs=[
            row(Q_LORA, SEC_MLA // Q_LORA),
            row(KV_LORA, SEC_MLA // KV_LORA + 1),
            row(256, SEC_SMALL // 256),
            pos, pos,
            const((1, Q_LORA)), const((1, KV_LORA)),
            const(wuq.shape), const(wukv.shape),
            const((1, LANES)), const((1, LANES)), const((1, LANES)), const((1, LANES)),
        ],
        out_specs=(row(qk_width, 0), row(qk_width, 0), row(MLA_WIDTH, 0)),
        compiler_params=_params(("parallel",)),
        name="mla_prep",
    )(p, p, p, cos, sin, gq, gkv, wuq, wukv, gqn, gqr, gkn, gkr)


def _mla_attn_kernel(q_ref, k_ref, v_ref, g_ref, o_ref, vt_ref, sa_ref, sb_ref, sd_ref, ta_ref, tb_ref, td_ref,
                     m_ref, acc_ref, *, t, heads):
    i = pl.program_id(2)
    nblk = v_ref.shape[0] // t
    qw = 2 * LANES
    key_chunk = lax.broadcasted_iota(jnp.int32, (t, t), 0) // CHUNK
    qry_chunk = lax.broadcasted_iota(jnp.int32, (t, t), 1) // CHUNK
    visible = key_chunk <= qry_chunk
    qs = [q_ref[:, h * qw:(h + 1) * qw] for h in range(heads)]

    @pl.when(i == 0)
    def _():
        for h in range(heads):
            for blk in range(nblk):
                vb = v_ref[blk * t:(blk + 1) * t, h * LANES:(h + 1) * LANES]
                vt_ref[h, blk, :MLA_V] = vb.astype(F32).T.astype(BF16)
                vt_ref[h, blk, MLA_V:] = jnp.ones((MLA_SUM_ROWS, t), BF16)

    def issue_scores(j, buf, top, diagonal=False):
        start = pl.multiple_of(j * t, t)
        for h in range(heads):
            s = _dot_nt(k_ref[pl.ds(start, t), h * qw:(h + 1) * qw], qs[h])
            if diagonal:
                s = jnp.where(visible, s, NEG_BIG)
            buf[h] = s
            top[h] = jnp.max(s, axis=0, keepdims=True)

    def absorb_diagonal(buf, top, j):
        weights = []
        for h in range(heads):
            m_new = top[h]
            m_ref[h] = m_new
            weights.append(jnp.exp2(buf[h] - m_new).astype(BF16))
        for h in range(heads):
            acc_ref[h] = _dot(vt_ref[h, j], weights[h])

    def absorb(buf, top, j):
        soft = []
        for h in range(heads):
            m = m_ref[h]
            m_new = jnp.maximum(m, top[h])
            m_ref[h] = m_new
            soft.append((jnp.exp2(m - m_new), jnp.exp2(buf[h] - m_new).astype(BF16)))
        for h in range(heads):
            alpha, pr = soft[h]
            acc_ref[h] = alpha * acc_ref[h] + _dot(vt_ref[h, j], pr)

    issue_scores(i, sd_ref, td_ref, diagonal=True)
    issue_scores(0, sa_ref, ta_ref)
    absorb_diagonal(sd_ref, td_ref, i)
    last = jnp.maximum(i - 1, 0)

    def pair(n, _):
        issue_scores(2 * n + 1, sb_ref, tb_ref)
        absorb(sa_ref, ta_ref, 2 * n)
        issue_scores(jnp.minimum(2 * n + 2, last), sa_ref, ta_ref)
        absorb(sb_ref, tb_ref, 2 * n + 1)
        return 0

    lax.fori_loop(0, i // 2, pair, 0)

    @pl.when(i % 2 == 1)
    def _():
        absorb(sa_ref, ta_ref, i - 1)

    for h in range(heads):
        hs = slice(h * LANES, (h + 1) * LANES)
        acc = acc_ref[h]
        y = (acc[:MLA_V] / acc[MLA_V:MLA_V + 1]).T
        o_ref[:, hs] = (y * _silu(g_ref[:, hs])).astype(o_ref.dtype)


def _mla_attn(q, k, v, p, batch, seq, t=MLA_TILE, heads=MLA_GROUP):
    nq = seq // t
    qw = heads * 2 * LANES
    vw = heads * LANES
    gate0 = (SEC_MLA + Q_LORA + KV_LORA) // vw
    return pl.pallas_call(
        functools.partial(_mla_attn_kernel, t=t, heads=heads),
        out_shape=jax.ShapeDtypeStruct((batch * seq, MLA_WIDTH), BF16),
        grid=(batch, MLA_HEADS // heads, nq),
        in_specs=[
            pl.BlockSpec((t, qw), lambda b, h, i: (b * nq + i, h)),
            pl.BlockSpec((seq, qw), lambda b, h, i: (b, h)),
            pl.BlockSpec((seq, vw), lambda b, h, i: (b, h)),
            pl.BlockSpec((t, vw), lambda b, h, i: (b * nq + i, gate0 + h)),
        ],
        out_specs=pl.BlockSpec((t, vw), lambda b, h, i: (b * nq + i, h)),
        scratch_shapes=[pltpu.VMEM((heads, nq, MLA_V + MLA_SUM_ROWS, t), BF16)]
        + [pltpu.VMEM((heads, t, t), F32)] * 3
        + [pltpu.VMEM((heads, 1, t), F32)] * 4
        + [pltpu.VMEM((heads, MLA_V + MLA_SUM_ROWS, t), F32)],
        compiler_params=_params(("parallel", "parallel", "arbitrary")),
        name="mla_attn",
    )(q, k, v, p)


def _rw_prep_kernel(cur_ref, prev_ref, smc_ref, smp_ref, mu_ref, mus_ref, wlora_ref, w0_ref, a0_ref,
                    kk_ref, ka_ref, ones_ref, r_ref, lw_ref, k_ref, v_ref, na_ref, b_ref,
                    *, tm, seq):
    i = pl.program_id(0)
    at_start = (i * tm) % seq == 0
    row = lax.broadcasted_iota(jnp.int32, (tm, 1), 0)

    def shifted(c_ref, p_ref, mu):
        cur = c_ref[...]
        last = jnp.where(at_start, 0.0, p_ref[...][7:8, :])
        prev = jnp.where(row == 0, last, pltpu.roll(cur, 1, axis=0))
        return cur + mu * (prev - cur)

    main = shifted(cur_ref, prev_ref, mu_ref[...])
    small = shifted(smc_ref, smp_ref, mus_ref[...])[:, LANES:]
    r = main[:, :RW_WIDTH]
    k = main[:, RW_WIDTH:2 * RW_WIDTH]
    v = main[:, 2 * RW_WIDTH:]

    lane = lax.broadcasted_iota(jnp.int32, (1, LANES), 1)
    lora_in = jnp.where(lane < DECAY_LORA, jnp.tanh(small), small).astype(BF16)
    lora = _dot(lora_in, wlora_ref[...])
    w_log = -_softplus(-(w0_ref[...] + lora[:, :RW_WIDTH])) - 0.5
    a = _sigmoid(a0_ref[...] + lora[:, RW_WIDTH:])

    kk = k * kk_ref[...]
    hi, lo = _split_bf16(kk * kk)
    ss = _dot(hi, ones_ref[...]) + _dot(lo, ones_ref[...])
    kk = kk * jnp.minimum(lax.rsqrt(ss), 1e12)

    r_ref[...] = r
    lw_ref[...] = -jnp.exp(w_log)
    k_ref[...] = k * (1.0 + (a - 1.0) * ka_ref[...])
    v_ref[...] = v
    na_ref[...] = -kk
    b_ref[...] = kk * a


def _rw_prep(p, mu, mus, wlora, w0, a0, k_k, k_a, ones_bd, seq, tm=PREP_ROWS):
    m = p.shape[0]
    vec = lambda width: pl.BlockSpec((1, width), lambda i: (0, 0))
    out = jax.ShapeDtypeStruct((m, RW_WIDTH), F32)
    main_w = 3 * RW_WIDTH
    prev_blk = lambda i: (jnp.maximum(i * (tm // 8) - 1, 0))
    return pl.pallas_call(
        functools.partial(_rw_prep_kernel, tm=tm, seq=seq),
        out_shape=(out,) * 6,
        grid=(m // tm,),
        in_specs=[
            pl.BlockSpec((tm, main_w), lambda i: (i, SEC_RW // main_w)),
            pl.BlockSpec((8, main_w), lambda i: (prev_blk(i), SEC_RW // main_w)),
            pl.BlockSpec((tm, 256), lambda i: (i, SEC_SMALL // 256)),
            pl.BlockSpec((8, 256), lambda i: (prev_blk(i), SEC_SMALL // 256)),
            vec(main_w), vec(256),
            pl.BlockSpec(wlora.shape, lambda i: (0, 0)),
            vec(RW_WIDTH), vec(RW_WIDTH), vec(RW_WIDTH), vec(RW_WIDTH),
            pl.BlockSpec((RW_WIDTH, RW_WIDTH), lambda i: (0, 0)),
        ],
        out_specs=(pl.BlockSpec((tm, RW_WIDTH), lambda i: (i, 0)),) * 6,
        compiler_params=_params(("parallel",)),
        name="rw_prep",
    )(p, p, p, p, mu, mus, wlora, w0, a0, k_k, k_a, ones_bd)


_HI = lax.Precision.HIGHEST

RW_MODE_SCORE = "b1"
RW_MODE_INV = "b1"
RW_MODE_MID = "b1"
RW_MODE_STATE = "b1"


def _mm(a, b, mode, form="nn"):
    f = {"nn": _dot, "nt": _dot_nt, "tn": _dot_tn}[form]
    if mode == "hi":
        return f(a, b, _HI)
    if mode == "b1":
        return f(a.astype(BF16), b.astype(BF16))
    a_hi, a_lo = _split_bf16(a)
    b_hi, b_lo = _split_bf16(b)
    return f(a_hi, b_hi) + (f(a_hi, b_lo) + f(a_lo, b_hi))


def _rw_chunks(probs):
    c = RW_CHUNK
    n = 2 * c
    lane = lax.broadcasted_iota(jnp.int32, (1, LANES), 1)
    first = lane < RW_HEAD_DIM
    ti = lax.broadcasted_iota(jnp.int32, (c, c), 0)
    si = lax.broadcasted_iota(jnp.int32, (c, c), 1)
    tri = (si <= ti).astype(BF16)
    ri = lax.broadcasted_iota(jnp.int32, (n, n), 0)
    ci = lax.broadcasted_iota(jnp.int32, (n, n), 1)
    strict = (ci % c) < (ri % c)
    incl = (ci % c) <= (ri % c)
    eye = ri == ci

    def stack(x):
        return jnp.concatenate([jnp.where(first, x, 0.0), jnp.where(first, 0.0, x)], axis=0)

    splits = [_split_bf16(lw) for _, lw, _, _, _, _ in probs]
    cums = [_dot(tri, h) + _dot(tri, l) for h, l in splits]

    feats = []
    for (r, lw, k, v, na, b), cum in zip(probs, cums):
        total = cum[c - 1:c, :]
        p_inv = jnp.exp(-cum)
        p_end = jnp.exp(total - cum)
        at = stack(na * jnp.exp(cum - lw))
        rt = stack(r * jnp.exp(cum))
        lhs = jnp.concatenate([at, rt], axis=0)
        rhs = jnp.concatenate([stack(b * p_inv), stack(k * p_inv)], axis=0)
        feats.append(dict(at=at, rt=rt, lhs=lhs, rhs=rhs, bh=stack(b * p_end), kh=stack(k * p_end),
                          vs=stack(v), decay=jnp.exp(total)))

    gs = [_mm(f["lhs"], f["rhs"], RW_MODE_SCORE, "nt") for f in feats]
    l_ab = [jnp.where(strict, g[:n, :n], 0.0) for g in gs]
    l_ak = [jnp.where(strict, g[:n, n:], 0.0) for g in gs]
    m_rb = [jnp.where(incl, g[n:, :n], 0.0) for g in gs]
    m_rk = [jnp.where(incl, g[n:, n:], 0.0) for g in gs]

    xs = l_ab
    tinv = [jnp.where(eye, 1.0, 0.0) + x for x in xs]
    for _ in range(int(math.log2(c)) - 1):
        xs = [_mm(x, x, RW_MODE_INV) for x in xs]
        tinv = [t + _mm(t, x, RW_MODE_INV) for t, x in zip(tinv, xs)]

    lv = [_mm(l, f["vs"], RW_MODE_MID) for l, f in zip(l_ak, feats)]
    wu = [_mm(t, jnp.concatenate([f["at"], y], axis=1), RW_MODE_MID)
          for t, f, y in zip(tinv, feats, lv)]
    ro = [_mm(m, w, RW_MODE_MID) for m, w in zip(m_rb, wu)]
    rv = [_mm(m, f["vs"], RW_MODE_MID) for m, f in zip(m_rk, feats)]
    ab = [_mm(w, f["bh"], RW_MODE_STATE, "tn") for w, f in zip(wu, feats)]
    vk = [_mm(f["vs"], f["kh"], RW_MODE_STATE, "tn") for f in feats]

    terms = []
    for i, f in enumerate(feats):
        rh = f["rt"] + ro[i][:, :LANES]
        o0 = ro[i][:, LANES:] + rv[i]
        a_mat = jnp.where(eye, f["decay"], 0.0) + ab[i][:LANES]
        b_mat = ab[i][LANES:] + vk[i]
        terms.append((rh[:c] + rh[c:], o0[:c] + o0[c:], a_mat, b_mat))
    return terms


def _rw_scan_kernel(r_ref, lw_ref, k_ref, v_ref, na_ref, b_ref, g_ref, rk_ref, gng_ref, gnb_ref,
                    o_ref, s_ref):
    @pl.when(pl.program_id(0) == 0)
    def _():
        s_ref[...] = jnp.zeros_like(s_ref)

    lane = lax.broadcasted_iota(jnp.int32, (1, LANES), 1)
    first = lane < RW_HEAD_DIM
    batch = r_ref.shape[0]
    pairs = RW_WIDTH // LANES
    where = [(bi, slice(pr * LANES, (pr + 1) * LANES)) for bi in range(batch) for pr in range(pairs)]
    rows = [slice(q * RW_CHUNK, (q + 1) * RW_CHUNK) for q in range(RW_CHUNKS_PER_STEP)]

    def head_sum(y):
        s_first = jnp.sum(jnp.where(first, y, 0.0), axis=-1, keepdims=True)
        s_second = jnp.sum(jnp.where(first, 0.0, y), axis=-1, keepdims=True)
        return jnp.where(first, s_first, s_second)

    probs = [(r_ref[bi, rs, sl], lw_ref[bi, rs, sl], k_ref[bi, rs, sl], v_ref[bi, rs, sl], na_ref[bi, rs, sl],
              b_ref[bi, rs, sl]) for bi, sl in where for rs in rows]
    terms = _rw_chunks(probs)
    for i, (bi, sl) in enumerate(where):
        state = s_ref[i]
        for q, rs in enumerate(rows):
            n = i * RW_CHUNKS_PER_STEP + q
            rhat, o0, a_mat, b_mat = terms[n]
            o = _mm(rhat, state, RW_MODE_STATE, "nt") + o0
            state = _mm(state, a_mat, RW_MODE_STATE) + b_mat
            r, _, k, v, _, _ = probs[n]
            mu = head_sum(o) / RW_HEAD_DIM
            d = o - mu
            var = head_sum(d * d) / RW_HEAD_DIM
            normed = d * lax.rsqrt(var + RW_GN_EPS) * gng_ref[:, sl] + gnb_ref[:, sl]
            bonus = head_sum(r * k * rk_ref[:, sl]) * v
            o_ref[bi, rs, sl] = ((normed + bonus) * _silu(g_ref[bi, rs, sl])).astype(o_ref.dtype)
        s_ref[i] = state


def _rw_scan(r, lw, k, v, na, b, p, r_k, gn_g, gn_b, batch, seq):
    c = RW_CHUNK * RW_CHUNKS_PER_STEP
    as3d = lambda t: t.reshape(batch, seq, t.shape[-1])
    tok = pl.BlockSpec((batch, c, RW_WIDTH), lambda ci: (0, ci, 0))
    vec = pl.BlockSpec((1, RW_WIDTH), lambda ci: (0, 0))
    gate = pl.BlockSpec((batch, c, RW_WIDTH), lambda ci: (0, ci, SEC_RG // RW_WIDTH))
    out = pl.pallas_call(
        _rw_scan_kernel,
        out_shape=jax.ShapeDtypeStruct((batch, seq, RW_WIDTH), BF16),
        grid=(seq // c,),
        in_specs=[tok] * 6 + [gate, vec, vec, vec],
        out_specs=tok,
        scratch_shapes=[pltpu.VMEM((batch * RW_WIDTH // LANES, LANES, LANES), F32)],
        compiler_params=_params(("arbitrary",)),
        name="rw_scan",
    )(*[as3d(t) for t in (r, lw, k, v, na, b, p)], r_k, gn_g, gn_b)
    return out.reshape(batch * seq, RW_WIDTH)


def _rw_branch(p, mu, w0, w_up, a0, a_up, k_k, k_a, r_k, gn_g, gn_b, batch, seq):
    row = lambda t: t[None, :]
    head_ones = jnp.kron(jnp.eye(RW_HEADS, dtype=F32), jnp.ones((RW_HEAD_DIM, RW_HEAD_DIM), F32)).astype(BF16)
    mu_small = jnp.concatenate([jnp.zeros((LANES,), F32), mu[3 * RW_WIDTH:]])
    zeros = jnp.zeros((DECAY_LORA, RW_WIDTH), F32)
    wlora = jnp.concatenate([jnp.concatenate([w_up, zeros], axis=1),
                             jnp.concatenate([zeros, a_up], axis=1)], axis=0).astype(BF16)
    rr, lw, rk, rv, na, rb = _rw_prep(p, row(mu[:3 * RW_WIDTH]), row(mu_small), wlora, row(w0), row(a0),
                                      row(k_k), row(k_a), head_ones, seq)
    return _rw_scan(rr, lw, rk, rv, na, rb, p, r_k.reshape(1, RW_WIDTH), row(gn_g), row(gn_b), batch, seq)


def _merge_out_kernel(x_ref, ysb_ref, ymla_ref, yrw_ref, wsb_ref, wmla_ref, wrw_ref, g1_ref, g2_ref, g3_ref,
                      wout_ref, o_ref):
    merged = _sigmoid(g1_ref[...]) * _dot(ysb_ref[...], wsb_ref[...])
    merged = merged + _sigmoid(g2_ref[...]) * _dot(ymla_ref[...], wmla_ref[...])
    merged = merged + _sigmoid(g3_ref[...]) * _dot(yrw_ref[...], wrw_ref[...])
    o_ref[...] = x_ref[...] + _dot(merged.astype(BF16), wout_ref[...])


def _merge_out(x2, y_sb, y_mla, y_rw, w_sb, w_mla, w_rw, p, w_out, tm=MERGE_ROWS):
    m = p.shape[0]
    rows = lambda width: pl.BlockSpec((tm, width), lambda i: (i, 0))
    resident = lambda depth: pl.BlockSpec((depth, D_MODEL), lambda i: (0, 0), pipeline_mode=pl.Buffered(1))
    gate = lambda br: pl.BlockSpec((tm, D_MODEL), lambda i: (i, SEC_GATE // D_MODEL + br))
    return pl.pallas_call(
        _merge_out_kernel,
        out_shape=jax.ShapeDtypeStruct((m, D_MODEL), F32),
        grid=(m // tm,),
        in_specs=[rows(D_MODEL), rows(SB_WIDTH), rows(MLA_WIDTH), rows(RW_WIDTH),
                  resident(SB_WIDTH), resident(MLA_WIDTH), resident(RW_WIDTH),
                  gate(0), gate(1), gate(2), resident(D_MODEL)],
        out_specs=rows(D_MODEL),
        compiler_params=_params(("parallel",)),
        name="merge_out",
    )(x2, y_sb, y_mla, y_rw, w_sb, w_mla, w_rw, p, p, p, w_out)


def _pair_rope_cols(t):
    lead = t.shape[:-2]
    t = t.reshape(lead + (MLA_HEADS // 2, 2, 2, HALF_ROPE))
    t = jnp.swapaxes(t, -3, -2)
    return t.reshape(lead + (MLA_HEADS // 2 * LANES,))


def _pair_rope_gain(g):
    g1, g2 = g[:HALF_ROPE], g[HALF_ROPE:]
    return jnp.concatenate([g1, g1, g2, g2])[None, :]


def _rope_tables(seq):
    freqs = ROPE_THETA ** (-jnp.arange(HALF_ROPE, dtype=F32) / HALF_ROPE)
    ang = jnp.arange(seq, dtype=F32)[:, None] * freqs[None, :]
    c, s = jnp.cos(ang), jnp.sin(ang)
    return jnp.concatenate([c, c, c, c], axis=1), jnp.concatenate([-s, -s, s, s], axis=1)


def _mla_branch(p, cos, sin, q_norm_g, kv_norm_g, w_uq, w_ukv, qn_g, kn_g, batch, seq):
    row = lambda t: t[None, :]
    uq = w_uq.reshape(Q_LORA, MLA_HEADS, MLA_QK)
    wuq = jnp.concatenate([uq[:, :, :MLA_NOPE].reshape(Q_LORA, -1), _pair_rope_cols(uq[:, :, MLA_NOPE:])],
                          axis=1).astype(BF16)
    ukv = w_ukv.reshape(KV_LORA, MLA_HEADS, MLA_NOPE + MLA_V)
    wukv = jnp.concatenate([ukv[:, :, :MLA_NOPE].reshape(KV_LORA, -1),
                            ukv[:, :, MLA_NOPE:].reshape(KV_LORA, -1)], axis=1).astype(BF16)
    q, k, v = _mla_prep(p, cos, sin, row(q_norm_g), row(kv_norm_g), wuq, wukv,
                        row(qn_g[:MLA_NOPE]) * MLA_EXP2_SCALE, _pair_rope_gain(qn_g[MLA_NOPE:]) * MLA_EXP2_SCALE,
                        row(kn_g[:MLA_NOPE]), _pair_rope_gain(kn_g[MLA_NOPE:]), seq)
    return _mla_attn(q, k, v, p, batch, seq)


def kernel(x, norm_g, w_in, mla_q_norm_g, mla_kv_norm_g, mla_w_uq, mla_w_ukv, mla_qn_g, mla_kn_g,
           rw_mu, rw_w0, rw_w_up, rw_a0, rw_a_up, rw_k_k, rw_k_a, rw_r_k, rw_gn_g, rw_gn_b,
           w_br_sb, w_br_mla, w_br_rw, w_out):
    batch, seq, _ = x.shape
    depth = w_in.shape[0]
    x2 = x.reshape(batch * seq, D_MODEL)
    cos, sin = _rope_tables(seq)
    w_in_bf16 = _w_in_relayout(w_in)

    for l in range(depth):
        p = _in_proj(x2, norm_g[l][None, :], w_in_bf16, l)
        y_sb = _sb_attn(p, batch, seq)
        y_mla = _mla_branch(p, cos, sin, mla_q_norm_g[l], mla_kv_norm_g[l], mla_w_uq[l], mla_w_ukv[l],
                            mla_qn_g[l], mla_kn_g[l], batch, seq)
        y_rw = _rw_branch(p, rw_mu[l], rw_w0[l], rw_w_up[l], rw_a0[l], rw_a_up[l], rw_k_k[l], rw_k_a[l],
                          rw_r_k[l], rw_gn_g[l], rw_gn_b[l], batch, seq)
        x2 = _merge_out(x2, y_sb, y_mla, y_rw, w_br_sb[l].astype(BF16), w_br_mla[l].astype(BF16),
                        w_br_rw[l].astype(BF16), p, w_out[l].astype(BF16))

    return x2.reshape(batch, seq, D_MODEL)
```

```python
import functools
import math

import jax
import jax.numpy as jnp
from jax import lax
from jax.experimental import pallas as pl
from jax.experimental.pallas import tpu as pltpu

F32 = jnp.float32
BF16 = jnp.bfloat16

D_MODEL = 2048
EPS = 1e-6
CHUNK = 64

SB_HEADS = 4
SB_HEAD_DIM = 128
SB_WIDTH = SB_HEADS * SB_HEAD_DIM

MLA_HEADS = 8
MLA_NOPE = 128
MLA_ROPE = 64
MLA_QK = MLA_NOPE + MLA_ROPE
MLA_V = 128
MLA_WIDTH = MLA_HEADS * MLA_V
Q_LORA = 512
KV_LORA = 512
ROPE_THETA = 10000.0
HALF_ROPE = MLA_ROPE // 2

RW_HEADS = 8
RW_HEAD_DIM = 64
RW_WIDTH = RW_HEADS * RW_HEAD_DIM
DECAY_LORA = 64
ICL_LORA = 64
RW_GN_EPS = 64e-5
RW_CHUNK = 64
RW_CHUNKS_PER_STEP = 2

LANES = 128
VMEM_LIMIT = 56 * 1024 * 1024

SEC_SB = 0
SEC_MLA = 2048
SEC_RG = 4096
SEC_RW = 4608
SEC_GATE = 6144
SEC_SMALL = 12288
N_PROJ = SEC_SMALL + 256

_ORIG_KR = 4 * SB_WIDTH + Q_LORA + KV_LORA
_ORIG_MG = _ORIG_KR + MLA_ROPE
_ORIG_RW = _ORIG_MG + MLA_WIDTH
_ORIG_LORA = _ORIG_RW + 3 * RW_WIDTH
_ORIG_RG = _ORIG_LORA + DECAY_LORA + ICL_LORA
_ORIG_GATE = _ORIG_RG + RW_WIDTH
W_IN_MOVES = (
    (0, 0, _ORIG_KR),
    (SEC_MLA + Q_LORA + KV_LORA, _ORIG_MG, MLA_WIDTH),
    (SEC_RG, _ORIG_RG, RW_WIDTH),
    (SEC_RW, _ORIG_RW, 3 * RW_WIDTH),
    (SEC_GATE, _ORIG_GATE, 3 * D_MODEL),
    (SEC_SMALL, _ORIG_KR, HALF_ROPE),
    (SEC_SMALL + HALF_ROPE, _ORIG_KR, HALF_ROPE),
    (SEC_SMALL + 2 * HALF_ROPE, _ORIG_KR + HALF_ROPE, HALF_ROPE),
    (SEC_SMALL + 3 * HALF_ROPE, _ORIG_KR + HALF_ROPE, HALF_ROPE),
    (SEC_SMALL + LANES, _ORIG_LORA, DECAY_LORA + ICL_LORA),
)

NEG_BIG = -1e30

SB_DEAD_LOG_WEIGHT = -104.0

IN_PROJ_ROWS = 1024
IN_PROJ_COLS = 1792
PREP_ROWS = 512
MERGE_ROWS = 256
ATTN_TILE = 256
MLA_TILE = 512
MLA_EXP2_SCALE = (MLA_QK ** -0.5) * math.log2(math.e)
MLA_SUM_ROWS = 16
SB_GROUP = 4
MLA_GROUP = 4


def _dot(a, b, precision=None):
    return jnp.dot(a, b, preferred_element_type=F32, precision=precision)


def _dot_nt(a, b, precision=None):
    return lax.dot_general(a, b, (((1,), (1,)), ((), ())), preferred_element_type=F32,
                           precision=precision)


def _dot_tn(a, b, precision=None):
    return lax.dot_general(a, b, (((0,), (0,)), ((), ())), preferred_element_type=F32,
                           precision=precision)


def _split_bf16(x):
    hi = x.astype(BF16)
    lo = (x - hi.astype(F32)).astype(BF16)
    return hi, lo


def _softplus(z):
    return jnp.maximum(z, 0.0) + jnp.log(1.0 + jnp.exp(-jnp.abs(z)))


def _sigmoid(z):
    return 1.0 / (1.0 + jnp.exp(-z))


def _silu(z):
    return z * _sigmoid(z)


def _params(semantics):
    return pltpu.CompilerParams(dimension_semantics=semantics, vmem_limit_bytes=VMEM_LIMIT)


def _in_proj_kernel(x_ref, g_ref, w_ref, o_ref, h_ref):
    @pl.when(pl.program_id(1) == 0)
    def _():
        x = x_ref[...]
        rs = lax.rsqrt(jnp.mean(x * x, axis=-1, keepdims=True) + EPS)
        h_ref[...] = (x * rs * g_ref[...]).astype(BF16)

    o_ref[...] = _dot(h_ref[...], w_ref[...])


def _w_in_relayout_kernel(wt_ref, o_ref):
    for dst, src, width in W_IN_MOVES:
        for c in range(0, width, LANES):
            cw = min(LANES, width - c)
            o_ref[:, dst + c:dst + c + cw] = wt_ref[src + c:src + c + cw, :].T.astype(BF16)


def _w_in_relayout(w_in, tk=LANES):
    depth, _, n_in = w_in.shape
    return pl.pallas_call(
        _w_in_relayout_kernel,
        out_shape=jax.ShapeDtypeStruct((depth, D_MODEL, N_PROJ), BF16),
        grid=(depth, D_MODEL // tk),
        in_specs=[pl.BlockSpec((None, n_in, tk), lambda l, i: (l, 0, i))],
        out_specs=pl.BlockSpec((None, tk, N_PROJ), lambda l, i: (l, i, 0)),
        compiler_params=_params(("parallel", "parallel")),
        name="w_in_relayout",
    )(jnp.swapaxes(w_in, 1, 2))


def _in_proj(x2, g, w, layer, tm=IN_PROJ_ROWS, tn=IN_PROJ_COLS):
    m = x2.shape[0]
    return pl.pallas_call(
        _in_proj_kernel,
        out_shape=jax.ShapeDtypeStruct((m, N_PROJ), F32),
        grid=(m // tm, N_PROJ // tn),
        in_specs=[
            pl.BlockSpec((tm, D_MODEL), lambda i, j: (i, 0)),
            pl.BlockSpec((1, D_MODEL), lambda i, j: (0, 0)),
            pl.BlockSpec((None, D_MODEL, tn), lambda i, j: (layer, 0, j)),
        ],
        out_specs=pl.BlockSpec((tm, tn), lambda i, j: (i, j)),
        scratch_shapes=[pltpu.VMEM((tm, D_MODEL), BF16)],
        compiler_params=_params(("parallel", "arbitrary")),
        name="in_proj",
    )(x2, g, w)


def _sb_attn_kernel(q_ref, k_ref, v_ref, g_ref, o_ref, kb_ref, vt_ref, *, t, heads):
    i = pl.program_id(2)
    nblk = k_ref.shape[0] // t
    scale = SB_HEAD_DIM ** -0.5
    key = lax.broadcasted_iota(jnp.int32, (t, t), 0)
    qry = lax.broadcasted_iota(jnp.int32, (t, t), 1)
    later = (qry > key).astype(BF16)
    causal = key < qry
    head = lambda h: slice(h * LANES, (h + 1) * LANES)
    qs = [q_ref[:, head(h)].astype(BF16) for h in range(heads)]

    @pl.when(i == 0)
    def _():
        for h in range(heads):
            for blk in range(nblk):
                rows = slice(blk * t, (blk + 1) * t)
                kb_ref[h, blk] = k_ref[rows, head(h)].astype(BF16)
                vt_ref[h, blk] = v_ref[rows, head(h)].T.astype(BF16)

    def sweep(blocks, carry):
        zs = [[_dot_nt(kb_ref[h, j], qs[h]) for h in range(heads)] for j, _ in blocks]
        stage = []
        for (j, diagonal), zb in zip(blocks, zs):
            per_head = []
            for h in range(heads):
                z = zb[h] * scale
                sp = _softplus(z)
                log_fail = jnp.where(causal, -sp, 0.0) if diagonal else -sp
                hi, lo = _split_bf16(log_fail)
                per_head.append((z - sp, log_fail, hi, lo))
            stage.append(per_head)
        sums = [[_dot(later, hi) + _dot(later, lo) for _, _, hi, lo in per_head] for per_head in stage]
        runs = [carry[h][0] for h in range(heads)]
        ws = []
        for (j, diagonal), per_head, sums_b in zip(blocks, stage, sums):
            wb = []
            for h in range(heads):
                w = jnp.exp(per_head[h][0] + sums_b[h] + runs[h])
                if diagonal:
                    w = jnp.where(causal, w, 0.0)
                wb.append(w.astype(BF16))
                runs[h] = runs[h] + jnp.sum(per_head[h][1], axis=0, keepdims=True)
            ws.append(wb)
        accs = [carry[h][1] for h in range(heads)]
        for (j, _), wb in zip(blocks, ws):
            for h in range(heads):
                accs[h] = accs[h] + _dot(vt_ref[h, j], wb[h])
        return tuple((runs[h], accs[h]) for h in range(heads))

    def step(j, carry, diagonal):
        return sweep([(j, diagonal)], carry)

    init = tuple((jnp.zeros((1, t), F32), jnp.zeros((SB_HEAD_DIM, t), F32)) for _ in range(heads))
    carry = lax.cond(i > 0, lambda: sweep([(i, True), (i - 1, False)], init), lambda: sweep([(i, True)], init))

    def alive(c):
        return functools.reduce(jnp.maximum, [jnp.max(c[h][0]) for h in range(heads)]) >= SB_DEAD_LOG_WEIGHT

    def body(state):
        n, _, c = state
        c = step(i - 1 - n, c, False)
        return n + 1, alive(c), c

    _, _, carry = lax.while_loop(lambda s: jnp.logical_and(s[0] < i, s[1]), body, (1, alive(carry), carry))
    for h in range(heads):
        o_ref[:, head(h)] = (carry[h][1].T * _silu(g_ref[:, head(h)])).astype(o_ref.dtype)


def _sb_attn(p, batch, seq, t=ATTN_TILE, heads=SB_GROUP):
    nq = seq // t
    w = heads * LANES
    sec = SB_WIDTH // w
    tile = lambda s: pl.BlockSpec((t, w), lambda b, h, i: (b * nq + i, s * sec + h))
    full = lambda s: pl.BlockSpec((seq, w), lambda b, h, i: (b, s * sec + h))
    return pl.pallas_call(
        functools.partial(_sb_attn_kernel, t=t, heads=heads),
        out_shape=jax.ShapeDtypeStruct((batch * seq, SB_WIDTH), BF16),
        grid=(batch, SB_HEADS // heads, nq),
        in_specs=[tile(0), full(1), full(2), tile(3)],
        out_specs=pl.BlockSpec((t, w), lambda b, h, i: (b * nq + i, h)),
        scratch_shapes=[pltpu.VMEM((heads, nq, t, SB_HEAD_DIM), BF16),
                        pltpu.VMEM((heads, nq, SB_HEAD_DIM, t), BF16)],
        compiler_params=_params(("parallel", "parallel", "arbitrary")),
        name="sb_attn",
    )(p, p, p, p)


def _mla_prep_kernel(cq_ref, ckv_ref, sm_ref, cos_ref, sin_ref, gq_ref, gkv_ref, wuq_ref, wukv_ref,
                     gqn_ref, gqr_ref, gkn_ref, gkr_ref, q_ref, k_ref, v_ref):
    lane = lax.broadcasted_iota(jnp.int32, (1, LANES), 1)
    first = (lane % MLA_ROPE) < HALF_ROPE
    cos = cos_ref[...]
    sin = sin_ref[...]

    def latent_norm(c_ref, g_ref):
        c = c_ref[...]
        rs = lax.rsqrt(jnp.mean(c * c, axis=-1, keepdims=True) + EPS)
        return (c * rs * g_ref[...]).astype(BF16)

    def rotary(y):
        return y * cos + pltpu.roll(y, MLA_ROPE, axis=1) * sin

    def head_sums(sq):
        s_first = jnp.sum(jnp.where(first, sq, 0.0), axis=-1, keepdims=True)
        s_second = jnp.sum(jnp.where(first, 0.0, sq), axis=-1, keepdims=True)
        return s_first, s_second

    qfull = _dot(latent_norm(cq_ref, gq_ref), wuq_ref[...])
    kvfull = _dot(latent_norm(ckv_ref, gkv_ref), wukv_ref[...])
    v_ref[...] = kvfull[:, MLA_HEADS * MLA_NOPE:].astype(BF16)

    kr = sm_ref[...][:, :LANES]
    kr_ss, _ = head_sums(kr * kr)
    kr_rot = rotary(kr * gkr_ref[...])

    for pair in range(MLA_HEADS // 2):
        qr = qfull[:, MLA_HEADS * MLA_NOPE + pair * LANES:MLA_HEADS * MLA_NOPE + (pair + 1) * LANES]
        qr_ss = head_sums(qr * qr)
        q_rs, k_rs = [], []
        for e in range(2):
            h = 2 * pair + e
            qn = qfull[:, h * MLA_NOPE:(h + 1) * MLA_NOPE]
            kn = kvfull[:, h * MLA_NOPE:(h + 1) * MLA_NOPE]
            qs = lax.rsqrt((jnp.sum(qn * qn, axis=-1, keepdims=True) + qr_ss[e]) / MLA_QK + EPS)
            ks = lax.rsqrt((jnp.sum(kn * kn, axis=-1, keepdims=True) + kr_ss) / MLA_QK + EPS)
            q_rs.append(qs)
            k_rs.append(ks)
            q_ref[:, 2 * h * LANES:(2 * h + 1) * LANES] = (qn * qs * gqn_ref[...]).astype(BF16)
            k_ref[:, 2 * h * LANES:(2 * h + 1) * LANES] = (kn * ks * gkn_ref[...]).astype(BF16)
        q_rot = rotary(qr * jnp.where(first, q_rs[0], q_rs[1]) * gqr_ref[...])
        k_rot = (kr_rot * jnp.where(first, k_rs[0], k_rs[1])).astype(BF16)
        for e in range(2):
            h = 2 * pair + e
            own = first if e == 0 else jnp.logical_not(first)
            q_ref[:, (2 * h + 1) * LANES:(2 * h + 2) * LANES] = jnp.where(own, q_rot, 0.0).astype(BF16)
            k_ref[:, (2 * h + 1) * LANES:(2 * h + 2) * LANES] = k_rot


def _mla_prep(p, cos, sin, gq, gkv, wuq, wukv, gqn, gqr, gkn, gkr, seq, tm=PREP_ROWS):
    m = p.shape[0]
    sblk = seq // tm
    row = lambda width, cb: pl.BlockSpec((tm, width), lambda i: (i, cb))
    const = lambda shape: pl.BlockSpec(shape, lambda i: (0, 0))
    pos = pl.BlockSpec((tm, LANES), lambda i: (i % sblk, 0))
    qk_width = MLA_HEADS * 2 * LANES
    return pl.pallas_call(
        _mla_prep_kernel,
        out_shape=(jax.ShapeDtypeStruct((m, qk_width), BF16),
                   jax.ShapeDtypeStruct((m, qk_width), BF16),
                   jax.ShapeDtypeStruct((m, MLA_WIDTH), BF16)),
        grid=(m // tm,),
        in_specs=[
            row(Q_LORA, SEC_MLA // Q_LORA),
            row(KV_LORA, SEC_MLA // KV_LORA + 1),
            row(256, SEC_SMALL // 256),
            pos, pos,
            const((1, Q_LORA)), const((1, KV_LORA)),
            const(wuq.shape), const(wukv.shape),
            const((1, LANES)), const((1, LANES)), const((1, LANES)), const((1, LANES)),
        ],
        out_specs=(row(qk_width, 0), row(qk_width, 0), row(MLA_WIDTH, 0)),
        compiler_params=_params(("parallel",)),
        name="mla_prep",
    )(p, p, p, cos, sin, gq, gkv, wuq, wukv, gqn, gqr, gkn, gkr)


def _mla_attn_kernel(q_ref, k_ref, v_ref, g_ref, o_ref, vt_ref, sa_ref, sb_ref, sd_ref, m_ref, acc_ref,
                     *, t, heads):
    i = pl.program_id(2)
    nblk = v_ref.shape[0] // t
    qw = 2 * LANES
    key_chunk = lax.broadcasted_iota(jnp.int32, (t, t), 0) // CHUNK
    qry_chunk = lax.broadcasted_iota(jnp.int32, (t, t), 1) // CHUNK
    visible = key_chunk <= qry_chunk
    qs = [q_ref[:, h * qw:(h + 1) * qw] for h in range(heads)]

    @pl.when(i == 0)
    def _():
        for h in range(heads):
            for blk in range(nblk):
                vb = v_ref[blk * t:(blk + 1) * t, h * LANES:(h + 1) * LANES]
                vt_ref[h, blk, :MLA_V] = vb.astype(F32).T.astype(BF16)
                vt_ref[h, blk, MLA_V:] = jnp.ones((MLA_SUM_ROWS, t), BF16)

    def issue_scores(j, buf):
        start = pl.multiple_of(j * t, t)
        for h in range(heads):
            buf[h] = _dot_nt(k_ref[pl.ds(start, t), h * qw:(h + 1) * qw], qs[h])

    def absorb_diagonal(buf, j):
        weights = []
        for h in range(heads):
            s = jnp.where(visible, buf[h], NEG_BIG)
            m_new = jnp.max(s, axis=0, keepdims=True)
            m_ref[h] = m_new
            weights.append(jnp.exp2(s - m_new).astype(BF16))
        for h in range(heads):
            acc_ref[h] = _dot(vt_ref[h, j], weights[h])

    def absorb(buf, j):
        half = t // 2
        soft = []
        for h in range(heads):
            for c in range(2):
                cols = slice(c * half, (c + 1) * half)
                s = buf[h, :, cols]
                m = m_ref[h, :, cols]
                m_new = jnp.maximum(m, jnp.max(s, axis=0, keepdims=True))
                m_ref[h, :, cols] = m_new
                soft.append((h, cols, jnp.exp2(m - m_new), jnp.exp2(s - m_new).astype(BF16)))
        for h, cols, alpha, pr in soft:
            acc_ref[h, :, cols] = alpha * acc_ref[h, :, cols] + _dot(vt_ref[h, j], pr)

    issue_scores(i, sd_ref)
    issue_scores(0, sa_ref)
    absorb_diagonal(sd_ref, i)
    last = jnp.maximum(i - 1, 0)

    def pair(n, _):
        issue_scores(2 * n + 1, sb_ref)
        absorb(sa_ref, 2 * n)
        issue_scores(jnp.minimum(2 * n + 2, last), sa_ref)
        absorb(sb_ref, 2 * n + 1)
        return 0

    lax.fori_loop(0, i // 2, pair, 0)

    @pl.when(i % 2 == 1)
    def _():
        absorb(sa_ref, i - 1)

    for h in range(heads):
        hs = slice(h * LANES, (h + 1) * LANES)
        acc = acc_ref[h]
        y = (acc[:MLA_V] / acc[MLA_V:MLA_V + 1]).T
        o_ref[:, hs] = (y * _silu(g_ref[:, hs])).astype(o_ref.dtype)


def _mla_attn(q, k, v, p, batch, seq, t=MLA_TILE, heads=MLA_GROUP):
    nq = seq // t
    qw = heads * 2 * LANES
    vw = heads * LANES
    gate0 = (SEC_MLA + Q_LORA + KV_LORA) // vw
    return pl.pallas_call(
        functools.partial(_mla_attn_kernel, t=t, heads=heads),
        out_shape=jax.ShapeDtypeStruct((batch * seq, MLA_WIDTH), BF16),
        grid=(batch, MLA_HEADS // heads, nq),
        in_specs=[
            pl.BlockSpec((t, qw), lambda b, h, i: (b * nq + i, h)),
            pl.BlockSpec((seq, qw), lambda b, h, i: (b, h)),
            pl.BlockSpec((seq, vw), lambda b, h, i: (b, h)),
            pl.BlockSpec((t, vw), lambda b, h, i: (b * nq + i, gate0 + h)),
        ],
        out_specs=pl.BlockSpec((t, vw), lambda b, h, i: (b * nq + i, h)),
        scratch_shapes=[pltpu.VMEM((heads, nq, MLA_V + MLA_SUM_ROWS, t), BF16)]
        + [pltpu.VMEM((heads, t, t), F32)] * 3
        + [pltpu.VMEM((heads, 1, t), F32)]
        + [pltpu.VMEM((heads, MLA_V + MLA_SUM_ROWS, t), F32)],
        compiler_params=_params(("parallel", "parallel", "arbitrary")),
        name="mla_attn",
    )(q, k, v, p)


def _rw_prep_kernel(cur_ref, prev_ref, smc_ref, smp_ref, mu_ref, mus_ref, wlora_ref, w0_ref, a0_ref,
                    kk_ref, ka_ref, ones_ref, r_ref, lw_ref, k_ref, v_ref, na_ref, b_ref,
                    *, tm, seq):
    i = pl.program_id(0)
    at_start = (i * tm) % seq == 0
    row = lax.broadcasted_iota(jnp.int32, (tm, 1), 0)

    def shifted(c_ref, p_ref, mu):
        cur = c_ref[...]
        last = jnp.where(at_start, 0.0, p_ref[...][7:8, :])
        prev = jnp.where(row == 0, last, pltpu.roll(cur, 1, axis=0))
        return cur + mu * (prev - cur)

    main = shifted(cur_ref, prev_ref, mu_ref[...])
    small = shifted(smc_ref, smp_ref, mus_ref[...])[:, LANES:]
    r = main[:, :RW_WIDTH]
    k = main[:, RW_WIDTH:2 * RW_WIDTH]
    v = main[:, 2 * RW_WIDTH:]

    lane = lax.broadcasted_iota(jnp.int32, (1, LANES), 1)
    lora_in = jnp.where(lane < DECAY_LORA, jnp.tanh(small), small).astype(BF16)
    lora = _dot(lora_in, wlora_ref[...])
    w_log = -_softplus(-(w0_ref[...] + lora[:, :RW_WIDTH])) - 0.5
    a = _sigmoid(a0_ref[...] + lora[:, RW_WIDTH:])

    kk = k * kk_ref[...]
    hi, lo = _split_bf16(kk * kk)
    ss = _dot(hi, ones_ref[...]) + _dot(lo, ones_ref[...])
    kk = kk * jnp.minimum(lax.rsqrt(ss), 1e12)

    r_ref[...] = r
    lw_ref[...] = -jnp.exp(w_log)
    k_ref[...] = k * (1.0 + (a - 1.0) * ka_ref[...])
    v_ref[...] = v
    na_ref[...] = -kk
    b_ref[...] = kk * a


def _rw_prep(p, mu, mus, wlora, w0, a0, k_k, k_a, ones_bd, seq, tm=PREP_ROWS):
    m = p.shape[0]
    vec = lambda width: pl.BlockSpec((1, width), lambda i: (0, 0))
    out = jax.ShapeDtypeStruct((m, RW_WIDTH), F32)
    main_w = 3 * RW_WIDTH
    prev_blk = lambda i: (jnp.maximum(i * (tm // 8) - 1, 0))
    return pl.pallas_call(
        functools.partial(_rw_prep_kernel, tm=tm, seq=seq),
        out_shape=(out,) * 6,
        grid=(m // tm,),
        in_specs=[
            pl.BlockSpec((tm, main_w), lambda i: (i, SEC_RW // main_w)),
            pl.BlockSpec((8, main_w), lambda i: (prev_blk(i), SEC_RW // main_w)),
            pl.BlockSpec((tm, 256), lambda i: (i, SEC_SMALL // 256)),
            pl.BlockSpec((8, 256), lambda i: (prev_blk(i), SEC_SMALL // 256)),
            vec(main_w), vec(256),
            pl.BlockSpec(wlora.shape, lambda i: (0, 0)),
            vec(RW_WIDTH), vec(RW_WIDTH), vec(RW_WIDTH), vec(RW_WIDTH),
            pl.BlockSpec((RW_WIDTH, RW_WIDTH), lambda i: (0, 0)),
        ],
        out_specs=(pl.BlockSpec((tm, RW_WIDTH), lambda i: (i, 0)),) * 6,
        compiler_params=_params(("parallel",)),
        name="rw_prep",
    )(p, p, p, p, mu, mus, wlora, w0, a0, k_k, k_a, ones_bd)


_HI = lax.Precision.HIGHEST

RW_MODE_SCORE = "b1"
RW_MODE_INV = "b1"
RW_MODE_MID = "b1"
RW_MODE_STATE = "b1"


def _mm(a, b, mode, form="nn"):
    f = {"nn": _dot, "nt": _dot_nt, "tn": _dot_tn}[form]
    if mode == "hi":
        return f(a, b, _HI)
    if mode == "b1":
        return f(a.astype(BF16), b.astype(BF16))
    a_hi, a_lo = _split_bf16(a)
    b_hi, b_lo = _split_bf16(b)
    return f(a_hi, b_hi) + (f(a_hi, b_lo) + f(a_lo, b_hi))


def _rw_chunks(probs):
    c = RW_CHUNK
    n = 2 * c
    lane = lax.broadcasted_iota(jnp.int32, (1, LANES), 1)
    first = lane < RW_HEAD_DIM
    ti = lax.broadcasted_iota(jnp.int32, (c, c), 0)
    si = lax.broadcasted_iota(jnp.int32, (c, c), 1)
    tri = (si <= ti).astype(BF16)
    ri = lax.broadcasted_iota(jnp.int32, (n, n), 0)
    ci = lax.broadcasted_iota(jnp.int32, (n, n), 1)
    strict = (ci % c) < (ri % c)
    incl = (ci % c) <= (ri % c)
    eye = ri == ci

    def stack(x):
        return jnp.concatenate([jnp.where(first, x, 0.0), jnp.where(first, 0.0, x)], axis=0)

    splits = [_split_bf16(lw) for _, lw, _, _, _, _ in probs]
    cums = [_dot(tri, h) + _dot(tri, l) for h, l in splits]

    feats = []
    for (r, lw, k, v, na, b), cum in zip(probs, cums):
        total = cum[c - 1:c, :]
        p_inv = jnp.exp(-cum)
        p_end = jnp.exp(total - cum)
        at = stack(na * jnp.exp(cum - lw))
        rt = stack(r * jnp.exp(cum))
        lhs = jnp.concatenate([at, rt], axis=0)
        rhs = jnp.concatenate([stack(b * p_inv), stack(k * p_inv)], axis=0)
        feats.append(dict(at=at, rt=rt, lhs=lhs, rhs=rhs, bh=stack(b * p_end), kh=stack(k * p_end),
                          vs=stack(v), decay=jnp.exp(total)))

    gs = [_mm(f["lhs"], f["rhs"], RW_MODE_SCORE, "nt") for f in feats]
    l_ab = [jnp.where(strict, g[:n, :n], 0.0) for g in gs]
    l_ak = [jnp.where(strict, g[:n, n:], 0.0) for g in gs]
    m_rb = [jnp.where(incl, g[n:, :n], 0.0) for g in gs]
    m_rk = [jnp.where(incl, g[n:, n:], 0.0) for g in gs]

    xs = l_ab
    tinv = [jnp.where(eye, 1.0, 0.0) + x for x in xs]
    for _ in range(int(math.log2(c)) - 1):
        xs = [_mm(x, x, RW_MODE_INV) for x in xs]
        tinv = [t + _mm(t, x, RW_MODE_INV) for t, x in zip(tinv, xs)]

    lv = [_mm(l, f["vs"], RW_MODE_MID) for l, f in zip(l_ak, feats)]
    wu = [_mm(t, jnp.concatenate([f["at"], y], axis=1), RW_MODE_MID)
          for t, f, y in zip(tinv, feats, lv)]
    ro = [_mm(m, w, RW_MODE_MID) for m, w in zip(m_rb, wu)]
    rv = [_mm(m, f["vs"], RW_MODE_MID) for m, f in zip(m_rk, feats)]
    ab = [_mm(w, f["bh"], RW_MODE_STATE, "tn") for w, f in zip(wu, feats)]
    vk = [_mm(f["vs"], f["kh"], RW_MODE_STATE, "tn") for f in feats]

    terms = []
    for i, f in enumerate(feats):
        rh = f["rt"] + ro[i][:, :LANES]
        o0 = ro[i][:, LANES:] + rv[i]
        a_mat = jnp.where(eye, f["decay"], 0.0) + ab[i][:LANES]
        b_mat = ab[i][LANES:] + vk[i]
        terms.append((rh[:c] + rh[c:], o0[:c] + o0[c:], a_mat, b_mat))
    return terms


def _rw_scan_kernel(r_ref, lw_ref, k_ref, v_ref, na_ref, b_ref, g_ref, rk_ref, gng_ref, gnb_ref,
                    o_ref, s_ref):
    @pl.when(pl.program_id(0) == 0)
    def _():
        s_ref[...] = jnp.zeros_like(s_ref)

    lane = lax.broadcasted_iota(jnp.int32, (1, LANES), 1)
    first = lane < RW_HEAD_DIM
    batch = r_ref.shape[0]
    pairs = RW_WIDTH // LANES
    where = [(bi, slice(pr * LANES, (pr + 1) * LANES)) for bi in range(batch) for pr in range(pairs)]
    rows = [slice(q * RW_CHUNK, (q + 1) * RW_CHUNK) for q in range(RW_CHUNKS_PER_STEP)]

    def head_sum(y):
        s_first = jnp.sum(jnp.where(first, y, 0.0), axis=-1, keepdims=True)
        s_second = jnp.sum(jnp.where(first, 0.0, y), axis=-1, keepdims=True)
        return jnp.where(first, s_first, s_second)

    probs = [(r_ref[bi, rs, sl], lw_ref[bi, rs, sl], k_ref[bi, rs, sl], v_ref[bi, rs, sl], na_ref[bi, rs, sl],
              b_ref[bi, rs, sl]) for bi, sl in where for rs in rows]
    terms = _rw_chunks(probs)
    for i, (bi, sl) in enumerate(where):
        state = s_ref[i]
        for q, rs in enumerate(rows):
            n = i * RW_CHUNKS_PER_STEP + q
            rhat, o0, a_mat, b_mat = terms[n]
            o = _mm(rhat, state, RW_MODE_STATE, "nt") + o0
            state = _mm(state, a_mat, RW_MODE_STATE) + b_mat
            r, _, k, v, _, _ = probs[n]
            mu = head_sum(o) / RW_HEAD_DIM
            d = o - mu
            var = head_sum(d * d) / RW_HEAD_DIM
            normed = d * lax.rsqrt(var + RW_GN_EPS) * gng_ref[:, sl] + gnb_ref[:, sl]
            bonus = head_sum(r * k * rk_ref[:, sl]) * v
            o_ref[bi, rs, sl] = ((normed + bonus) * _silu(g_ref[bi, rs, sl])).astype(o_ref.dtype)
        s_ref[i] = state


def _rw_scan(r, lw, k, v, na, b, p, r_k, gn_g, gn_b, batch, seq):
    c = RW_CHUNK * RW_CHUNKS_PER_STEP
    as3d = lambda t: t.reshape(batch, seq, t.shape[-1])
    tok = pl.BlockSpec((batch, c, RW_WIDTH), lambda ci: (0, ci, 0))
    vec = pl.BlockSpec((1, RW_WIDTH), lambda ci: (0, 0))
    gate = pl.BlockSpec((batch, c, RW_WIDTH), lambda ci: (0, ci, SEC_RG // RW_WIDTH))
    out = pl.pallas_call(
        _rw_scan_kernel,
        out_shape=jax.ShapeDtypeStruct((batch, seq, RW_WIDTH), BF16),
        grid=(seq // c,),
        in_specs=[tok] * 6 + [gate, vec, vec, vec],
        out_specs=tok,
        scratch_shapes=[pltpu.VMEM((batch * RW_WIDTH // LANES, LANES, LANES), F32)],
        compiler_params=_params(("arbitrary",)),
        name="rw_scan",
    )(*[as3d(t) for t in (r, lw, k, v, na, b, p)], r_k, gn_g, gn_b)
    return out.reshape(batch * seq, RW_WIDTH)


def _rw_branch(p, mu, w0, w_up, a0, a_up, k_k, k_a, r_k, gn_g, gn_b, batch, seq):
    row = lambda t: t[None, :]
    head_ones = jnp.kron(jnp.eye(RW_HEADS, dtype=F32), jnp.ones((RW_HEAD_DIM, RW_HEAD_DIM), F32)).astype(BF16)
    mu_small = jnp.concatenate([jnp.zeros((LANES,), F32), mu[3 * RW_WIDTH:]])
    zeros = jnp.zeros((DECAY_LORA, RW_WIDTH), F32)
    wlora = jnp.concatenate([jnp.concatenate([w_up, zeros], axis=1),
                             jnp.concatenate([zeros, a_up], axis=1)], axis=0).astype(BF16)
    rr, lw, rk, rv, na, rb = _rw_prep(p, row(mu[:3 * RW_WIDTH]), row(mu_small), wlora, row(w0), row(a0),
                                      row(k_k), row(k_a), head_ones, seq)
    return _rw_scan(rr, lw, rk, rv, na, rb, p, r_k.reshape(1, RW_WIDTH), row(gn_g), row(gn_b), batch, seq)


def _merge_out_kernel(x_ref, ysb_ref, ymla_ref, yrw_ref, wsb_ref, wmla_ref, wrw_ref, g1_ref, g2_ref, g3_ref,
                      wout_ref, o_ref):
    merged = _sigmoid(g1_ref[...]) * _dot(ysb_ref[...], wsb_ref[...])
    merged = merged + _sigmoid(g2_ref[...]) * _dot(ymla_ref[...], wmla_ref[...])
    merged = merged + _sigmoid(g3_ref[...]) * _dot(yrw_ref[...], wrw_ref[...])
    o_ref[...] = x_ref[...] + _dot(merged.astype(BF16), wout_ref[...])


def _merge_out(x2, y_sb, y_mla, y_rw, w_sb, w_mla, w_rw, p, w_out, tm=MERGE_ROWS):
    m = p.shape[0]
    rows = lambda width: pl.BlockSpec((tm, width), lambda i: (i, 0))
    resident = lambda depth: pl.BlockSpec((depth, D_MODEL), lambda i: (0, 0), pipeline_mode=pl.Buffered(1))
    gate = lambda br: pl.BlockSpec((tm, D_MODEL), lambda i: (i, SEC_GATE // D_MODEL + br))
    return pl.pallas_call(
        _merge_out_kernel,
        out_shape=jax.ShapeDtypeStruct((m, D_MODEL), F32),
        grid=(m // tm,),
        in_specs=[rows(D_MODEL), rows(SB_WIDTH), rows(MLA_WIDTH), rows(RW_WIDTH),
                  resident(SB_WIDTH), resident(MLA_WIDTH), resident(RW_WIDTH),
                  gate(0), gate(1), gate(2), resident(D_MODEL)],
        out_specs=rows(D_MODEL),
        compiler_params=_params(("parallel",)),
        name="merge_out",
    )(x2, y_sb, y_mla, y_rw, w_sb, w_mla, w_rw, p, p, p, w_out)


def _pair_rope_cols(t):
    lead = t.shape[:-2]
    t = t.reshape(lead + (MLA_HEADS // 2, 2, 2, HALF_ROPE))
    t = jnp.swapaxes(t, -3, -2)
    return t.reshape(lead + (MLA_HEADS // 2 * LANES,))


def _pair_rope_gain(g):
    g1, g2 = g[:HALF_ROPE], g[HALF_ROPE:]
    return jnp.concatenate([g1, g1, g2, g2])[None, :]


def _rope_tables(seq):
    freqs = ROPE_THETA ** (-jnp.arange(HALF_ROPE, dtype=F32) / HALF_ROPE)
    ang = jnp.arange(seq, dtype=F32)[:, None] * freqs[None, :]
    c, s = jnp.cos(ang), jnp.sin(ang)
    return jnp.concatenate([c, c, c, c], axis=1), jnp.concatenate([-s, -s, s, s], axis=1)


def _mla_branch(p, cos, sin, q_norm_g, kv_norm_g, w_uq, w_ukv, qn_g, kn_g, batch, seq):
    row = lambda t: t[None, :]
    uq = w_uq.reshape(Q_LORA, MLA_HEADS, MLA_QK)
    wuq = jnp.concatenate([uq[:, :, :MLA_NOPE].reshape(Q_LORA, -1), _pair_rope_cols(uq[:, :, MLA_NOPE:])],
                          axis=1).astype(BF16)
    ukv = w_ukv.reshape(KV_LORA, MLA_HEADS, MLA_NOPE + MLA_V)
    wukv = jnp.concatenate([ukv[:, :, :MLA_NOPE].reshape(KV_LORA, -1),
                            ukv[:, :, MLA_NOPE:].reshape(KV_LORA, -1)], axis=1).astype(BF16)
    q, k, v = _mla_prep(p, cos, sin, row(q_norm_g), row(kv_norm_g), wuq, wukv,
                        row(qn_g[:MLA_NOPE]) * MLA_EXP2_SCALE, _pair_rope_gain(qn_g[MLA_NOPE:]) * MLA_EXP2_SCALE,
                        row(kn_g[:MLA_NOPE]), _pair_rope_gain(kn_g[MLA_NOPE:]), seq)
    return _mla_attn(q, k, v, p, batch, seq)


def kernel(x, norm_g, w_in, mla_q_norm_g, mla_kv_norm_g, mla_w_uq, mla_w_ukv, mla_qn_g, mla_kn_g,
           rw_mu, rw_w0, rw_w_up, rw_a0, rw_a_up, rw_k_k, rw_k_a, rw_r_k, rw_gn_g, rw_gn_b,
           w_br_sb, w_br_mla, w_br_rw, w_out):
    batch, seq, _ = x.shape
    depth = w_in.shape[0]
    x2 = x.reshape(batch * seq, D_MODEL)
    cos, sin = _rope_tables(seq)
    w_in_bf16 = _w_in_relayout(w_in)

    for l in range(depth):
        p = _in_proj(x2, norm_g[l][None, :], w_in_bf16, l)
        y_sb = _sb_attn(p, batch, seq)
        y_mla = _mla_branch(p, cos, sin, mla_q_norm_g[l], mla_kv_norm_g[l], mla_w_uq[l], mla_w_ukv[l],
                            mla_qn_g[l], mla_kn_g[l], batch, seq)
        y_rw = _rw_branch(p, rw_mu[l], rw_w0[l], rw_w_up[l], rw_a0[l], rw_a_up[l], rw_k_k[l], rw_k_a[l],
                          rw_r_k[l], rw_gn_g[l], rw_gn_b[l], batch, seq)
        x2 = _merge_out(x2, y_sb, y_mla, y_rw, w_br_sb[l].astype(BF16), w_br_mla[l].astype(BF16),
                        w_br_rw[l].astype(BF16), p, w_out[l].astype(BF16))

    return x2.reshape(batch, seq, D_MODEL)
```
